```python
import math
import jax, jax.numpy as jnp
from jax import lax
import numpy as np

D_MODEL = 1024
BATCH = 16
SEQ = 4096
DEPTH = 4
DEC_BATCH = 16
DEC_SEQ = 2048
PAST_LEN = 128

GRID_W = 64
LRU_WIDTH = D_MODEL // 2
LRU_BLOCKS = 8
LRU_BLOCK = LRU_WIDTH // LRU_BLOCKS
LRU_C = 8.0
CONV_W = 4
CONV_LEFT = 2
ATTN_HEADS = 8
KV_HEADS = 2
HEAD_DIM = 64
GQA_GROUP = ATTN_HEADS // KV_HEADS
ATTN_Q_W = ATTN_HEADS * HEAD_DIM
ATTN_KV_W = KV_HEADS * HEAD_DIM
AXIAL_DIM = HEAD_DIM // 2
AXIAL_FREQS = AXIAL_DIM // 2
ROPE_THETA = 10000.0
Q_BLOCK = 128
HGRN_WIDTH = D_MODEL // 2
HGRN_HEADS = 4
HGRN_DK = HGRN_WIDTH // HGRN_HEADS
HGRN_DV = HGRN_DK
HGRN_CHUNK = 64
N_BRANCH = 3
IN_WIDTHS = (ATTN_Q_W, ATTN_KV_W, ATTN_KV_W,
             LRU_WIDTH, LRU_WIDTH,
             HGRN_WIDTH, HGRN_WIDTH, HGRN_WIDTH, HGRN_WIDTH, HGRN_WIDTH,
             N_BRANCH * D_MODEL)
N_IN = sum(IN_WIDTHS)
N_GROUPS = 4
EXPERTS_PER_GROUP = 8
N_EXPERTS = N_GROUPS * EXPERTS_PER_GROUP
TOP_K = 2
EXPERT_HIDDEN = D_MODEL // 2
MOE_BLOCK = 128
DEEPNORM_ALPHA = (2 * DEPTH) ** 0.25
DEEPNORM_BETA = (8 * DEPTH) ** -0.25
LN_EPS = 1e-5
RMS_EPS = 1e-6

kernel_name = 'hybrid_bidir_encoder_hmoe'


def layernorm(x, g, b):
    xf = x.astype(jnp.float32)
    mu = jnp.mean(xf, -1, keepdims=True)
    var = jnp.mean(jnp.square(xf - mu), -1, keepdims=True)
    return ((xf - mu) * lax.rsqrt(var + LN_EPS) * g + b).astype(x.dtype)


def rmsnorm(x, g):
    xf = x.astype(jnp.float32)
    return (xf * lax.rsqrt(jnp.mean(jnp.square(xf), -1, keepdims=True) + RMS_EPS) * g).astype(x.dtype)


def split_columns(p):
    out, o = [], 0
    for w in IN_WIDTHS:
        out.append(p[..., o:o + w])
        o += w
    return out


def axial_rope_tables(seq_len):
    rows = seq_len // GRID_W
    row = jnp.repeat(jnp.arange(rows), GRID_W).astype(jnp.float32)
    col = jnp.tile(jnp.arange(GRID_W), rows).astype(jnp.float32)
    inv = ROPE_THETA ** (-jnp.arange(AXIAL_FREQS, dtype=jnp.float32) / AXIAL_FREQS)
    ang_r = row[:, None] * inv
    ang_c = col[:, None] * inv
    return (jnp.cos(ang_r), jnp.sin(ang_r), jnp.cos(ang_c), jnp.sin(ang_c))


def _rotate(u, cos, sin):
    u1, u2 = u[..., :AXIAL_FREQS], u[..., AXIAL_FREQS:]
    c = cos[None, :, None, :]
    s = sin[None, :, None, :]
    return jnp.concatenate([u1 * c - u2 * s, u2 * c + u1 * s], -1)


def apply_axial_rope(x, tables):
    cr, sr, cc, sc = tables
    xf = x.astype(jnp.float32)
    out = jnp.concatenate([_rotate(xf[..., :AXIAL_DIM], cr, sr),
                           _rotate(xf[..., AXIAL_DIM:], cc, sc)], -1)
    return out.astype(x.dtype)


def block_attention(q, k, v):
    B, S = q.shape[0], q.shape[1]
    nb = S // Q_BLOCK
    scale = HEAD_DIM ** -0.5
    qb = q.reshape(B, nb, Q_BLOCK, KV_HEADS, GQA_GROUP, HEAD_DIM).transpose(1, 0, 2, 3, 4, 5)

    def one_block(qi):
        s = jnp.einsum('bqkgd,bskd->bkgqs', qi, k).astype(jnp.float32) * scale
        p = jax.nn.softmax(s, axis=-1).astype(v.dtype)
        return jnp.einsum('bkgqs,bskd->bqkgd', p, v)

    o = lax.map(one_block, qb)
    return o.transpose(1, 0, 2, 3, 4, 5).reshape(B, S, ATTN_Q_W)


def conv_centered(x, w, b):
    S = x.shape[1]
    xp = jnp.pad(x, ((0, 0), (CONV_LEFT, CONV_W - 1 - CONV_LEFT), (0, 0)))
    return sum(xp[:, j:j + S] * w[j] for j in range(CONV_W)) + b


def _linear_combine(left, right):
    a1, b1 = left
    a2, b2 = right
    return a1 * a2, a2 * b1 + b2


def rglru_branch(xl, conv_w, conv_b, wa, ba, wi, bi, lam):
    xc = conv_centered(xl, conv_w, conv_b)
    B, S, W = xc.shape
    xb = xc.reshape(B, S, LRU_BLOCKS, LRU_BLOCK)

    def direction(d, reverse):
        r = jax.nn.sigmoid(jnp.einsum('bsni,nij->bsnj', xb, wa[d]).reshape(B, S, W) + ba[d])
        i = jax.nn.sigmoid(jnp.einsum('bsni,nij->bsnj', xb, wi[d]).reshape(B, S, W) + bi[d])
        log_a = -LRU_C * r.astype(jnp.float32) * jax.nn.softplus(-lam[d].astype(jnp.float32))
        a = jnp.exp(log_a)
        u = jnp.sqrt(-jnp.expm1(2.0 * log_a)) * (i * xc).astype(jnp.float32)
        _, h = lax.associative_scan(_linear_combine, (a, u), axis=1, reverse=reverse)
        return h

    return (direction(0, False) + direction(1, True)).astype(xl.dtype)


def hgrn2_lower_bounds(logits):
    p = jax.nn.softmax(logits.astype(jnp.float32), axis=0)
    return jnp.cumsum(p, axis=0) - p[0]


def _hgrn2_scan(q, k, g, v):
    B, S, H, dk = q.shape
    nc = S // HGRN_CHUNK

    def chunks(a):
        return a.reshape(B, nc, HGRN_CHUNK, H, a.shape[-1]).transpose(1, 0, 3, 2, 4)

    mask = jnp.tril(jnp.ones((HGRN_CHUNK, HGRN_CHUNK), bool))[None, None, :, :, None]

    def step(state, inp):
        qc, kc, gc, vc = inp
        b = jnp.cumsum(gc, axis=2)
        inter = jnp.einsum('bhtk,bhkv->bhtv', qc * jnp.exp(b), state)
        diff = b[:, :, :, None, :] - b[:, :, None, :, :]
        decay = jnp.where(mask, jnp.exp(jnp.where(mask, diff, 0.0)), 0.0)
        scores = jnp.einsum('bhtsk,bhtk,bhsk->bhts', decay, qc, kc)
        intra = jnp.einsum('bhts,bhsv->bhtv', scores, vc)
        b_last = b[:, :, -1, :]
        new_state = jnp.exp(b_last)[..., None] * state + jnp.einsum(
            'bhsk,bhsv->bhkv', kc * jnp.exp(b_last[:, :, None, :] - b), vc)
        return new_state, inter + intra

    s0 = jnp.zeros((B, H, dk, v.shape[-1]), jnp.float32)
    _, o = lax.scan(step, s0, (chunks(q), chunks(k), chunks(g), chunks(v)))
    return o.transpose(1, 0, 3, 2, 4).reshape(B, S, H, v.shape[-1])


def hgrn2_branch(hq, hff, hfb, hi, hg, lb, norm_gain):
    B, S, _ = hq.shape

    def heads(a):
        return a.astype(jnp.float32).reshape(B, S, HGRN_HEADS, HGRN_DK)

    q = jax.nn.silu(heads(hq))
    v = heads(hi)
    lbh = lb.reshape(HGRN_HEADS, HGRN_DK).astype(jnp.float32)

    def gates(raw):
        f = lbh + (1.0 - lbh) * jax.nn.sigmoid(heads(raw))
        return 1.0 - f, jnp.log(f)

    k_f, g_f = gates(hff)
    k_b, g_b = gates(hfb)
    flip = lambda a: jnp.flip(a, axis=1)
    o = _hgrn2_scan(q, k_f, g_f, v) + flip(_hgrn2_scan(flip(q), flip(k_b), flip(g_b), flip(v)))
    o = rmsnorm(o, norm_gain) * jax.nn.silu(heads(hg))
    return o.reshape(B, S, HGRN_WIDTH).astype(hq.dtype)


def hybrid_mixer(h, rope, lb, w_in, q_gain, k_gain, w_attn_o, conv_w, conv_b, wa, ba, wi, bi, lam,
                 w_lru_o, norm_gain, w_hgrn_o, w_out):
    B, S, D = h.shape
    aq, ak, av, lx, lg, hq, hff, hfb, hi, hg, mg = split_columns(h @ w_in)
    q = apply_axial_rope(rmsnorm(aq.reshape(B, S, ATTN_HEADS, HEAD_DIM), q_gain), rope)
    k = apply_axial_rope(rmsnorm(ak.reshape(B, S, KV_HEADS, HEAD_DIM), k_gain), rope)
    v = av.reshape(B, S, KV_HEADS, HEAD_DIM)
    y_attn = block_attention(q, k, v) @ w_attn_o
    y_lru = (rglru_branch(lx, conv_w, conv_b, wa, ba, wi, bi, lam) * jax.nn.gelu(lg)) @ w_lru_o
    y_hgrn = hgrn2_branch(hq, hff, hfb, hi, hg, lb, norm_gain) @ w_hgrn_o
    gates = jax.nn.sigmoid(mg.reshape(B, S, N_BRANCH, D))
    merged = gates[:, :, 0] * y_attn + gates[:, :, 1] * y_lru + gates[:, :, 2] * y_hgrn
    return merged @ w_out


def hierarchical_moe(h, rg_w, re_w, w13, w2):
    B, S, D = h.shape
    N = B * S
    t = h.reshape(N, D)
    lg = (t @ rg_w).astype(jnp.float32)
    pg = jax.nn.softmax(lg, -1)
    gsel = jnp.argmax(lg, -1)
    gp = jnp.take_along_axis(pg, gsel[:, None], 1)[:, 0]
    le = jnp.einsum('nd,gde->nge', t, re_w).astype(jnp.float32)
    le = jnp.take_along_axis(le, gsel[:, None, None], 1)[:, 0]
    pe = jax.nn.softmax(le, -1)
    topw, topi = lax.top_k(pe, TOP_K)
    topw = topw / jnp.sum(topw, -1, keepdims=True)
    w = gp[:, None] * topw
    eid = gsel[:, None] * EXPERTS_PER_GROUP + topi
    A = N * TOP_K
    eid_f = eid.reshape(A).astype(jnp.int32)
    tok_f = jnp.repeat(jnp.arange(N, dtype=jnp.int32), TOP_K)
    w_f = w.reshape(A)
    order = jnp.argsort(eid_f)
    e_s = eid_f[order]
    counts = jnp.bincount(eid_f, length=N_EXPERTS)
    starts = jnp.cumsum(counts) - counts
    pcounts = (counts + MOE_BLOCK - 1) // MOE_BLOCK * MOE_BLOCK
    pends = jnp.cumsum(pcounts)
    pstarts = pends - pcounts
    dest = pstarts[e_s] + jnp.arange(A, dtype=jnp.int32) - starts[e_s]
    P = A + N_EXPERTS * MOE_BLOCK
    nb = P // MOE_BLOCK
    buf_tok = jnp.full((P,), N, jnp.int32).at[dest].set(tok_f[order])
    buf_w = jnp.zeros((P,), jnp.float32).at[dest].set(w_f[order])
    blk_e = jnp.clip(jnp.searchsorted(pends, jnp.arange(nb) * MOE_BLOCK, side='right'), 0, N_EXPERTS - 1)
    t_pad = jnp.concatenate([t, jnp.zeros((1, D), t.dtype)], 0)
    xs = t_pad[buf_tok].reshape(nb, MOE_BLOCK, D)

    def expert_block(args):
        xb, e = args
        a, b = jnp.split(xb @ w13[e], 2, axis=-1)
        return (jax.nn.silu(a) * b) @ w2[e]

    ys = lax.map(expert_block, (xs, blk_e)).reshape(P, D)
    out = jnp.zeros((N + 1, D), ys.dtype).at[buf_tok].add(ys * buf_w[:, None].astype(ys.dtype))[:N]
    return out.reshape(B, S, D).astype(h.dtype)


def trunk(x, c, ada_w, ada_b, w_in, attn_q_gain, attn_k_gain, w_attn_o, lru_conv_w, lru_conv_b,
          lru_wa, lru_ba, lru_wi, lru_bi, lru_lambda, w_lru_o, hgrn_lb_logits, hgrn_norm_gain, w_hgrn_o,
          w_out, ln1_g, ln1_b, router_group_w, router_expert_w, expert_w13, expert_w2, ln2_g, ln2_b):
    rope = axial_rope_tables(x.shape[1])
    lbs = hgrn2_lower_bounds(hgrn_lb_logits)
    for l in range(DEPTH):
        mod = (jax.nn.silu(c) @ ada_w[l] + ada_b[l])[:, None, :]
        sh1, sc1, g1, sh2, sc2, g2 = jnp.split(mod, 6, axis=-1)
        h = x * (1.0 + sc1) + sh1
        mix = hybrid_mixer(h, rope, lbs[l], w_in[l], attn_q_gain[l], attn_k_gain[l], w_attn_o[l],
                           lru_conv_w[l], lru_conv_b[l], lru_wa[l], lru_ba[l], lru_wi[l], lru_bi[l],
                           lru_lambda[l], w_lru_o[l], hgrn_norm_gain[l], w_hgrn_o[l], w_out[l])
        x = layernorm(DEEPNORM_ALPHA * x + (1.0 + g1) * mix, ln1_g[l], ln1_b[l])
        h = x * (1.0 + sc2) + sh2
        ffn = hierarchical_moe(h, router_group_w[l], router_expert_w[l], expert_w13[l], expert_w2[l])
        x = layernorm(DEEPNORM_ALPHA * x + (1.0 + g2) * ffn, ln2_g[l], ln2_b[l])
    return x


def setup_inputs(seed: int = 0) -> dict:
    key = jax.random.key(seed)
    ks = jax.random.split(key, 40)
    f32 = jnp.float32
    L, D = DEPTH, D_MODEL

    def nrm(k, shape, s):
        return jax.random.normal(k, shape, f32) * s

    u = jax.random.uniform(ks[16], (L, 2, LRU_WIDTH), f32)
    a = (0.9 + 0.099 * u) ** (1.0 / LRU_C)
    lru_lambda = jnp.log(a) - jnp.log1p(-a)
    return {
        'x_prompt': nrm(ks[0], (BATCH, SEQ, D), 1.0),
        'x_sample': nrm(ks[1], (DEC_BATCH, DEC_SEQ, D), 1.0),
        'c_prompt': nrm(ks[2], (BATCH, D), 1.0),
        'c_sample': nrm(ks[3], (DEC_BATCH, D), 1.0),
        'ada_w': nrm(ks[4], (L, D, 6 * D), 0.1 * D ** -0.5),
        'ada_b': nrm(ks[5], (L, 6 * D), 0.01),
        'w_in': nrm(ks[6], (L, D, N_IN), D ** -0.5),
        'attn_q_gain': 1.0 + nrm(ks[7], (L, HEAD_DIM), 0.02),
        'attn_k_gain': 1.0 + nrm(ks[8], (L, HEAD_DIM), 0.02),
        'w_attn_o': nrm(ks[9], (L, ATTN_Q_W, D), ATTN_Q_W ** -0.5 * DEEPNORM_BETA),
        'lru_conv_w': nrm(ks[10], (L, CONV_W, LRU_WIDTH), CONV_W ** -0.5),
        'lru_conv_b': nrm(ks[11], (L, LRU_WIDTH), 0.01),
        'lru_wa': nrm(ks[12], (L, 2, LRU_BLOCKS, LRU_BLOCK, LRU_BLOCK), LRU_BLOCK ** -0.5),
        'lru_ba': nrm(ks[13], (L, 2, LRU_WIDTH), 0.01),
        'lru_wi': nrm(ks[14], (L, 2, LRU_BLOCKS, LRU_BLOCK, LRU_BLOCK), LRU_BLOCK ** -0.5),
        'lru_bi': nrm(ks[15], (L, 2, LRU_WIDTH), 0.01),
        'lru_lambda': lru_lambda,
        'w_lru_o': nrm(ks[17], (L, LRU_WIDTH, D), LRU_WIDTH ** -0.5 * DEEPNORM_BETA),
        'hgrn_lb_logits': nrm(ks[18], (L, HGRN_WIDTH), 0.5),
        'hgrn_norm_gain': 1.0 + nrm(ks[19], (L, HGRN_DV), 0.02),
        'w_hgrn_o': nrm(ks[20], (L, HGRN_WIDTH, D), HGRN_WIDTH ** -0.5 * DEEPNORM_BETA),
        'w_out': nrm(ks[21], (L, D, D), D ** -0.5 * DEEPNORM_BETA),
        'ln1_g': 1.0 + nrm(ks[22], (L, D), 0.02),
        'ln1_b': nrm(ks[23], (L, D), 0.01),
        'router_group_w': nrm(ks[24], (L, D, N_GROUPS), D ** -0.5),
        'router_expert_w': nrm(ks[25], (L, N_GROUPS, D, EXPERTS_PER_GROUP), D ** -0.5),
        'expert_w13': nrm(ks[26], (L, N_EXPERTS, D, 2 * EXPERT_HIDDEN), D ** -0.5),
        'expert_w2': nrm(ks[27], (L, N_EXPERTS, EXPERT_HIDDEN, D), EXPERT_HIDDEN ** -0.5 * DEEPNORM_BETA),
        'ln2_g': 1.0 + nrm(ks[28], (L, D), 0.02),
        'ln2_b': nrm(ks[29], (L, D), 0.01),
    }


def reference(x_prompt, x_sample, c_prompt, c_sample, ada_w, ada_b, w_in, attn_q_gain, attn_k_gain,
              w_attn_o, lru_conv_w, lru_conv_b, lru_wa, lru_ba, lru_wi, lru_bi, lru_lambda, w_lru_o,
              hgrn_lb_logits, hgrn_norm_gain, w_hgrn_o, w_out, ln1_g, ln1_b, router_group_w,
              router_expert_w, expert_w13, expert_w2, ln2_g, ln2_b):
    params = (ada_w, ada_b, w_in, attn_q_gain, attn_k_gain, w_attn_o, lru_conv_w, lru_conv_b,
              lru_wa, lru_ba, lru_wi, lru_bi, lru_lambda, w_lru_o, hgrn_lb_logits, hgrn_norm_gain,
              w_hgrn_o, w_out, ln1_g, ln1_b, router_group_w, router_expert_w, expert_w13, expert_w2,
              ln2_g, ln2_b)
    y_prompt = trunk(x_prompt, c_prompt, *params)
    y_sample = trunk(x_sample, c_sample, *params)
    return (y_prompt, y_sample)
```

```python
import functools
import math

import numpy as np
import jax
import jax.numpy as jnp
from jax import lax
from jax.experimental import pallas as pl
from jax.experimental.pallas import tpu as pltpu

F32 = jnp.float32
BF16 = jnp.bfloat16

GRID_W = 64
CONV_W = 4
CONV_LEFT = 2
LRU_C = 8.0
ATTN_HEADS = 8
KV_HEADS = 2
HEAD_DIM = 64
GQA_GROUP = ATTN_HEADS // KV_HEADS
AXIAL_DIM = HEAD_DIM // 2
AXIAL_FREQS = AXIAL_DIM // 2
ROPE_THETA = 10000.0
HGRN_HEADS = 4
N_BRANCH = 3
TOP_K = 2
LN_EPS = 1e-5
RMS_EPS = 1e-6

V7X_LANES = 128
V7X_SUBLANES = 8
V7X_VMEM_LIMIT_BYTES = 56 * 1024 * 1024

TM_INPROJ = 512
TM_PREP = 512
TQ_ATTN = 256
T_LRU = 256
C_HGRN = 128
TM_MERGE = 512
R_MOE = 128
TC_MOE = 128


def _cparams(sem):
    return pltpu.CompilerParams(dimension_semantics=sem, vmem_limit_bytes=V7X_VMEM_LIMIT_BYTES)


def _dot(a, b):
    return jnp.dot(a, b, preferred_element_type=F32)


def _dot_nt(a, b):
    return lax.dot_general(a, b, (((1,), (1,)), ((), ())), preferred_element_type=F32)


def _dot_hi(a, b):
    return jnp.dot(a, b, preferred_element_type=F32, precision=lax.Precision.HIGHEST)


def _split2(x):
    hi = x.astype(BF16)
    lo = (x - hi.astype(F32)).astype(BF16)
    return hi, lo


def _dot_x_sel(x, m):
    hi, lo = _split2(x)
    return _dot(hi, m) + _dot(lo, m)


def _dot_sel_x(m, x):
    hi, lo = _split2(x)
    return _dot(m, hi) + _dot(m, lo)


def _sigmoid(x):
    return 1.0 / (1.0 + jnp.exp(-x))


def _silu(x):
    return x * _sigmoid(x)


def _layernorm(y, g, b):
    mu = jnp.mean(y, axis=-1, keepdims=True)
    d = y - mu
    var = jnp.mean(d * d, axis=-1, keepdims=True)
    return d * lax.rsqrt(var + LN_EPS) * g + b


def _ada_kernel(c_ref, w_ref, b_ref, o_ref):
    c = c_ref[...]
    o_ref[0] = _dot_hi(_silu(c), w_ref[0]) + b_ref[0]


def _ada_mod(c, ada_w, ada_b):
    L, D, N6 = ada_w.shape
    B = c.shape[0]
    tn = N6 // 6
    out = pl.pallas_call(
        _ada_kernel,
        grid=(L, N6 // tn),
        in_specs=[pl.BlockSpec((B, D), lambda l, j: (0, 0)),
                  pl.BlockSpec((1, D, tn), lambda l, j: (l, 0, j)),
                  pl.BlockSpec((1, 1, tn), lambda l, j: (l, 0, j))],
        out_specs=pl.BlockSpec((1, B, tn), lambda l, j: (l, 0, j)),
        out_shape=jax.ShapeDtypeStruct((L, B, N6), F32),
        compiler_params=_cparams(("arbitrary", "arbitrary")),
        name="ada_mod",
    )(c, ada_w, ada_b.reshape(L, 1, N6))
    return out.reshape(L, B, 6, N6 // 6)


def _inproj_kernel(x_ref, mod_ref, w_ref, *o_refs, chunk):
    m = mod_ref[0]
    h = (x_ref[0] * (1.0 + m[1:2]) + m[0:1]).astype(BF16)
    off = 0
    for o_ref in o_refs:
        n = o_ref.shape[-1]
        for c0 in range(0, n, chunk):
            c1 = min(c0 + chunk, n)
            o_ref[0, :, c0:c1] = _dot(h, w_ref[:, off + c0:off + c1]).astype(o_ref.dtype)
        off += n


def _inproj(x, mod_l, w_in_bf, widths):
    B, S, D = x.shape
    N = w_in_bf.shape[1]
    tm = min(TM_INPROJ, S)
    out_shape = [jax.ShapeDtypeStruct((B, S, n), BF16) for n in widths]
    out_specs = [pl.BlockSpec((1, tm, n), lambda b, i: (b, i, 0)) for n in widths]
    return pl.pallas_call(
        functools.partial(_inproj_kernel, chunk=512),
        grid=(B, S // tm),
        in_specs=[pl.BlockSpec((1, tm, D), lambda b, i: (b, i, 0)),
                  pl.BlockSpec((1, 6, D), lambda b, i: (b, 0, 0)),
                  pl.BlockSpec((D, N), lambda b, i: (0, 0), pipeline_mode=pl.Buffered(1))],
        out_specs=out_specs,
        out_shape=out_shape,
        compiler_params=_cparams(("parallel", "parallel")),
        name="in_proj",
    )(x, mod_l, w_in_bf)


def _rope_tables(S):
    t = np.arange(S)
    row = (t // GRID_W).astype(np.float64)
    col = (t % GRID_W).astype(np.float64)
    inv = ROPE_THETA ** (-np.arange(AXIAL_FREQS, dtype=np.float64) / AXIAL_FREQS)
    ang_r = (row[:, None].astype(np.float32) * inv.astype(np.float32)[None, :]).astype(np.float32)
    ang_c = (col[:, None].astype(np.float32) * inv.astype(np.float32)[None, :]).astype(np.float32)
    cos_h = np.concatenate([np.cos(ang_r), np.cos(ang_r), np.cos(ang_c), np.cos(ang_c)], -1)
    sin_h = np.concatenate([np.sin(ang_r), np.sin(ang_r), np.sin(ang_c), np.sin(ang_c)], -1)
    cos_t = np.tile(cos_h, (1, ATTN_HEADS)).astype(np.float32)
    sin_t = np.tile(sin_h, (1, ATTN_HEADS)).astype(np.float32)
    return cos_t, sin_t


def _head_consts():
    W = ATTN_HEADS * HEAD_DIM
    d = np.arange(W)
    bind = (d[:, None] // HEAD_DIM == d[None, :] // HEAD_DIM).astype(np.float32) / HEAD_DIM
    psw = np.zeros((W, W), np.float32)
    lowhalf = (d % AXIAL_DIM) < AXIAL_FREQS
    for j in d:
        if lowhalf[j]:
            psw[j + AXIAL_FREQS, j] = -1.0
        else:
            psw[j - AXIAL_FREQS, j] = 1.0
    return bind, psw


def _prep_kernel(pq_ref, pkv_ref, cos_ref, sin_ref, gq_ref, gk_ref, bind_ref, psw_ref,
                 q_ref, kt_ref, v_ref):
    KW = KV_HEADS * HEAD_DIM
    cos = cos_ref[...]
    sin = sin_ref[...]

    def norm_rope(u, gain, w):
        ms = _dot_x_sel(u * u, bind_ref[:w, :w])
        un = u * lax.rsqrt(ms + RMS_EPS) * gain
        return un * cos[:, :w] + _dot_x_sel(un, psw_ref[:w, :w]) * sin[:, :w]

    q = norm_rope(pq_ref[0].astype(F32), gq_ref[...], ATTN_HEADS * HEAD_DIM)
    q_ref[0] = (q * (HEAD_DIM ** -0.5)).astype(BF16)
    kv = pkv_ref[0]
    k = norm_rope(kv[:, :KW].astype(F32), gk_ref[...], KW)
    kt_ref[0] = k.T.astype(BF16)
    for g in range(KV_HEADS):
        v_ref[0, g] = kv[:, KW + g * HEAD_DIM:KW + (g + 1) * HEAD_DIM]


def _attn_prep(pq, pkv, cos_t, sin_t, gq, gk, bind, psw):
    B, S, QW = pq.shape
    KW = KV_HEADS * HEAD_DIM
    tm = min(TM_PREP, S)
    return pl.pallas_call(
        _prep_kernel,
        grid=(B, S // tm),
        in_specs=[pl.BlockSpec((1, tm, QW), lambda b, i: (b, i, 0)),
                  pl.BlockSpec((1, tm, 2 * KW), lambda b, i: (b, i, 0)),
                  pl.BlockSpec((tm, QW), lambda b, i: (i, 0)),
                  pl.BlockSpec((tm, QW), lambda b, i: (i, 0)),
                  pl.BlockSpec((1, QW), lambda b, i: (0, 0)),
                  pl.BlockSpec((1, KW), lambda b, i: (0, 0)),
                  pl.BlockSpec((QW, QW), lambda b, i: (0, 0)),
                  pl.BlockSpec((QW, QW), lambda b, i: (0, 0))],
        out_specs=[pl.BlockSpec((1, tm, QW), lambda b, i: (b, i, 0)),
                   pl.BlockSpec((1, KW, tm), lambda b, i: (b, 0, i)),
                   pl.BlockSpec((1, KV_HEADS, tm, HEAD_DIM), lambda b, i: (b, 0, i, 0))],
        out_shape=[jax.ShapeDtypeStruct((B, S, QW), BF16),
                   jax.ShapeDtypeStruct((B, KW, S), BF16),
                   jax.ShapeDtypeStruct((B, KV_HEADS, S, HEAD_DIM), BF16)],
        compiler_params=_cparams(("parallel", "parallel")),
        name="attn_prep",
    )(pq, pkv, cos_t, sin_t, gq, gk, bind, psw)


def _attn_kernel(q_ref, kt_ref, v_ref, o_ref):
    kt = kt_ref[0]
    v = v_ref[0, 0]
    q = q_ref[0]
    outs = []
    for h in range(GQA_GROUP):
        qh = q[:, h * HEAD_DIM:(h + 1) * HEAD_DIM]
        s = _dot(qh, kt)
        m = jnp.max(s, axis=-1, keepdims=True)
        p = jnp.exp(s - m)
        l = jnp.sum(p, axis=-1, keepdims=True)
        o = _dot(p.astype(BF16), v)
        outs.append(o / l)
    o_ref[0] = jnp.concatenate(outs, axis=-1).astype(o_ref.dtype)


def _attention(q, kt, v):
    B, S, QW = q.shape
    GW = GQA_GROUP * HEAD_DIM
    tq = min(TQ_ATTN, S)
    return pl.pallas_call(
        _attn_kernel,
        grid=(B, KV_HEADS, S // tq),
        in_specs=[pl.BlockSpec((1, tq, GW), lambda b, g, i: (b, i, g)),
                  pl.BlockSpec((1, HEAD_DIM, S), lambda b, g, i: (b, g, 0)),
                  pl.BlockSpec((1, 1, S, HEAD_DIM), lambda b, g, i: (b, g, 0, 0))],
        out_specs=pl.BlockSpec((1, tq, GW), lambda b, g, i: (b, i, g)),
        out_shape=jax.ShapeDtypeStruct((B, S, QW), BF16),
        compiler_params=_cparams(("parallel", "parallel", "parallel")),
        name="attention",
    )(q, kt, v)


def _log1p(y):
    u = 1.0 + y
    return jnp.where(u == 1.0, y, jnp.log(u) * (y / (u - 1.0)))


def _softplus(z):
    return jnp.maximum(z, 0.0) + _log1p(jnp.exp(-jnp.abs(z)))


def _neg_expm1(z):
    u = jnp.exp(z)
    lu = jnp.log(u)
    em1 = jnp.where(u == 1.0, z, (u - 1.0) * (z / jnp.where(lu == 0.0, 1.0, lu)))
    em1 = jnp.where(z < -0.5, u - 1.0, em1)
    return -em1


def _gelu_tanh(x):
    return 0.5 * x * (1.0 + jnp.tanh(math.sqrt(2.0 / math.pi) * (x + 0.044715 * (x * x * x))))


def _lru_kernel(*refs, reverse, T):
    if reverse:
        (x_ref, xp_ref, xn_ref, cw_ref, cb_ref, wbd_ref, gb_ref, lam_ref, hf_ref, lg_ref,
         o_ref, carry_ref) = refs
    else:
        (x_ref, xp_ref, xn_ref, cw_ref, cb_ref, wbd_ref, gb_ref, lam_ref,
         o_ref, carry_ref) = refs
    i = pl.program_id(1)
    nt = pl.num_programs(1)
    tile = nt - 1 - i if reverse else i
    W = x_ref.shape[-1]

    @pl.when(i == 0)
    def _():
        carry_ref[...] = jnp.zeros_like(carry_ref)

    x = x_ref[0].astype(F32)
    prev = jnp.where(tile == 0, 0.0, xp_ref[0].astype(F32))
    nxt = jnp.where(tile == nt - 1, 0.0, xn_ref[0].astype(F32))
    row8 = lax.broadcasted_iota(jnp.int32, (V7X_SUBLANES, W), 0)

    def shifted(k):
        if k == 0:
            return x
        r = pltpu.roll(x, (-k) % T, axis=0)
        if k < 0:
            fill = pltpu.roll(prev, (-k) % V7X_SUBLANES, axis=0)
            head = jnp.where(row8 < -k, fill, r[:V7X_SUBLANES])
            return jnp.concatenate([head, r[V7X_SUBLANES:]], axis=0)
        fill = pltpu.roll(nxt, (-k) % V7X_SUBLANES, axis=0)
        tail = jnp.where(row8 >= V7X_SUBLANES - k, fill, r[T - V7X_SUBLANES:])
        return jnp.concatenate([r[:T - V7X_SUBLANES], tail], axis=0)

    cw = cw_ref[...]
    xc = cb_ref[...] + sum(shifted(j - CONV_LEFT) * cw[j:j + 1] for j in range(CONV_W))

    gates = _dot(xc.astype(BF16), wbd_ref[0]) + gb_ref[0]
    r = _sigmoid(gates[:, :W])
    ig = _sigmoid(gates[:, W:])
    log_a = (-LRU_C) * r * _softplus(-lam_ref[0])
    a = jnp.exp(log_a)
    u = jnp.sqrt(_neg_expm1(2.0 * log_a)) * (ig * xc)

    row = lax.broadcasted_iota(jnp.int32, (T, W), 0)
    A, U = a, u
    d = 1
    while d < T:
        if reverse:
            As = pltpu.roll(A, T - d, axis=0)
            Us = pltpu.roll(U, T - d, axis=0)
            msk = row < T - d
        else:
            As = pltpu.roll(A, d, axis=0)
            Us = pltpu.roll(U, d, axis=0)
            msk = row >= d
        U = jnp.where(msk, A * Us + U, U)
        A = jnp.where(msk, A * As, A)
        d *= 2
    h = A * carry_ref[...] + U
    if reverse:
        carry_ref[...] = h[0:1]
        o_ref[0] = ((hf_ref[0] + h) * _gelu_tanh(lg_ref[0].astype(F32))).astype(o_ref.dtype)
    else:
        carry_ref[...] = h[T - 1:T]
        o_ref[0] = h


def _lru_pass(plru, conv_w, conv_b, wbd, gb, lam, hf, reverse):
    B, S, W2 = plru.shape
    W = W2 // 2
    T = min(T_LRU, S)
    nt = S // T
    r8 = T // V7X_SUBLANES
    nb8 = S // V7X_SUBLANES
    d = 1 if reverse else 0

    def tmap(i):
        return nt - 1 - i if reverse else i

    in_specs = [
        pl.BlockSpec((1, T, W), lambda b, i: (b, tmap(i), 0)),
        pl.BlockSpec((1, V7X_SUBLANES, W), lambda b, i: (b, jnp.maximum(tmap(i) * r8 - 1, 0), 0)),
        pl.BlockSpec((1, V7X_SUBLANES, W), lambda b, i: (b, jnp.minimum((tmap(i) + 1) * r8, nb8 - 1), 0)),
        pl.BlockSpec((CONV_W, W), lambda b, i: (0, 0)),
        pl.BlockSpec((1, W), lambda b, i: (0, 0)),
        pl.BlockSpec((1, W, 2 * W), lambda b, i: (d, 0, 0)),
        pl.BlockSpec((1, 1, 2 * W), lambda b, i: (d, 0, 0)),
        pl.BlockSpec((1, 1, W), lambda b, i: (d, 0, 0)),
    ]
    args = [plru, plru, plru, conv_w, conv_b, wbd, gb, lam]
    if reverse:
        in_specs += [pl.BlockSpec((1, T, W), lambda b, i: (b, tmap(i), 0)),
                     pl.BlockSpec((1, T, W), lambda b, i: (b, tmap(i), 1))]
        args += [hf, plru]
    return pl.pallas_call(
        functools.partial(_lru_kernel, reverse=reverse, T=T),
        grid=(B, nt),
        in_specs=in_specs,
        out_specs=pl.BlockSpec((1, T, W), lambda b, i: (b, tmap(i), 0)),
        out_shape=jax.ShapeDtypeStruct((B, S, W), BF16 if reverse else F32),
        scratch_shapes=[pltpu.VMEM((1, W), F32)],
        compiler_params=_cparams(("parallel", "arbitrary")),
        name="lru_bwd" if reverse else "lru_fwd",
    )(*args)


def _hgrn_consts(C, reverse):
    nl = int(round(math.log2(C)))
    t = np.arange(C)
    a_rows, masks, upper = [], [], []
    for lvl in range(nl):
        m = C >> (lvl + 1)
        blk = t // (2 * m)
        up = (t % (2 * m)) >= m
        mid = blk * 2 * m + m
        r = t[None, :]
        a_up = (r >= mid[:, None]) & (r <= t[:, None])
        a_lo = (r > t[:, None]) & (r <= mid[:, None] - 1)
        a_rows.append(np.where(up[:, None], a_up, a_lo))
        masks.append((blk[:, None] == blk[None, :]) & up[:, None] & (~up)[None, :])
        upper.append(up)
    a_rows.append(t[None, :] <= t[:, None])
    a_rows.append(t[None, :] > t[:, None])
    a_rows.append(np.ones((V7X_SUBLANES, C), bool))
    masks.append(np.eye(C, dtype=bool))
    if reverse:
        a_rows = [a[::-1, ::-1] if a.shape[0] == C else a for a in a_rows]
        masks = [mm[::-1, ::-1] for mm in masks]
        upper = [u[::-1] for u in upper]
    amat = np.concatenate(a_rows, 0).astype(np.float32)
    msk = np.stack(masks, 0).astype(np.float32)
    upv = np.stack(upper, 0).astype(np.float32)[:, :, None]
    return nl, amat, msk, upv


def _hgrn_kernel(*refs, layer, reverse, C, NL):
    if reverse:
        (hq_ref, hf_ref, hi_ref, lbl_ref, amat_ref, msk_ref, up_ref, of_ref, hg_ref, gain_ref,
         o_ref, st_ref) = refs
    else:
        (hq_ref, hf_ref, hi_ref, lbl_ref, amat_ref, msk_ref, up_ref, o_ref, st_ref) = refs
    H = HGRN_HEADS
    W = hq_ref.shape[-1]
    dk = W // H

    @pl.when(pl.program_id(1) == 0)
    def _():
        st_ref[...] = jnp.zeros_like(st_ref)

    lgt = lbl_ref[...]
    e = jnp.exp(lgt - jnp.max(lgt, axis=0, keepdims=True))
    p = e / jnp.sum(e, axis=0, keepdims=True)
    lb = jnp.zeros((1, W), F32)
    for j in range(1, layer + 1):
        lb = lb + p[j:j + 1]

    q = _silu(hq_ref[0].astype(F32))
    f = lb + (1.0 - lb) * _sigmoid(hf_ref[0].astype(F32))
    k = 1.0 - f
    g = jnp.log(f)
    v = hi_ref[0].astype(F32)

    ex = jnp.exp(_dot_sel_x(amat_ref[...], g))
    ex_b = ex[NL * C:(NL + 1) * C]
    ex_rem = ex[(NL + 1) * C:(NL + 2) * C]
    ex_tot = ex[(NL + 2) * C:(NL + 2) * C + 1]
    xs = [(jnp.where(up_ref[l] > 0.5, q, k) * ex[l * C:(l + 1) * C]).astype(BF16) for l in range(NL)]
    qk = q * k
    qb = (q * ex_b).astype(BF16)
    kr = (k * ex_rem).astype(BF16)

    outs = []
    for h in range(H):
        sl = slice(h * dk, (h + 1) * dk)
        sc = msk_ref[NL] * jnp.sum(qk[:, sl], axis=-1, keepdims=True)
        for l in range(NL):
            xl = xs[l][:, sl]
            sc = sc + msk_ref[l] * _dot_nt(xl, xl)
        vh = v[:, sl]
        st = st_ref[h]
        o_h = _dot_nt(qb[:, sl], st.astype(BF16)) + _dot(sc.astype(BF16), vh.astype(BF16))
        st_ref[h] = st * ex_tot[:, sl] + _dot(vh.T.astype(BF16), kr[:, sl])
        outs.append(o_h)

    if reverse:
        of = of_ref[0]
        hg = hg_ref[0].astype(F32)
        gain = gain_ref[...]
        for h in range(H):
            sl = slice(h * dk, (h + 1) * dk)
            tot = of[:, sl] + outs[h]
            ms = jnp.mean(tot * tot, axis=-1, keepdims=True)
            o_ref[0, :, sl] = (tot * lax.rsqrt(ms + RMS_EPS) * gain * _silu(hg[:, sl])).astype(o_ref.dtype)
    else:
        for h in range(H):
            o_ref[0, :, h * dk:(h + 1) * dk] = outs[h]


def _hgrn_pass(phg, lb_logits, norm_gain, of, layer, reverse):
    B, S, W5 = phg.shape
    W = W5 // 5
    C = min(C_HGRN, S)
    nc = S // C
    NL, amat, msk, upv = _hgrn_consts(C, reverse)
    L = lb_logits.shape[0]
    dk = W // HGRN_HEADS

    def cmap(i):
        return nc - 1 - i if reverse else i

    fcol = 2 if reverse else 1
    in_specs = [
        pl.BlockSpec((1, C, W), lambda b, i: (b, cmap(i), 0)),
        pl.BlockSpec((1, C, W), lambda b, i: (b, cmap(i), fcol)),
        pl.BlockSpec((1, C, W), lambda b, i: (b, cmap(i), 3)),
        pl.BlockSpec((L, W), lambda b, i: (0, 0)),
        pl.BlockSpec(amat.shape, lambda b, i: (0, 0)),
        pl.BlockSpec(msk.shape, lambda b, i: (0, 0, 0)),
        pl.BlockSpec(upv.shape, lambda b, i: (0, 0, 0)),
    ]
    args = [phg, phg, phg, lb_logits, jnp.asarray(amat, BF16), jnp.asarray(msk, F32), jnp.asarray(upv, F32)]
    if reverse:
        in_specs += [pl.BlockSpec((1, C, W), lambda b, i: (b, cmap(i), 0)),
                     pl.BlockSpec((1, C, W), lambda b, i: (b, cmap(i), 4)),
                     pl.BlockSpec((1, dk), lambda b, i: (0, 0))]
        args += [of, phg, norm_gain.reshape(1, dk)]
    return pl.pallas_call(
        functools.partial(_hgrn_kernel, layer=layer, reverse=reverse, C=C, NL=NL),
        grid=(B, nc),
        in_specs=in_specs,
        out_specs=pl.BlockSpec((1, C, W), lambda b, i: (b, cmap(i), 0)),
        out_shape=jax.ShapeDtypeStruct((B, S, W), BF16 if reverse else F32),
        scratch_shapes=[pltpu.VMEM((HGRN_HEADS, dk, dk), F32)],
        compiler_params=_cparams(("parallel", "arbitrary")),
        name="hgrn_bwd" if reverse else "hgrn_fwd",
    )(*args)


def _merge_kernel(ya_ref, yl_ref, yh_ref, g0_ref, g1_ref, g2_ref, x_ref, mod_ref,
                  wa_ref, wl_ref, wh_ref, wo_ref, lng_ref, lnb_ref, wr_ref,
                  x1_ref, h2_ref, route_ref, *, alpha, n_groups, e_per):
    m = mod_ref[0]
    merged = (_sigmoid(g0_ref[0].astype(F32)) * _dot(ya_ref[0], wa_ref[...])
              + _sigmoid(g1_ref[0].astype(F32)) * _dot(yl_ref[0], wl_ref[...])
              + _sigmoid(g2_ref[0].astype(F32)) * _dot(yh_ref[0], wh_ref[...]))
    mix = _dot(merged.astype(BF16), wo_ref[...])
    x1 = _layernorm(alpha * x_ref[0] + (1.0 + m[2:3]) * mix, lng_ref[...], lnb_ref[...])
    x1_ref[0] = x1
    h2 = x1 * (1.0 + m[4:5]) + m[3:4]
    h2_ref[0] = h2

    logits = _dot_hi(h2, wr_ref[...])
    lane = lax.broadcasted_iota(jnp.int32, logits.shape, 1)
    big = jnp.int32(V7X_LANES)
    neg = jnp.float32(-jnp.inf)
    gl = jnp.where(lane < n_groups, logits, neg)
    gmax = jnp.max(gl, axis=-1, keepdims=True)
    gsel = jnp.min(jnp.where(gl == gmax, lane, big), axis=-1, keepdims=True)
    gp = 1.0 / jnp.sum(jnp.exp(gl - gmax), axis=-1, keepdims=True)
    lo = n_groups + gsel * e_per
    el = jnp.where((lane >= lo) & (lane < lo + e_per), logits, neg)
    m1 = jnp.max(el, axis=-1, keepdims=True)
    i1 = jnp.min(jnp.where(el == m1, lane, big), axis=-1, keepdims=True)
    el2 = jnp.where(lane == i1, neg, el)
    m2 = jnp.max(el2, axis=-1, keepdims=True)
    i2 = jnp.min(jnp.where(el2 == m2, lane, big), axis=-1, keepdims=True)
    z = jnp.sum(jnp.exp(el - m1), axis=-1, keepdims=True)
    p1 = 1.0 / z
    p2 = jnp.exp(m2 - m1) / z
    w1 = gp * (p1 / (p1 + p2))
    w2 = gp * (p2 / (p1 + p2))
    e1 = (i1 - n_groups).astype(F32)
    e2 = (i2 - n_groups).astype(F32)
    route_ref[0] = jnp.where(lane == 0, w1, jnp.where(lane == 1, w2,
                             jnp.where(lane == 2, e1, jnp.where(lane == 3, e2, 0.0))))


def _merge(ya, yl, yh, pmg, x, mod_l, wa, wl, wh, wo, lng, lnb, wr, n_groups, e_per, alpha):
    B, S, D = x.shape
    tm = min(TM_MERGE, S)
    bw = ya.shape[-1]
    tok = lambda b, i: (b, i, 0)
    const = lambda b, i: (0, 0)
    return pl.pallas_call(
        functools.partial(_merge_kernel, alpha=alpha, n_groups=n_groups, e_per=e_per),
        grid=(B, S // tm),
        in_specs=[pl.BlockSpec((1, tm, bw), tok), pl.BlockSpec((1, tm, bw), tok), pl.BlockSpec((1, tm, bw), tok),
                  pl.BlockSpec((1, tm, D), lambda b, i: (b, i, 0)),
                  pl.BlockSpec((1, tm, D), lambda b, i: (b, i, 1)),
                  pl.BlockSpec((1, tm, D), lambda b, i: (b, i, 2)),
                  pl.BlockSpec((1, tm, D), tok),
                  pl.BlockSpec((1, 6, D), lambda b, i: (b, 0, 0)),
                  pl.BlockSpec((bw, D), const), pl.BlockSpec((bw, D), const), pl.BlockSpec((bw, D), const),
                  pl.BlockSpec((D, D), const),
                  pl.BlockSpec((1, D), const), pl.BlockSpec((1, D), const),
                  pl.BlockSpec((D, V7X_LANES), const)],
        out_specs=[pl.BlockSpec((1, tm, D), tok), pl.BlockSpec((1, tm, D), tok),
                   pl.BlockSpec((1, tm, V7X_LANES), tok)],
        out_shape=[jax.ShapeDtypeStruct((B, S, D), F32), jax.ShapeDtypeStruct((B, S, D), F32),
                   jax.ShapeDtypeStruct((B, S, V7X_LANES), F32)],
        compiler_params=_cparams(("parallel", "parallel")),
        name="merge",
    )(ya, yl, yh, pmg, pmg, pmg, x, mod_l, wa, wl, wh, wo, lng, lnb, wr)


def _row_copy(src_hbm, dst_vmem, sem, src_row, dst_row):
    return pltpu.make_async_copy(src_hbm.at[pl.ds(src_row, 1)], dst_vmem.at[pl.ds(dst_row, 1)], sem)


def _expert_kernel(blk_e_ref, nused_ref, tok_ref, bw_ref, h_hbm, w13_ref, w2_ref, ys_ref, xbuf, sem, *, R, F):
    i = pl.program_id(0)

    @pl.when(i < nused_ref[0])
    def _():
        def start(r, c):
            _row_copy(h_hbm, xbuf, sem, tok_ref[0, 0, r], r).start()
            return c
        lax.fori_loop(0, R, start, 0)

        def wait(r, c):
            _row_copy(h_hbm, xbuf, sem, 0, r).wait()
            return c
        lax.fori_loop(0, R, wait, 0)

        ab = _dot(xbuf[...].astype(BF16), w13_ref[0])
        hmid = (_silu(ab[:, :F]) * ab[:, F:]).astype(BF16)
        ys_ref[...] = _dot(hmid, w2_ref[0]) * bw_ref[...]

    @pl.when(i >= nused_ref[0])
    def _():
        ys_ref[...] = jnp.zeros_like(ys_ref)


def _experts(h2, blk_e, nused, buf_tok, buf_w, w13, w2):
    N, D = h2.shape
    P = buf_tok.shape[0]
    R = R_MOE
    nb = P // R
    F = w2.shape[1]
    grid_spec = pltpu.PrefetchScalarGridSpec(
        num_scalar_prefetch=2,
        grid=(nb,),
        in_specs=[pl.BlockSpec((1, 1, R), lambda i, be, nu: (i, 0, 0), memory_space=pltpu.SMEM),
                  pl.BlockSpec((R, 1), lambda i, be, nu: (i, 0)),
                  pl.BlockSpec(memory_space=pl.ANY),
                  pl.BlockSpec((1, D, 2 * F), lambda i, be, nu: (be[i], 0, 0)),
                  pl.BlockSpec((1, F, D), lambda i, be, nu: (be[i], 0, 0))],
        out_specs=pl.BlockSpec((R, D), lambda i, be, nu: (i, 0)),
        scratch_shapes=[pltpu.VMEM((R, D), F32), pltpu.SemaphoreType.DMA(())],
    )
    return pl.pallas_call(
        functools.partial(_expert_kernel, R=R, F=F),
        grid_spec=grid_spec,
        out_shape=jax.ShapeDtypeStruct((P, D), F32),
        compiler_params=_cparams(("arbitrary",)),
        name="moe_experts",
    )(blk_e, nused, buf_tok.reshape(nb, 1, R), buf_w.reshape(P, 1), h2, w13, w2)


def _combine_kernel(pos_ref, ys_hbm, x_ref, mod_ref, lng_ref, lnb_ref, o_ref, ybuf, sem, *, TC, alpha):
    def start(r, c):
        for a in range(TOP_K):
            _row_copy(ys_hbm, ybuf.at[a], sem, pos_ref[0, 0, TOP_K * r + a], r).start()
        return c
    lax.fori_loop(0, TC, start, 0)

    def wait(r, c):
        for a in range(TOP_K):
            _row_copy(ys_hbm, ybuf.at[a], sem, 0, r).wait()
        return c
    lax.fori_loop(0, TC, wait, 0)

    m = mod_ref[0]
    ffn = ybuf[0]
    for a in range(1, TOP_K):
        ffn = ffn + ybuf[a]
    o_ref[...] = _layernorm(alpha * x_ref[...] + (1.0 + m[5:6]) * ffn, lng_ref[...], lnb_ref[...])


def _combine(ys, pos, x1, mod_l, lng, lnb, S, alpha):
    N, D = x1.shape
    TC = min(TC_MOE, S)
    nt = N // TC
    per_b = S // TC
    return pl.pallas_call(
        functools.partial(_combine_kernel, TC=TC, alpha=alpha),
        grid=(nt,),
        in_specs=[pl.BlockSpec((1, 1, TOP_K * TC), lambda i: (i, 0, 0), memory_space=pltpu.SMEM),
                  pl.BlockSpec(memory_space=pl.ANY),
                  pl.BlockSpec((TC, D), lambda i: (i, 0)),
                  pl.BlockSpec((1, 6, D), lambda i: (i // per_b, 0, 0)),
                  pl.BlockSpec((1, D), lambda i: (0, 0)),
                  pl.BlockSpec((1, D), lambda i: (0, 0))],
        out_specs=pl.BlockSpec((TC, D), lambda i: (i, 0)),
        out_shape=jax.ShapeDtypeStruct((N, D), F32),
        scratch_shapes=[pltpu.VMEM((TOP_K, TC, D), F32), pltpu.SemaphoreType.DMA(())],
        compiler_params=_cparams(("arbitrary",)),
        name="moe_combine",
    )(pos.reshape(nt, 1, TOP_K * TC), ys, x1, mod_l, lng, lnb)


def _dispatch_plan(route, n_experts, R):
    N = route.shape[0]
    w = route[:, 0:TOP_K]
    eid = route[:, TOP_K:2 * TOP_K].astype(jnp.int32)
    A = N * TOP_K
    eid_f = eid.reshape(A)
    tok_f = jnp.repeat(jnp.arange(N, dtype=jnp.int32), TOP_K)
    order = jnp.argsort(eid_f, stable=True)
    rank = jnp.zeros((A,), jnp.int32).at[order].set(jnp.arange(A, dtype=jnp.int32))
    counts = jnp.zeros((n_experts,), jnp.int32).at[eid_f].add(1)
    starts = jnp.cumsum(counts) - counts
    pcounts = (counts + R - 1) // R * R
    pends = jnp.cumsum(pcounts)
    pstarts = pends - pcounts
    dest = pstarts[eid_f] + rank - starts[eid_f]
    P = A + n_experts * R
    nb = P // R
    buf_tok = jnp.zeros((P,), jnp.int32).at[dest].set(tok_f)
    buf_w = jnp.zeros((P,), F32).at[dest].set(w.reshape(A))
    blk_e = jnp.clip(jnp.searchsorted(pends, jnp.arange(nb, dtype=jnp.int32) * R, side='right'),
                     0, n_experts - 1).astype(jnp.int32)
    nused = (pends[-1:] // R).astype(jnp.int32)
    return buf_tok, buf_w, blk_e, nused, dest


def _prep_weights(p):
    L = p['w_in'].shape[0]
    W = p['lru_conv_w'].shape[-1]
    nblk, bs = p['lru_wa'].shape[2], p['lru_wa'].shape[3]

    def blockdiag(w):
        eye = jnp.eye(nblk, dtype=w.dtype)
        return jnp.einsum('ldnij,nm->ldnimj', w, eye).reshape(L, 2, W, W)

    G, E = p['router_expert_w'].shape[1], p['router_expert_w'].shape[3]
    D = p['w_in'].shape[1]
    wr = jnp.concatenate([p['router_group_w'],
                          p['router_expert_w'].transpose(0, 2, 1, 3).reshape(L, D, G * E)], -1)
    wr = jnp.pad(wr, ((0, 0), (0, 0), (0, V7X_LANES - wr.shape[-1])))
    return dict(
        w_in=p['w_in'].astype(BF16),
        gq=jnp.tile(p['attn_q_gain'], (1, ATTN_HEADS))[:, None, :],
        gk=jnp.tile(p['attn_k_gain'], (1, KV_HEADS))[:, None, :],
        w_attn_o=p['w_attn_o'].astype(BF16),
        conv_w=p['lru_conv_w'], conv_b=p['lru_conv_b'][:, None, :],
        wbd=jnp.concatenate([blockdiag(p['lru_wa']), blockdiag(p['lru_wi'])], -1).astype(BF16),
        gb=jnp.concatenate([p['lru_ba'], p['lru_bi']], -1)[:, :, None, :],
        lam=p['lru_lambda'][:, :, None, :],
        w_lru_o=p['w_lru_o'].astype(BF16),
        w_hgrn_o=p['w_hgrn_o'].astype(BF16),
        w_out=p['w_out'].astype(BF16),
        ln1_g=p['ln1_g'][:, None, :], ln1_b=p['ln1_b'][:, None, :],
        ln2_g=p['ln2_g'][:, None, :], ln2_b=p['ln2_b'][:, None, :],
        wr=wr, n_groups=G, e_per=E,
        w13=p['expert_w13'].astype(BF16), w2=p['expert_w2'].astype(BF16),
    )


def _trunk(x, c, p, wp):
    B, S, D = x.shape
    L = p['w_in'].shape[0]
    alpha = (2 * L) ** 0.25
    QW = ATTN_HEADS * HEAD_DIM
    KW = KV_HEADS * HEAD_DIM
    LW = p['lru_conv_w'].shape[-1]
    HW = p['hgrn_lb_logits'].shape[-1]
    widths = (QW, 2 * KW, 2 * LW, 5 * HW, N_BRANCH * D)
    assert sum(widths) == p['w_in'].shape[-1]
    n_experts = wp['n_groups'] * wp['e_per']

    mod = _ada_mod(c, p['ada_w'], p['ada_b'])
    cos_t, sin_t = _rope_tables(S)
    bind, psw = _head_consts()
    cos_t, sin_t = jnp.asarray(cos_t), jnp.asarray(sin_t)
    bind, psw = jnp.asarray(bind, BF16), jnp.asarray(psw, BF16)

    for l in range(L):
        pq, pkv, plru, phg, pmg = _inproj(x, mod[l], wp['w_in'][l], widths)
        q, kt, v = _attn_prep(pq, pkv, cos_t, sin_t, wp['gq'][l], wp['gk'][l], bind, psw)
        ya = _attention(q, kt, v)
        lru_args = (plru, wp['conv_w'][l], wp['conv_b'][l], wp['wbd'][l], wp['gb'][l], wp['lam'][l])
        hf = _lru_pass(*lru_args, None, reverse=False)
        yl = _lru_pass(*lru_args, hf, reverse=True)
        of = _hgrn_pass(phg, p['hgrn_lb_logits'], p['hgrn_norm_gain'][l], None, l, reverse=False)
        yh = _hgrn_pass(phg, p['hgrn_lb_logits'], p['hgrn_norm_gain'][l], of, l, reverse=True)
        x1, h2, route = _merge(ya, yl, yh, pmg, x, mod[l], wp['w_attn_o'][l], wp['w_lru_o'][l],
                               wp['w_hgrn_o'][l], wp['w_out'][l], wp['ln1_g'][l], wp['ln1_b'][l],
                               wp['wr'][l], wp['n_groups'], wp['e_per'], alpha)
        N = B * S
        buf_tok, buf_w, blk_e, nused, dest = _dispatch_plan(route.reshape(N, V7X_LANES), n_experts, R_MOE)
        ys = _experts(h2.reshape(N, D), blk_e, nused, buf_tok, buf_w, wp['w13'][l], wp['w2'][l])
        x = _combine(ys, dest, x1.reshape(N, D), mod[l], wp['ln2_g'][l], wp['ln2_b'][l], S, alpha).reshape(B, S, D)
    return x


def kernel(x_prompt, x_sample, c_prompt, c_sample, ada_w, ada_b, w_in, attn_q_gain, attn_k_gain, w_attn_o, lru_conv_w, lru_conv_b, lru_wa, lru_ba, lru_wi, lru_bi, lru_lambda, w_lru_o, hgrn_lb_logits, hgrn_norm_gain, w_hgrn_o, w_out, ln1_g, ln1_b, router_group_w, router_expert_w, expert_w13, expert_w2, ln2_g, ln2_b):
    p = dict(ada_w=ada_w, ada_b=ada_b, w_in=w_in, attn_q_gain=attn_q_gain, attn_k_gain=attn_k_gain,
             w_attn_o=w_attn_o, lru_conv_w=lru_conv_w, lru_conv_b=lru_conv_b, lru_wa=lru_wa, lru_ba=lru_ba,
             lru_wi=lru_wi, lru_bi=lru_bi, lru_lambda=lru_lambda, w_lru_o=w_lru_o,
             hgrn_lb_logits=hgrn_lb_logits, hgrn_norm_gain=hgrn_norm_gain, w_hgrn_o=w_hgrn_o, w_out=w_out,
             ln1_g=ln1_g, ln1_b=ln1_b, router_group_w=router_group_w, router_expert_w=router_expert_w,
             expert_w13=expert_w13, expert_w2=expert_w2, ln2_g=ln2_g, ln2_b=ln2_b)
    wp = _prep_weights(p)
    return (_trunk(x_prompt, c_prompt, p, wp), _trunk(x_sample, c_sample, p, wp))
```

```python
import functools
import math

import numpy as np
import jax
import jax.numpy as jnp
from jax import lax
from jax.experimental import pallas as pl
from jax.experimental.pallas import tpu as pltpu

F32 = jnp.float32
BF16 = jnp.bfloat16

GRID_W = 64
CONV_W = 4
CONV_LEFT = 2
LRU_C = 8.0
ATTN_HEADS = 8
KV_HEADS = 2
HEAD_DIM = 64
GQA_GROUP = ATTN_HEADS // KV_HEADS
AXIAL_DIM = HEAD_DIM // 2
AXIAL_FREQS = AXIAL_DIM // 2
ROPE_THETA = 10000.0
HGRN_HEADS = 4
N_BRANCH = 3
TOP_K = 2
LN_EPS = 1e-5
RMS_EPS = 1e-6
LOG2_E = 1.4426950408889634

V7X_LANES = 128
V7X_SUBLANES = 8
V7X_VMEM_LIMIT_BYTES = 56 * 1024 * 1024

TM_INPROJ = 512
TM_PREP = 512
TQ_ATTN = 256
T_LRU = 256
C_HGRN = 128
TM_MERGE = 512
R_MOE = 256
TM_DISPATCH = 512
TC_MOE = 256


def _cparams(sem):
    return pltpu.CompilerParams(dimension_semantics=sem, vmem_limit_bytes=V7X_VMEM_LIMIT_BYTES)


def _dot(a, b):
    return jnp.dot(a, b, preferred_element_type=F32)


def _dot_nt(a, b):
    return lax.dot_general(a, b, (((1,), (1,)), ((), ())), preferred_element_type=F32)


def _dot_hi(a, b):
    return jnp.dot(a, b, preferred_element_type=F32, precision=lax.Precision.HIGHEST)


def _split2(x):
    hi = x.astype(BF16)
    lo = (x - hi.astype(F32)).astype(BF16)
    return hi, lo


def _dot_x_sel(x, m):
    hi, lo = _split2(x)
    return _dot(hi, m) + _dot(lo, m)


def _dot_sel_x(m, x):
    hi, lo = _split2(x)
    return _dot(m, hi) + _dot(m, lo)


def _sigmoid(x):
    return 1.0 / (1.0 + jnp.exp(-x))


def _silu(x):
    return x * _sigmoid(x)


def _layernorm(y, g, b):
    mu = jnp.mean(y, axis=-1, keepdims=True)
    d = y - mu
    var = jnp.mean(d * d, axis=-1, keepdims=True)
    return d * lax.rsqrt(var + LN_EPS) * g + b


def _ada_kernel(c_ref, w_ref, b_ref, o_ref):
    c = c_ref[...]
    o_ref[0] = _dot_hi(_silu(c), w_ref[0]) + b_ref[0]


def _ada_mod(c, ada_w, ada_b):
    L, D, N6 = ada_w.shape
    B = c.shape[0]
    tn = N6 // 6
    out = pl.pallas_call(
        _ada_kernel,
        grid=(L, N6 // tn),
        in_specs=[pl.BlockSpec((B, D), lambda l, j: (0, 0)),
                  pl.BlockSpec((1, D, tn), lambda l, j: (l, 0, j)),
                  pl.BlockSpec((1, 1, tn), lambda l, j: (l, 0, j))],
        out_specs=pl.BlockSpec((1, B, tn), lambda l, j: (l, 0, j)),
        out_shape=jax.ShapeDtypeStruct((L, B, N6), F32),
        compiler_params=_cparams(("arbitrary", "arbitrary")),
        name="ada_mod",
    )(c, ada_w, ada_b.reshape(L, 1, N6))
    return out.reshape(L, B, 6, N6 // 6)


def _inproj_kernel(x_ref, mod_ref, w_ref, *o_refs, chunk):
    m = mod_ref[0]
    h = (x_ref[0] * (1.0 + m[1:2]) + m[0:1]).astype(BF16)
    off = 0
    for o_ref in o_refs:
        n = o_ref.shape[-1]
        for c0 in range(0, n, chunk):
            c1 = min(c0 + chunk, n)
            o_ref[0, :, c0:c1] = _dot(h, w_ref[:, off + c0:off + c1]).astype(o_ref.dtype)
        off += n


def _inproj(x, mod_l, w_in_bf, widths):
    B, S, D = x.shape
    N = w_in_bf.shape[1]
    tm = min(TM_INPROJ, S)
    out_shape = [jax.ShapeDtypeStruct((B, S, n), BF16) for n in widths]
    out_specs = [pl.BlockSpec((1, tm, n), lambda b, i: (b, i, 0)) for n in widths]
    return pl.pallas_call(
        functools.partial(_inproj_kernel, chunk=512),
        grid=(B, S // tm),
        in_specs=[pl.BlockSpec((1, tm, D), lambda b, i: (b, i, 0)),
                  pl.BlockSpec((1, 6, D), lambda b, i: (b, 0, 0)),
                  pl.BlockSpec((D, N), lambda b, i: (0, 0), pipeline_mode=pl.Buffered(1))],
        out_specs=out_specs,
        out_shape=out_shape,
        compiler_params=_cparams(("parallel", "parallel")),
        name="in_proj",
    )(x, mod_l, w_in_bf)


def _rope_tables(S):
    t = np.arange(S)
    row = (t // GRID_W).astype(np.float64)
    col = (t % GRID_W).astype(np.float64)
    inv = ROPE_THETA ** (-np.arange(AXIAL_FREQS, dtype=np.float64) / AXIAL_FREQS)
    ang_r = (row[:, None].astype(np.float32) * inv.astype(np.float32)[None, :]).astype(np.float32)
    ang_c = (col[:, None].astype(np.float32) * inv.astype(np.float32)[None, :]).astype(np.float32)
    cos_h = np.concatenate([np.cos(ang_r), np.cos(ang_r), np.cos(ang_c), np.cos(ang_c)], -1)
    sin_h = np.concatenate([np.sin(ang_r), np.sin(ang_r), np.sin(ang_c), np.sin(ang_c)], -1)
    cos_t = np.tile(cos_h, (1, ATTN_HEADS)).astype(np.float32)
    sin_t = np.tile(sin_h, (1, ATTN_HEADS)).astype(np.float32)
    return cos_t, sin_t


def _head_consts():
    W = ATTN_HEADS * HEAD_DIM
    d = np.arange(W)
    bind = (d[:, None] // HEAD_DIM == d[None, :] // HEAD_DIM).astype(np.float32) / HEAD_DIM
    psw = np.zeros((W, W), np.float32)
    lowhalf = (d % AXIAL_DIM) < AXIAL_FREQS
    for j in d:
        if lowhalf[j]:
            psw[j + AXIAL_FREQS, j] = -1.0
        else:
            psw[j - AXIAL_FREQS, j] = 1.0
    return bind, psw


def _prep_kernel(pq_ref, pkv_ref, cos_ref, sin_ref, gq_ref, gk_ref, bind_ref, psw_ref,
                 qt_ref, k_ref, vt_ref):
    KW = KV_HEADS * HEAD_DIM
    cos = cos_ref[...]
    sin = sin_ref[...]

    def norm_rope(u, gain, w):
        ms = _dot_x_sel(u * u, bind_ref[:w, :w])
        un = u * lax.rsqrt(ms + RMS_EPS) * gain
        return un * cos[:, :w] + _dot_x_sel(un, psw_ref[:w, :w]) * sin[:, :w]

    q = norm_rope(pq_ref[0].astype(F32), gq_ref[...], ATTN_HEADS * HEAD_DIM)
    qt = (q * (HEAD_DIM ** -0.5 * LOG2_E)).T
    zeros = jnp.zeros((HEAD_DIM, qt.shape[1]), F32)
    for h in range(ATTN_HEADS):
        rows = [zeros] * KV_HEADS
        rows[h // GQA_GROUP] = qt[h * HEAD_DIM:(h + 1) * HEAD_DIM]
        qt_ref[0, h] = jnp.concatenate(rows, axis=0).astype(BF16)
    kv = pkv_ref[0]
    k_ref[0] = norm_rope(kv[:, :KW].astype(F32), gk_ref[...], KW).astype(BF16)
    vt_ref[0] = kv[:, KW:].astype(F32).T.astype(BF16)


def _attn_prep(pq, pkv, cos_t, sin_t, gq, gk, bind, psw):
    B, S, QW = pq.shape
    KW = KV_HEADS * HEAD_DIM
    tm = min(TM_PREP, S)
    return pl.pallas_call(
        _prep_kernel,
        grid=(B, S // tm),
        in_specs=[pl.BlockSpec((1, tm, QW), lambda b, i: (b, i, 0)),
                  pl.BlockSpec((1, tm, 2 * KW), lambda b, i: (b, i, 0)),
                  pl.BlockSpec((tm, QW), lambda b, i: (i, 0)),
                  pl.BlockSpec((tm, QW), lambda b, i: (i, 0)),
                  pl.BlockSpec((1, QW), lambda b, i: (0, 0)),
                  pl.BlockSpec((1, KW), lambda b, i: (0, 0)),
                  pl.BlockSpec((QW, QW), lambda b, i: (0, 0)),
                  pl.BlockSpec((QW, QW), lambda b, i: (0, 0))],
        out_specs=[pl.BlockSpec((1, ATTN_HEADS, KW, tm), lambda b, i: (b, 0, 0, i)),
                   pl.BlockSpec((1, tm, KW), lambda b, i: (b, i, 0)),
                   pl.BlockSpec((1, KW, tm), lambda b, i: (b, 0, i))],
        out_shape=[jax.ShapeDtypeStruct((B, ATTN_HEADS, KW, S), BF16),
                   jax.ShapeDtypeStruct((B, S, KW), BF16),
                   jax.ShapeDtypeStruct((B, KW, S), BF16)],
        compiler_params=_cparams(("parallel", "parallel")),
        name="attn_prep",
    )(pq, pkv, cos_t, sin_t, gq, gk, bind, psw)


def _attn_kernel(qt_ref, k_ref, vt_ref, o_ref):
    k = k_ref[0]
    vt = vt_ref[0]
    outs = []
    for h in range(GQA_GROUP):
        st = _dot(k, qt_ref[0, h])
        m = jnp.max(st, axis=0, keepdims=True)
        p = jnp.exp2(st - m)
        l = jnp.sum(p, axis=0, keepdims=True)
        ot = _dot(vt, p.astype(BF16))
        outs.append((ot / l).T)
    o_ref[0] = jnp.concatenate(outs, axis=-1).astype(o_ref.dtype)


def _attention(qt, k, vt):
    B, S, KW = k.shape
    GW = GQA_GROUP * HEAD_DIM
    tq = min(TQ_ATTN, S)
    return pl.pallas_call(
        _attn_kernel,
        grid=(B, KV_HEADS, S // tq),
        in_specs=[pl.BlockSpec((1, GQA_GROUP, KW, tq), lambda b, g, i: (b, g, 0, i)),
                  pl.BlockSpec((1, S, KW), lambda b, g, i: (b, 0, 0)),
                  pl.BlockSpec((1, HEAD_DIM, S), lambda b, g, i: (b, g, 0))],
        out_specs=pl.BlockSpec((1, tq, GW), lambda b, g, i: (b, i, g)),
        out_shape=jax.ShapeDtypeStruct((B, S, ATTN_HEADS * HEAD_DIM), BF16),
        compiler_params=_cparams(("parallel", "parallel", "parallel")),
        name="attention",
    )(qt, k, vt)


def _log1p(y):
    u = 1.0 + y
    return jnp.where(u == 1.0, y, jnp.log(u) * (y / (u - 1.0)))


def _softplus(z):
    return jnp.maximum(z, 0.0) + _log1p(jnp.exp(-jnp.abs(z)))


def _neg_expm1_2x(z, ez):
    return -jnp.tanh(z) * (ez * ez + 1.0)


def _gelu_tanh(x):
    return 0.5 * x * (1.0 + jnp.tanh(math.sqrt(2.0 / math.pi) * (x + 0.044715 * (x * x * x))))


def _lru_kernel(*refs, reverse, T):
    if reverse:
        (x_ref, xp_ref, xn_ref, cw_ref, cb_ref, wbd_ref, gb_ref, lam_ref, hf_ref, lg_ref,
         o_ref, carry_ref) = refs
    else:
        (x_ref, xp_ref, xn_ref, cw_ref, cb_ref, wbd_ref, gb_ref, lam_ref,
         o_ref, carry_ref) = refs
    i = pl.program_id(1)
    nt = pl.num_programs(1)
    tile = nt - 1 - i if reverse else i
    W = x_ref.shape[-1]

    @pl.when(i == 0)
    def _():
        carry_ref[...] = jnp.zeros_like(carry_ref)

    x = x_ref[0].astype(F32)
    prev = jnp.where(tile == 0, 0.0, xp_ref[0].astype(F32))
    nxt = jnp.where(tile == nt - 1, 0.0, xn_ref[0].astype(F32))
    row8 = lax.broadcasted_iota(jnp.int32, (V7X_SUBLANES, W), 0)

    def shifted(k):
        if k == 0:
            return x
        r = pltpu.roll(x, (-k) % T, axis=0)
        if k < 0:
            fill = pltpu.roll(prev, (-k) % V7X_SUBLANES, axis=0)
            head = jnp.where(row8 < -k, fill, r[:V7X_SUBLANES])
            return jnp.concatenate([head, r[V7X_SUBLANES:]], axis=0)
        fill = pltpu.roll(nxt, (-k) % V7X_SUBLANES, axis=0)
        tail = jnp.where(row8 >= V7X_SUBLANES - k, fill, r[T - V7X_SUBLANES:])
        return jnp.concatenate([r[:T - V7X_SUBLANES], tail], axis=0)

    cw = cw_ref[...]
    xc = cb_ref[...] + sum(shifted(j - CONV_LEFT) * cw[j:j + 1] for j in range(CONV_W))

    gates = _dot(xc.astype(BF16), wbd_ref[0]) + gb_ref[0]
    r = _sigmoid(gates[:, :W])
    ig = _sigmoid(gates[:, W:])
    log_a = (-LRU_C) * r * _softplus(-lam_ref[0])
    a = jnp.exp(log_a)
    u = jnp.sqrt(_neg_expm1_2x(log_a, a)) * (ig * xc)

    row = lax.broadcasted_iota(jnp.int32, (T, W), 0)

    def neighbour(z, d, fill):
        if d % V7X_SUBLANES == 0:
            pad = jnp.full((d, W), fill, F32)
            return jnp.concatenate([z[d:], pad] if reverse else [pad, z[:T - d]], axis=0)
        if reverse:
            return jnp.where(row < T - d, pltpu.roll(z, T - d, axis=0), fill)
        return jnp.where(row >= d, pltpu.roll(z, d, axis=0), fill)

    A, U = a, u
    d = 1
    while d < T:
        U = A * neighbour(U, d, 0.0) + U
        A = A * neighbour(A, d, 1.0)
        d *= 2
    h = A * carry_ref[...] + U
    if reverse:
        carry_ref[...] = h[0:1]
        o_ref[0] = ((hf_ref[0] + h) * _gelu_tanh(lg_ref[0].astype(F32))).astype(o_ref.dtype)
    else:
        carry_ref[...] = h[T - 1:T]
        o_ref[0] = h


def _lru_pass(plru, conv_w, conv_b, wbd, gb, lam, hf, reverse):
    B, S, W2 = plru.shape
    W = W2 // 2
    T = min(T_LRU, S)
    nt = S // T
    r8 = T // V7X_SUBLANES
    nb8 = S // V7X_SUBLANES
    d = 1 if reverse else 0

    def tmap(i):
        return nt - 1 - i if reverse else i

    in_specs = [
        pl.BlockSpec((1, T, W), lambda b, i: (b, tmap(i), 0)),
        pl.BlockSpec((1, V7X_SUBLANES, W), lambda b, i: (b, jnp.maximum(tmap(i) * r8 - 1, 0), 0)),
        pl.BlockSpec((1, V7X_SUBLANES, W), lambda b, i: (b, jnp.minimum((tmap(i) + 1) * r8, nb8 - 1), 0)),
        pl.BlockSpec((CONV_W, W), lambda b, i: (0, 0)),
        pl.BlockSpec((1, W), lambda b, i: (0, 0)),
        pl.BlockSpec((1, W, 2 * W), lambda b, i: (d, 0, 0)),
        pl.BlockSpec((1, 1, 2 * W), lambda b, i: (d, 0, 0)),
        pl.BlockSpec((1, 1, W), lambda b, i: (d, 0, 0)),
    ]
    args = [plru, plru, plru, conv_w, conv_b, wbd, gb, lam]
    if reverse:
        in_specs += [pl.BlockSpec((1, T, W), lambda b, i: (b, tmap(i), 0)),
                     pl.BlockSpec((1, T, W), lambda b, i: (b, tmap(i), 1))]
        args += [hf, plru]
    return pl.pallas_call(
        functools.partial(_lru_kernel, reverse=reverse, T=T),
        grid=(B, nt),
        in_specs=in_specs,
        out_specs=pl.BlockSpec((1, T, W), lambda b, i: (b, tmap(i), 0)),
        out_shape=jax.ShapeDtypeStruct((B, S, W), BF16 if reverse else F32),
        scratch_shapes=[pltpu.VMEM((1, W), F32)],
        compiler_params=_cparams(("parallel", "arbitrary")),
        name="lru_bwd" if reverse else "lru_fwd",
    )(*args)


def _hgrn_consts(C, reverse):
    nl = int(round(math.log2(C)))
    t = np.arange(C)
    a_rows, masks, upper = [], [], []
    for lvl in range(nl):
        m = C >> (lvl + 1)
        blk = t // (2 * m)
        up = (t % (2 * m)) >= m
        mid = blk * 2 * m + m
        r = t[None, :]
        a_up = (r >= mid[:, None]) & (r <= t[:, None])
        a_lo = (r > t[:, None]) & (r <= mid[:, None] - 1)
        a_rows.append(np.where(up[:, None], a_up, a_lo))
        masks.append((blk[:, None] == blk[None, :]) & up[:, None] & (~up)[None, :])
        upper.append(up)
    a_rows.append(t[None, :] <= t[:, None])
    a_rows.append(t[None, :] > t[:, None])
    a_rows.append(np.ones((V7X_SUBLANES, C), bool))
    masks.append(np.eye(C, dtype=bool))
    if reverse:
        a_rows = [a[::-1, ::-1] if a.shape[0] == C else a for a in a_rows]
        masks = [mm[::-1, ::-1] for mm in masks]
        upper = [u[::-1] for u in upper]
    amat = np.concatenate(a_rows, 0).astype(np.float32)
    msk = np.stack(masks, 0).astype(np.float32)
    upv = np.stack(upper, 0).astype(np.float32)[:, :, None]
    return nl, amat, msk, upv


def _hgrn_kernel(*refs, layer, reverse, C, NL):
    if reverse:
        (hq_ref, hf_ref, hi_ref, lbl_ref, amat_ref, msk_ref, up_ref, of_ref, hg_ref, gain_ref,
         o_ref, st_ref) = refs
    else:
        (hq_ref, hf_ref, hi_ref, lbl_ref, amat_ref, msk_ref, up_ref, o_ref, st_ref) = refs
    H = HGRN_HEADS
    W = hq_ref.shape[-1]
    dk = W // H

    @pl.when(pl.program_id(1) == 0)
    def _():
        st_ref[...] = jnp.zeros_like(st_ref)

    lgt = lbl_ref[...]
    e = jnp.exp(lgt - jnp.max(lgt, axis=0, keepdims=True))
    p = e / jnp.sum(e, axis=0, keepdims=True)
    lb = jnp.zeros((1, W), F32)
    for j in range(1, layer + 1):
        lb = lb + p[j:j + 1]

    q = _silu(hq_ref[0].astype(F32))
    f = lb + (1.0 - lb) * _sigmoid(hf_ref[0].astype(F32))
    k = 1.0 - f
    g = jnp.log(f)
    v = hi_ref[0].astype(F32)

    ex = jnp.exp(_dot_sel_x(amat_ref[...], g))
    ex_b = ex[NL * C:(NL + 1) * C]
    ex_rem = ex[(NL + 1) * C:(NL + 2) * C]
    ex_tot = ex[(NL + 2) * C:(NL + 2) * C + 1]
    xs = [(jnp.where(up_ref[l] > 0.5, q, k) * ex[l * C:(l + 1) * C]).astype(BF16) for l in range(NL)]
    qk = q * k
    qb = (q * ex_b).astype(BF16)
    kr = (k * ex_rem).astype(BF16)

    outs = []
    for h in range(H):
        sl = slice(h * dk, (h + 1) * dk)
        sc = msk_ref[NL] * jnp.sum(qk[:, sl], axis=-1, keepdims=True)
        for l in range(NL):
            xl = xs[l][:, sl]
            sc = sc + msk_ref[l] * _dot_nt(xl, xl)
        vh = v[:, sl]
        st = st_ref[h]
        o_h = _dot_nt(qb[:, sl], st.astype(BF16)) + _dot(sc.astype(BF16), vh.astype(BF16))
        st_ref[h] = st * ex_tot[:, sl] + _dot(vh.T.astype(BF16), kr[:, sl])
        outs.append(o_h)

    if reverse:
        of = of_ref[0]
        hg = hg_ref[0].astype(F32)
        gain = gain_ref[...]
        for h in range(H):
            sl = slice(h * dk, (h + 1) * dk)
            tot = of[:, sl] + outs[h]
            ms = jnp.mean(tot * tot, axis=-1, keepdims=True)
            o_ref[0, :, sl] = (tot * lax.rsqrt(ms + RMS_EPS) * gain * _silu(hg[:, sl])).astype(o_ref.dtype)
    else:
        for h in range(H):
            o_ref[0, :, h * dk:(h + 1) * dk] = outs[h]


def _hgrn_pass(phg, lb_logits, norm_gain, of, layer, reverse):
    B, S, W5 = phg.shape
    W = W5 // 5
    C = min(C_HGRN, S)
    nc = S // C
    NL, amat, msk, upv = _hgrn_consts(C, reverse)
    L = lb_logits.shape[0]
    dk = W // HGRN_HEADS

    def cmap(i):
        return nc - 1 - i if reverse else i

    fcol = 2 if reverse else 1
    in_specs = [
        pl.BlockSpec((1, C, W), lambda b, i: (b, cmap(i), 0)),
        pl.BlockSpec((1, C, W), lambda b, i: (b, cmap(i), fcol)),
        pl.BlockSpec((1, C, W), lambda b, i: (b, cmap(i), 3)),
        pl.BlockSpec((L, W), lambda b, i: (0, 0)),
        pl.BlockSpec(amat.shape, lambda b, i: (0, 0)),
        pl.BlockSpec(msk.shape, lambda b, i: (0, 0, 0)),
        pl.BlockSpec(upv.shape, lambda b, i: (0, 0, 0)),
    ]
    args = [phg, phg, phg, lb_logits, jnp.asarray(amat, BF16), jnp.asarray(msk, F32), jnp.asarray(upv, F32)]
    if reverse:
        in_specs += [pl.BlockSpec((1, C, W), lambda b, i: (b, cmap(i), 0)),
                     pl.BlockSpec((1, C, W), lambda b, i: (b, cmap(i), 4)),
                     pl.BlockSpec((1, dk), lambda b, i: (0, 0))]
        args += [of, phg, norm_gain.reshape(1, dk)]
    return pl.pallas_call(
        functools.partial(_hgrn_kernel, layer=layer, reverse=reverse, C=C, NL=NL),
        grid=(B, nc),
        in_specs=in_specs,
        out_specs=pl.BlockSpec((1, C, W), lambda b, i: (b, cmap(i), 0)),
        out_shape=jax.ShapeDtypeStruct((B, S, W), BF16 if reverse else F32),
        scratch_shapes=[pltpu.VMEM((HGRN_HEADS, dk, dk), F32)],
        compiler_params=_cparams(("parallel", "arbitrary")),
        name="hgrn_bwd" if reverse else "hgrn_fwd",
    )(*args)


def _merge_kernel(ya_ref, yl_ref, yh_ref, g0_ref, g1_ref, g2_ref, x_ref, mod_ref,
                  wa_ref, wl_ref, wh_ref, wo_ref, lng_ref, lnb_ref, wr_ref,
                  x1_ref, h2_ref, route_ref, *, alpha, n_groups, e_per):
    m = mod_ref[0]
    merged = (_sigmoid(g0_ref[0].astype(F32)) * _dot(ya_ref[0], wa_ref[...])
              + _sigmoid(g1_ref[0].astype(F32)) * _dot(yl_ref[0], wl_ref[...])
              + _sigmoid(g2_ref[0].astype(F32)) * _dot(yh_ref[0], wh_ref[...]))
    mix = _dot(merged.astype(BF16), wo_ref[...])
    x1 = _layernorm(alpha * x_ref[0] + (1.0 + m[2:3]) * mix, lng_ref[...], lnb_ref[...])
    x1_ref[0] = x1
    h2 = x1 * (1.0 + m[4:5]) + m[3:4]
    h2_ref[0] = h2

    xh, xl = _split2(h2)
    r = _dot(xh, wr_ref[...])
    logits = r[:, :V7X_LANES] + r[:, V7X_LANES:] + _dot(xl, wr_ref[:, :V7X_LANES])
    lane = lax.broadcasted_iota(jnp.int32, logits.shape, 1)
    big = jnp.int32(V7X_LANES)
    neg = jnp.float32(-jnp.inf)
    gl = jnp.where(lane < n_groups, logits, neg)
    gmax = jnp.max(gl, axis=-1, keepdims=True)
    gsel = jnp.min(jnp.where(gl == gmax, lane, big), axis=-1, keepdims=True)
    gp = 1.0 / jnp.sum(jnp.exp(gl - gmax), axis=-1, keepdims=True)
    lo = n_groups + gsel * e_per
    el = jnp.where((lane >= lo) & (lane < lo + e_per), logits, neg)
    m1 = jnp.max(el, axis=-1, keepdims=True)
    i1 = jnp.min(jnp.where(el == m1, lane, big), axis=-1, keepdims=True)
    el2 = jnp.where(lane == i1, neg, el)
    m2 = jnp.max(el2, axis=-1, keepdims=True)
    i2 = jnp.min(jnp.where(el2 == m2, lane, big), axis=-1, keepdims=True)
    z = jnp.sum(jnp.exp(el - m1), axis=-1, keepdims=True)
    p1 = 1.0 / z
    p2 = jnp.exp(m2 - m1) / z
    w1 = gp * (p1 / (p1 + p2))
    w2 = gp * (p2 / (p1 + p2))
    e1 = (i1 - n_groups).astype(F32)
    e2 = (i2 - n_groups).astype(F32)
    route_ref[0] = jnp.where(lane == 0, w1, jnp.where(lane == 1, w2,
                             jnp.where(lane == 2, e1, jnp.where(lane == 3, e2, 0.0))))


def _merge(ya, yl, yh, pmg, x, mod_l, wa, wl, wh, wo, lng, lnb, wr, n_groups, e_per, alpha):
    B, S, D = x.shape
    tm = min(TM_MERGE, S)
    bw = ya.shape[-1]
    tok = lambda b, i: (b, i, 0)
    const = lambda b, i: (0, 0)
    return pl.pallas_call(
        functools.partial(_merge_kernel, alpha=alpha, n_groups=n_groups, e_per=e_per),
        grid=(B, S // tm),
        in_specs=[pl.BlockSpec((1, tm, bw), tok), pl.BlockSpec((1, tm, bw), tok), pl.BlockSpec((1, tm, bw), tok),
                  pl.BlockSpec((1, tm, D), lambda b, i: (b, i, 0)),
                  pl.BlockSpec((1, tm, D), lambda b, i: (b, i, 1)),
                  pl.BlockSpec((1, tm, D), lambda b, i: (b, i, 2)),
                  pl.BlockSpec((1, tm, D), tok),
                  pl.BlockSpec((1, 6, D), lambda b, i: (b, 0, 0)),
                  pl.BlockSpec((bw, D), const), pl.BlockSpec((bw, D), const), pl.BlockSpec((bw, D), const),
                  pl.BlockSpec((D, D), const),
                  pl.BlockSpec((1, D), const), pl.BlockSpec((1, D), const),
                  pl.BlockSpec((D, 2 * V7X_LANES), const)],
        out_specs=[pl.BlockSpec((1, tm, D), tok), pl.BlockSpec((1, tm, D), tok),
                   pl.BlockSpec((1, tm, V7X_LANES), tok)],
        out_shape=[jax.ShapeDtypeStruct((B, S, D), F32), jax.ShapeDtypeStruct((B, S, D), F32),
                   jax.ShapeDtypeStruct((B, S, V7X_LANES), F32)],
        compiler_params=_cparams(("parallel", "parallel")),
        name="merge",
    )(ya, yl, yh, pmg, pmg, pmg, x, mod_l, wa, wl, wh, wo, lng, lnb, wr)


def _dispatch_kernel(dest_ref, h_ref, xs_in, xs_hbm, sem, *, TM):
    del xs_in

    def start(r, c):
        for a in range(TOP_K):
            pltpu.make_async_copy(h_ref.at[pl.ds(r, 1)],
                                  xs_hbm.at[pl.ds(dest_ref[0, 0, TOP_K * r + a], 1)], sem).start()
        return c
    lax.fori_loop(0, TM, start, 0, unroll=4)
    for a in range(TOP_K):
        pltpu.make_async_copy(h_ref, xs_hbm.at[pl.ds(0, TM)], sem).wait()


def _dispatch(h2, dest, xs_prev, S):
    N, D = h2.shape
    TM = min(TM_DISPATCH, S)
    nt = N // TM
    return pl.pallas_call(
        functools.partial(_dispatch_kernel, TM=TM),
        grid=(nt,),
        in_specs=[pl.BlockSpec((1, 1, TOP_K * TM), lambda i: (i, 0, 0), memory_space=pltpu.SMEM),
                  pl.BlockSpec((TM, D), lambda i: (i, 0)),
                  pl.BlockSpec(memory_space=pl.ANY)],
        out_specs=pl.BlockSpec(memory_space=pl.ANY),
        out_shape=jax.ShapeDtypeStruct(xs_prev.shape, xs_prev.dtype),
        scratch_shapes=[pltpu.SemaphoreType.DMA(())],
        input_output_aliases={2: 0},
        compiler_params=_cparams(("arbitrary",)),
        name="moe_dispatch",
    )(dest.reshape(nt, 1, TOP_K * TM), h2, xs_prev)


def _expert_kernel(blk_e_ref, nused_ref, xs_ref, w13_ref, w2_ref, ys_ref, *, F):
    i = pl.program_id(0)

    @pl.when(i < nused_ref[0])
    def _():
        ab = _dot(xs_ref[...].astype(BF16), w13_ref[0])
        hmid = (_silu(ab[:, :F]) * ab[:, F:]).astype(BF16)
        ys_ref[...] = _dot(hmid, w2_ref[0])

    @pl.when(i >= nused_ref[0])
    def _():
        ys_ref[...] = jnp.zeros_like(ys_ref)


def _experts(xs, blk_e, nused, w13, w2):
    P, D = xs.shape
    R = R_MOE
    nb = P // R
    F = w2.shape[1]
    grid_spec = pltpu.PrefetchScalarGridSpec(
        num_scalar_prefetch=2,
        grid=(nb,),
        in_specs=[pl.BlockSpec((R, D), lambda i, be, nu: (i, 0)),
                  pl.BlockSpec((1, D, 2 * F), lambda i, be, nu: (be[i], 0, 0)),
                  pl.BlockSpec((1, F, D), lambda i, be, nu: (be[i], 0, 0))],
        out_specs=pl.BlockSpec((R, D), lambda i, be, nu: (i, 0)),
    )
    return pl.pallas_call(
        functools.partial(_expert_kernel, F=F),
        grid_spec=grid_spec,
        out_shape=jax.ShapeDtypeStruct((P, D), F32),
        compiler_params=_cparams(("arbitrary",)),
        name="moe_experts",
    )(blk_e, nused, xs, w13, w2)


def _combine_kernel(pos_ref, posn_ref, ys_hbm, route_ref, x_ref, mod_ref, lng_ref, lnb_ref, o_ref,
                    ybuf, sems, *, TC, alpha):
    i = pl.program_id(0)
    n = pl.num_programs(0)
    slot = i % 2

    def issue(p_ref, s):
        def body(r, c):
            for a in range(TOP_K):
                pltpu.make_async_copy(ys_hbm.at[pl.ds(p_ref[0, 0, TOP_K * r + a], 1)],
                                      ybuf.at[s, a, pl.ds(r, 1)], sems.at[s]).start()
            return c
        lax.fori_loop(0, TC, body, 0, unroll=4)

    @pl.when(i == 0)
    def _():
        issue(pos_ref, 0)

    @pl.when(i + 1 < n)
    def _():
        issue(posn_ref, 1 - slot)

    for a in range(TOP_K):
        pltpu.make_async_copy(ys_hbm.at[pl.ds(0, TC)], ybuf.at[slot, a], sems.at[slot]).wait()

    m = mod_ref[0]
    w = route_ref[...]
    ffn = w[:, 0:1] * ybuf[slot, 0]
    for a in range(1, TOP_K):
        ffn = ffn + w[:, a:a + 1] * ybuf[slot, a]
    o_ref[...] = _layernorm(alpha * x_ref[...] + (1.0 + m[5:6]) * ffn, lng_ref[...], lnb_ref[...])


def _combine(ys, dest, route, x1, mod_l, lng, lnb, S, alpha):
    N, D = x1.shape
    TC = min(TC_MOE, S)
    nt = N // TC
    per_b = S // TC
    pos = dest.reshape(nt, 1, TOP_K * TC)
    return pl.pallas_call(
        functools.partial(_combine_kernel, TC=TC, alpha=alpha),
        grid=(nt,),
        in_specs=[pl.BlockSpec((1, 1, TOP_K * TC), lambda i: (i, 0, 0), memory_space=pltpu.SMEM),
                  pl.BlockSpec((1, 1, TOP_K * TC), lambda i: (jnp.minimum(i + 1, nt - 1), 0, 0),
                               memory_space=pltpu.SMEM),
                  pl.BlockSpec(memory_space=pl.ANY),
                  pl.BlockSpec((TC, V7X_LANES), lambda i: (i, 0)),
                  pl.BlockSpec((TC, D), lambda i: (i, 0)),
                  pl.BlockSpec((1, 6, D), lambda i: (i // per_b, 0, 0)),
                  pl.BlockSpec((1, D), lambda i: (0, 0)),
                  pl.BlockSpec((1, D), lambda i: (0, 0))],
        out_specs=pl.BlockSpec((TC, D), lambda i: (i, 0)),
        out_shape=jax.ShapeDtypeStruct((N, D), F32),
        scratch_shapes=[pltpu.VMEM((2, TOP_K, TC, D), F32), pltpu.SemaphoreType.DMA((2,))],
        compiler_params=_cparams(("arbitrary",)),
        name="moe_combine",
    )(pos, pos, ys, route, x1, mod_l, lng, lnb)


def _dispatch_plan(route, n_experts, R):
    eid = route[:, TOP_K:2 * TOP_K].astype(jnp.int32)
    hot = (eid[:, :, None] == jnp.arange(n_experts, dtype=jnp.int32)).astype(jnp.int32).sum(1)
    csum = jnp.cumsum(hot, axis=0)
    counts = csum[-1]
    rank = jnp.take_along_axis(csum - hot, eid, axis=1)
    pcounts = (counts + R - 1) // R * R
    pends = jnp.cumsum(pcounts)
    pstarts = pends - pcounts
    dest = (pstarts[eid] + rank).astype(jnp.int32)
    nb = (eid.shape[0] * TOP_K + n_experts * R) // R
    starts = jnp.arange(nb, dtype=jnp.int32) * R
    blk_e = jnp.minimum((pends[None, :] <= starts[:, None]).astype(jnp.int32).sum(1), n_experts - 1)
    nused = (pends[-1:] // R).astype(jnp.int32)
    return blk_e, nused, dest


def _prep_weights(p):
    L = p['w_in'].shape[0]
    W = p['lru_conv_w'].shape[-1]
    nblk, bs = p['lru_wa'].shape[2], p['lru_wa'].shape[3]

    def blockdiag(w):
        eye = jnp.eye(nblk, dtype=w.dtype)
        return jnp.einsum('ldnij,nm->ldnimj', w, eye).reshape(L, 2, W, W)

    G, E = p['router_expert_w'].shape[1], p['router_expert_w'].shape[3]
    D = p['w_in'].shape[1]
    wr = jnp.concatenate([p['router_group_w'],
                          p['router_expert_w'].transpose(0, 2, 1, 3).reshape(L, D, G * E)], -1)
    wr = jnp.pad(wr, ((0, 0), (0, 0), (0, V7X_LANES - wr.shape[-1])))
    wr_hi = wr.astype(BF16)
    wr = jnp.concatenate([wr_hi, (wr - wr_hi.astype(F32)).astype(BF16)], -1)
    return dict(
        w_in=p['w_in'].astype(BF16),
        gq=jnp.tile(p['attn_q_gain'], (1, ATTN_HEADS))[:, None, :],
        gk=jnp.tile(p['attn_k_gain'], (1, KV_HEADS))[:, None, :],
        w_attn_o=p['w_attn_o'].astype(BF16),
        conv_w=p['lru_conv_w'], conv_b=p['lru_conv_b'][:, None, :],
        wbd=jnp.concatenate([blockdiag(p['lru_wa']), blockdiag(p['lru_wi'])], -1).astype(BF16),
        gb=jnp.concatenate([p['lru_ba'], p['lru_bi']], -1)[:, :, None, :],
        lam=p['lru_lambda'][:, :, None, :],
        w_lru_o=p['w_lru_o'].astype(BF16),
        w_hgrn_o=p['w_hgrn_o'].astype(BF16),
        w_out=p['w_out'].astype(BF16),
        ln1_g=p['ln1_g'][:, None, :], ln1_b=p['ln1_b'][:, None, :],
        ln2_g=p['ln2_g'][:, None, :], ln2_b=p['ln2_b'][:, None, :],
        wr=wr, n_groups=G, e_per=E,
        w13=p['expert_w13'].astype(BF16), w2=p['expert_w2'].astype(BF16),
    )


def _trunk(x, c, p, wp):
    B, S, D = x.shape
    L = p['w_in'].shape[0]
    alpha = (2 * L) ** 0.25
    QW = ATTN_HEADS * HEAD_DIM
    KW = KV_HEADS * HEAD_DIM
    LW = p['lru_conv_w'].shape[-1]
    HW = p['hgrn_lb_logits'].shape[-1]
    widths = (QW, 2 * KW, 2 * LW, 5 * HW, N_BRANCH * D)
    assert sum(widths) == p['w_in'].shape[-1]
    n_experts = wp['n_groups'] * wp['e_per']

    mod = _ada_mod(c, p['ada_w'], p['ada_b'])
    cos_t, sin_t = _rope_tables(S)
    bind, psw = _head_consts()
    cos_t, sin_t = jnp.asarray(cos_t), jnp.asarray(sin_t)
    bind, psw = jnp.asarray(bind, BF16), jnp.asarray(psw, BF16)

    xs = None
    for l in range(L):
        pq, pkv, plru, phg, pmg = _inproj(x, mod[l], wp['w_in'][l], widths)
        qt, k, vt = _attn_prep(pq, pkv, cos_t, sin_t, wp['gq'][l], wp['gk'][l], bind, psw)
        ya = _attention(qt, k, vt)
        lru_args = (plru, wp['conv_w'][l], wp['conv_b'][l], wp['wbd'][l], wp['gb'][l], wp['lam'][l])
        hf = _lru_pass(*lru_args, None, reverse=False)
        yl = _lru_pass(*lru_args, hf, reverse=True)
        of = _hgrn_pass(phg, p['hgrn_lb_logits'], p['hgrn_norm_gain'][l], None, l, reverse=False)
        yh = _hgrn_pass(phg, p['hgrn_lb_logits'], p['hgrn_norm_gain'][l], of, l, reverse=True)
        x1, h2, route = _merge(ya, yl, yh, pmg, x, mod[l], wp['w_attn_o'][l], wp['w_lru_o'][l],
                               wp['w_hgrn_o'][l], wp['w_out'][l], wp['ln1_g'][l], wp['ln1_b'][l],
                               wp['wr'][l], wp['n_groups'], wp['e_per'], alpha)
        N = B * S
        route = route.reshape(N, V7X_LANES)
        blk_e, nused, dest = _dispatch_plan(route, n_experts, R_MOE)
        if xs is None:
            xs = jnp.zeros((N * TOP_K + n_experts * R_MOE, D), F32)
        xs = _dispatch(h2.reshape(N, D), dest, xs, S)
        ys = _experts(xs, blk_e, nused, wp['w13'][l], wp['w2'][l])
        x = _combine(ys, dest, route, x1.reshape(N, D), mod[l], wp['ln2_g'][l], wp['ln2_b'][l], S,
                     alpha).reshape(B, S, D)
    return x


def kernel(x_prompt, x_sample, c_prompt, c_sample, ada_w, ada_b, w_in, attn_q_gain, attn_k_gain, w_attn_o, lru_conv_w, lru_conv_b, lru_wa, lru_ba, lru_wi, lru_bi, lru_lambda, w_lru_o, hgrn_lb_logits, hgrn_norm_gain, w_hgrn_o, w_out, ln1_g, ln1_b, router_group_w, router_expert_w, expert_w13, expert_w2, ln2_g, ln2_b):
    p = dict(ada_w=ada_w, ada_b=ada_b, w_in=w_in, attn_q_gain=attn_q_gain, attn_k_gain=attn_k_gain,
             w_attn_o=w_attn_o, lru_conv_w=lru_conv_w, lru_conv_b=lru_conv_b, lru_wa=lru_wa, lru_ba=lru_ba,
             lru_wi=lru_wi, lru_bi=lru_bi, lru_lambda=lru_lambda, w_lru_o=w_lru_o,
             hgrn_lb_logits=hgrn_lb_logits, hgrn_norm_gain=hgrn_norm_gain, w_hgrn_o=w_hgrn_o, w_out=w_out,
             ln1_g=ln1_g, ln1_b=ln1_b, router_group_w=router_group_w, router_expert_w=router_expert_w,
             expert_w13=expert_w13, expert_w2=expert_w2, ln2_g=ln2_g, ln2_b=ln2_b)
    wp = _prep_weights(p)
    return (_trunk(x_prompt, c_prompt, p, wp), _trunk(x_sample, c_sample, p, wp))
```

```python
import functools
import math

import numpy as np
import jax
import jax.numpy as jnp
from jax import lax
from jax.experimental import pallas as pl
from jax.experimental.pallas import tpu as pltpu

F32 = jnp.float32
BF16 = jnp.bfloat16

GRID_W = 64
CONV_W = 4
CONV_LEFT = 2
LRU_C = 8.0
ATTN_HEADS = 8
KV_HEADS = 2
HEAD_DIM = 64
GQA_GROUP = ATTN_HEADS // KV_HEADS
AXIAL_DIM = HEAD_DIM // 2
AXIAL_FREQS = AXIAL_DIM // 2
ROPE_THETA = 10000.0
HGRN_HEADS = 4
N_BRANCH = 3
TOP_K = 2
LN_EPS = 1e-5
RMS_EPS = 1e-6
LOG2_E = 1.4426950408889634

V7X_LANES = 128
V7X_SUBLANES = 8
V7X_VMEM_LIMIT_BYTES = 56 * 1024 * 1024

TM_INPROJ = 512
V_AUG_ROWS = HEAD_DIM + 16
TQ_ATTN = 256
T_LRU = 256
C_HGRN = 128
CB_HGRN = 2
TK_ATTN = 512
TM_MERGE = 512
R_MOE = 256
TM_DISPATCH = 512
TC_MOE = 256


def _cparams(sem):
    return pltpu.CompilerParams(dimension_semantics=sem, vmem_limit_bytes=V7X_VMEM_LIMIT_BYTES)


def _dot(a, b):
    return jnp.dot(a, b, preferred_element_type=F32)


def _dot_nt(a, b):
    return lax.dot_general(a, b, (((1,), (1,)), ((), ())), preferred_element_type=F32)


def _dot_hi(a, b):
    return jnp.dot(a, b, preferred_element_type=F32, precision=lax.Precision.HIGHEST)


def _split2(x):
    hi = x.astype(BF16)
    lo = (x - hi.astype(F32)).astype(BF16)
    return hi, lo


def _dot_x_sel(x, m):
    hi, lo = _split2(x)
    return _dot(hi, m) + _dot(lo, m)


def _dot_sel_x(m, x):
    hi, lo = _split2(x)
    return _dot(m, hi) + _dot(m, lo)


def _sigmoid(x):
    return 1.0 / (1.0 + jnp.exp(-x))


def _silu(x):
    return x * _sigmoid(x)


def _layernorm(y, g, b):
    mu = jnp.mean(y, axis=-1, keepdims=True)
    d = y - mu
    var = jnp.mean(d * d, axis=-1, keepdims=True)
    return d * lax.rsqrt(var + LN_EPS) * g + b


def _ada_kernel(c_ref, w_ref, b_ref, o_ref):
    c = c_ref[...]
    o_ref[0] = _dot_hi(_silu(c), w_ref[0]) + b_ref[0]


def _ada_mod(c, ada_w, ada_b):
    L, D, N6 = ada_w.shape
    B = c.shape[0]
    tn = N6 // 6
    out = pl.pallas_call(
        _ada_kernel,
        grid=(L, N6 // tn),
        in_specs=[pl.BlockSpec((B, D), lambda l, j: (0, 0)),
                  pl.BlockSpec((1, D, tn), lambda l, j: (l, 0, j)),
                  pl.BlockSpec((1, 1, tn), lambda l, j: (l, 0, j))],
        out_specs=pl.BlockSpec((1, B, tn), lambda l, j: (l, 0, j)),
        out_shape=jax.ShapeDtypeStruct((L, B, N6), F32),
        compiler_params=_cparams(("arbitrary", "arbitrary")),
        name="ada_mod",
    )(c, ada_w, ada_b.reshape(L, 1, N6))
    return out.reshape(L, B, 6, N6 // 6)


def _inproj_kernel(x_ref, mod_ref, w_ref, *o_refs, chunk):
    m = mod_ref[0]
    h = (x_ref[0] * (1.0 + m[1:2]) + m[0:1]).astype(BF16)
    off = 0
    for o_ref in o_refs:
        n = o_ref.shape[-1]
        for c0 in range(0, n, chunk):
            c1 = min(c0 + chunk, n)
            o_ref[0, :, c0:c1] = _dot(h, w_ref[:, off + c0:off + c1]).astype(o_ref.dtype)
        off += n


def _inproj(x, mod_l, w_in_bf, widths):
    B, S, D = x.shape
    N = w_in_bf.shape[1]
    tm = min(TM_INPROJ, S)
    out_shape = [jax.ShapeDtypeStruct((B, S, n), BF16) for n in widths]
    out_specs = [pl.BlockSpec((1, tm, n), lambda b, i: (b, i, 0)) for n in widths]
    return pl.pallas_call(
        functools.partial(_inproj_kernel, chunk=512),
        grid=(B, S // tm),
        in_specs=[pl.BlockSpec((1, tm, D), lambda b, i: (b, i, 0)),
                  pl.BlockSpec((1, 6, D), lambda b, i: (b, 0, 0)),
                  pl.BlockSpec((D, N), lambda b, i: (0, 0), pipeline_mode=pl.Buffered(1))],
        out_specs=out_specs,
        out_shape=out_shape,
        compiler_params=_cparams(("parallel", "parallel")),
        name="in_proj",
    )(x, mod_l, w_in_bf)


def _rope_tables(S):
    t = np.arange(S)
    row = (t // GRID_W).astype(np.float64)
    col = (t % GRID_W).astype(np.float64)
    inv = ROPE_THETA ** (-np.arange(AXIAL_FREQS, dtype=np.float64) / AXIAL_FREQS)
    ang_r = (row[:, None].astype(np.float32) * inv.astype(np.float32)[None, :]).astype(np.float32)
    ang_c = (col[:, None].astype(np.float32) * inv.astype(np.float32)[None, :]).astype(np.float32)
    cos_h = np.concatenate([np.cos(ang_r), np.cos(ang_r), np.cos(ang_c), np.cos(ang_c)], -1)
    sin_h = np.concatenate([np.sin(ang_r), np.sin(ang_r), np.sin(ang_c), np.sin(ang_c)], -1)
    cos_t = np.tile(cos_h, (1, ATTN_HEADS)).astype(np.float32)
    sin_t = np.tile(sin_h, (1, ATTN_HEADS)).astype(np.float32)
    return cos_t, sin_t


def _head_consts():
    W = ATTN_HEADS * HEAD_DIM
    d = np.arange(W)
    bind = (d[:, None] // HEAD_DIM == d[None, :] // HEAD_DIM).astype(np.float32) / HEAD_DIM
    psw = np.zeros((W, W), np.float32)
    lowhalf = (d % AXIAL_DIM) < AXIAL_FREQS
    for j in d:
        if lowhalf[j]:
            psw[j + AXIAL_FREQS, j] = -1.0
        else:
            psw[j - AXIAL_FREQS, j] = 1.0
    return bind, psw


def _prep_kernel(pq_ref, pkv_ref, cos_ref, sin_ref, gq_ref, gk_ref, bind_ref, psw_ref,
                 qt_ref, k_ref, vt_ref):
    KW = KV_HEADS * HEAD_DIM
    cos = cos_ref[...]
    sin = sin_ref[...]

    def norm_rope(u, gain, w):
        ms = _dot_x_sel(u * u, bind_ref[:w, :w])
        un = u * lax.rsqrt(ms + RMS_EPS) * gain
        return un * cos[:, :w] + _dot_x_sel(un, psw_ref[:w, :w]) * sin[:, :w]

    q = norm_rope(pq_ref[0].astype(F32), gq_ref[...], ATTN_HEADS * HEAD_DIM)
    qt = (q * (HEAD_DIM ** -0.5 * LOG2_E)).T
    zeros = jnp.zeros((HEAD_DIM, qt.shape[1]), F32)
    for h in range(ATTN_HEADS):
        rows = [zeros] * KV_HEADS
        rows[h // GQA_GROUP] = qt[h * HEAD_DIM:(h + 1) * HEAD_DIM]
        qt_ref[0, h] = jnp.concatenate(rows, axis=0).astype(BF16)
    kv = pkv_ref[0]
    k_ref[0, 0] = norm_rope(kv[:, :KW].astype(F32), gk_ref[...], KW).astype(BF16)
    vt = kv[:, KW:].astype(F32).T
    tm = vt.shape[1]
    extra = (lax.broadcasted_iota(jnp.int32, (V_AUG_ROWS - HEAD_DIM, tm), 0) == 0).astype(F32)
    vt_ref[0, 0] = jnp.concatenate(
        [blk for g in range(KV_HEADS) for blk in (vt[g * HEAD_DIM:(g + 1) * HEAD_DIM], extra)],
        axis=0).astype(BF16)


def _attn_prep(pq, pkv, cos_t, sin_t, gq, gk, bind, psw):
    B, S, QW = pq.shape
    KW = KV_HEADS * HEAD_DIM
    tm = min(TK_ATTN, S)
    return pl.pallas_call(
        _prep_kernel,
        grid=(B, S // tm),
        in_specs=[pl.BlockSpec((1, tm, QW), lambda b, i: (b, i, 0)),
                  pl.BlockSpec((1, tm, 2 * KW), lambda b, i: (b, i, 0)),
                  pl.BlockSpec((tm, QW), lambda b, i: (i, 0)),
                  pl.BlockSpec((tm, QW), lambda b, i: (i, 0)),
                  pl.BlockSpec((1, QW), lambda b, i: (0, 0)),
                  pl.BlockSpec((1, KW), lambda b, i: (0, 0)),
                  pl.BlockSpec((QW, QW), lambda b, i: (0, 0)),
                  pl.BlockSpec((QW, QW), lambda b, i: (0, 0))],
        out_specs=[pl.BlockSpec((1, ATTN_HEADS, KW, tm), lambda b, i: (b, 0, 0, i)),
                   pl.BlockSpec((1, 1, tm, KW), lambda b, i: (b, i, 0, 0)),
                   pl.BlockSpec((1, 1, KV_HEADS * V_AUG_ROWS, tm), lambda b, i: (b, i, 0, 0))],
        out_shape=[jax.ShapeDtypeStruct((B, ATTN_HEADS, KW, S), BF16),
                   jax.ShapeDtypeStruct((B, S // tm, tm, KW), BF16),
                   jax.ShapeDtypeStruct((B, S // tm, KV_HEADS * V_AUG_ROWS, tm), BF16)],
        compiler_params=_cparams(("parallel", "parallel")),
        name="attn_prep",
    )(pq, pkv, cos_t, sin_t, gq, gk, bind, psw)


def _attn_kernel(qt_ref, k_ref, vt_ref, o_ref, st_scr, m_scr, acc_scr, *, nk):
    def scores_to(slot, c):
        kc = k_ref[0, c]
        for h in range(GQA_GROUP):
            st_scr[slot, h] = _dot(kc, qt_ref[0, h])

    def consume(slot, c):
        vc = vt_ref[0, c]
        for h in range(GQA_GROUP):
            st = st_scr[slot, h]
            m_old = m_scr[h]
            m_new = jnp.maximum(m_old, jnp.max(st, axis=0, keepdims=True))
            p = jnp.exp2(st - m_new).astype(BF16)
            acc_scr[h] = jnp.exp2(m_old - m_new) * acc_scr[h] + _dot(vc, p)
            m_scr[h] = m_new

    m_scr[...] = jnp.full(m_scr.shape, -jnp.inf, F32)
    acc_scr[...] = jnp.zeros(acc_scr.shape, F32)
    scores_to(0, 0)

    def body(j, carry):
        c0 = 2 * j
        scores_to(1, c0 + 1)
        consume(0, c0)
        scores_to(0, jnp.minimum(c0 + 2, nk - 1))
        consume(1, c0 + 1)
        return carry
    lax.fori_loop(0, nk // 2, body, 0)

    outs = []
    for h in range(GQA_GROUP):
        acc = acc_scr[h]
        outs.append((acc[:HEAD_DIM] / acc[HEAD_DIM:HEAD_DIM + 1]).T)
    o_ref[0] = jnp.concatenate(outs, axis=-1).astype(o_ref.dtype)


def _attention(qt, k, vt):
    B, nk, tk, KW = k.shape
    S = nk * tk
    GW = GQA_GROUP * HEAD_DIM
    tq = min(TQ_ATTN, S)
    assert nk % 2 == 0
    scratch = [pltpu.VMEM((2, GQA_GROUP, tk, tq), F32), pltpu.VMEM((GQA_GROUP, 1, tq), F32),
               pltpu.VMEM((GQA_GROUP, V_AUG_ROWS, tq), F32)]
    return pl.pallas_call(
        functools.partial(_attn_kernel, nk=nk),
        scratch_shapes=scratch,
        grid=(B, KV_HEADS, S // tq),
        in_specs=[pl.BlockSpec((1, GQA_GROUP, KW, tq), lambda b, g, i: (b, g, 0, i)),
                  pl.BlockSpec((1, nk, tk, KW), lambda b, g, i: (b, 0, 0, 0)),
                  pl.BlockSpec((1, nk, V_AUG_ROWS, tk), lambda b, g, i: (b, 0, g, 0))],
        out_specs=pl.BlockSpec((1, tq, GW), lambda b, g, i: (b, i, g)),
        out_shape=jax.ShapeDtypeStruct((B, S, ATTN_HEADS * HEAD_DIM), BF16),
        compiler_params=_cparams(("parallel", "parallel", "parallel")),
        name="attention",
    )(qt, k, vt)


def _log1p(y):
    u = 1.0 + y
    return jnp.where(u == 1.0, y, jnp.log(u) * (y / (u - 1.0)))


def _softplus(z):
    return jnp.maximum(z, 0.0) + _log1p(jnp.exp(-jnp.abs(z)))


def _neg_expm1_2x(z, ez):
    return -jnp.tanh(z) * (ez * ez + 1.0)


def _gelu_tanh(x):
    return 0.5 * x * (1.0 + jnp.tanh(math.sqrt(2.0 / math.pi) * (x + 0.044715 * (x * x * x))))


def _lru_kernel(*refs, reverse, T):
    if reverse:
        (x_ref, xp_ref, xn_ref, cw_ref, cb_ref, wbd_ref, gb_ref, lam_ref, hf_ref, lg_ref,
         o_ref, carry_ref) = refs
    else:
        (x_ref, xp_ref, xn_ref, cw_ref, cb_ref, wbd_ref, gb_ref, lam_ref,
         o_ref, carry_ref) = refs
    i = pl.program_id(1)
    nt = pl.num_programs(1)
    tile = nt - 1 - i if reverse else i
    W = x_ref.shape[-1]

    @pl.when(i == 0)
    def _():
        carry_ref[...] = jnp.zeros_like(carry_ref)

    x = x_ref[0].astype(F32)
    prev = jnp.where(tile == 0, 0.0, xp_ref[0].astype(F32))
    nxt = jnp.where(tile == nt - 1, 0.0, xn_ref[0].astype(F32))
    row8 = lax.broadcasted_iota(jnp.int32, (V7X_SUBLANES, W), 0)

    def shifted(k):
        if k == 0:
            return x
        r = pltpu.roll(x, (-k) % T, axis=0)
        if k < 0:
            fill = pltpu.roll(prev, (-k) % V7X_SUBLANES, axis=0)
            head = jnp.where(row8 < -k, fill, r[:V7X_SUBLANES])
            return jnp.concatenate([head, r[V7X_SUBLANES:]], axis=0)
        fill = pltpu.roll(nxt, (-k) % V7X_SUBLANES, axis=0)
        tail = jnp.where(row8 >= V7X_SUBLANES - k, fill, r[T - V7X_SUBLANES:])
        return jnp.concatenate([r[:T - V7X_SUBLANES], tail], axis=0)

    cw = cw_ref[...]
    xc = cb_ref[...] + sum(shifted(j - CONV_LEFT) * cw[j:j + 1] for j in range(CONV_W))

    gates = _dot(xc.astype(BF16), wbd_ref[0]) + gb_ref[0]
    r = _sigmoid(gates[:, :W])
    ig = _sigmoid(gates[:, W:])
    log_a = (-LRU_C) * r * _softplus(-lam_ref[0])
    a = jnp.exp(log_a)
    u = jnp.sqrt(_neg_expm1_2x(log_a, a)) * (ig * xc)

    row = lax.broadcasted_iota(jnp.int32, (T, W), 0)

    def neighbour(z, d, fill):
        if d % V7X_SUBLANES == 0:
            pad = jnp.full((d, W), fill, F32)
            return jnp.concatenate([z[d:], pad] if reverse else [pad, z[:T - d]], axis=0)
        if reverse:
            return jnp.where(row < T - d, pltpu.roll(z, T - d, axis=0), fill)
        return jnp.where(row >= d, pltpu.roll(z, d, axis=0), fill)

    A, U = a, u
    d = 1
    while d < T:
        U = A * neighbour(U, d, 0.0) + U
        A = A * neighbour(A, d, 1.0)
        d *= 2
    h = A * carry_ref[...] + U
    if reverse:
        carry_ref[...] = h[0:1]
        o_ref[0] = ((hf_ref[0] + h) * _gelu_tanh(lg_ref[0].astype(F32))).astype(o_ref.dtype)
    else:
        carry_ref[...] = h[T - 1:T]
        o_ref[0] = h


def _lru_pass(plru, conv_w, conv_b, wbd, gb, lam, hf, reverse):
    B, S, W2 = plru.shape
    W = W2 // 2
    T = min(T_LRU, S)
    nt = S // T
    r8 = T // V7X_SUBLANES
    nb8 = S // V7X_SUBLANES
    d = 1 if reverse else 0

    def tmap(i):
        return nt - 1 - i if reverse else i

    in_specs = [
        pl.BlockSpec((1, T, W), lambda b, i: (b, tmap(i), 0)),
        pl.BlockSpec((1, V7X_SUBLANES, W), lambda b, i: (b, jnp.maximum(tmap(i) * r8 - 1, 0), 0)),
        pl.BlockSpec((1, V7X_SUBLANES, W), lambda b, i: (b, jnp.minimum((tmap(i) + 1) * r8, nb8 - 1), 0)),
        pl.BlockSpec((CONV_W, W), lambda b, i: (0, 0)),
        pl.BlockSpec((1, W), lambda b, i: (0, 0)),
        pl.BlockSpec((1, W, 2 * W), lambda b, i: (d, 0, 0)),
        pl.BlockSpec((1, 1, 2 * W), lambda b, i: (d, 0, 0)),
        pl.BlockSpec((1, 1, W), lambda b, i: (d, 0, 0)),
    ]
    args = [plru, plru, plru, conv_w, conv_b, wbd, gb, lam]
    if reverse:
        in_specs += [pl.BlockSpec((1, T, W), lambda b, i: (b, tmap(i), 0)),
                     pl.BlockSpec((1, T, W), lambda b, i: (b, tmap(i), 1))]
        args += [hf, plru]
    return pl.pallas_call(
        functools.partial(_lru_kernel, reverse=reverse, T=T),
        grid=(B, nt),
        in_specs=in_specs,
        out_specs=pl.BlockSpec((1, T, W), lambda b, i: (b, tmap(i), 0)),
        out_shape=jax.ShapeDtypeStruct((B, S, W), BF16 if reverse else F32),
        scratch_shapes=[pltpu.VMEM((1, W), F32)],
        compiler_params=_cparams(("parallel", "arbitrary")),
        name="lru_bwd" if reverse else "lru_fwd",
    )(*args)


def _hgrn_consts(C, reverse):
    nl = int(round(math.log2(C)))
    t = np.arange(C)
    halves = [C >> (lvl + 1) for lvl in range(nl)]
    a_rows, masks, upper = [], [], []
    for m in halves:
        blk = t // (2 * m)
        up = (t % (2 * m)) >= m
        mid = blk * 2 * m + m
        r = t[None, :]
        masks.append((blk[:, None] == blk[None, :]) & up[:, None] & (~up)[None, :])
        if m < V7X_SUBLANES:
            a_up = (r >= mid[:, None]) & (r <= t[:, None])
            a_lo = (r > t[:, None]) & (r <= mid[:, None] - 1)
            a_rows.append(np.where(up[:, None], a_up, a_lo))
            upper.append(up)
    a_rows.append(t[None, :] <= t[:, None])
    a_rows.append(np.ones((V7X_SUBLANES, C), bool))
    masks.append(np.eye(C, dtype=bool))
    if reverse:
        a_rows = [a[::-1, ::-1] if a.shape[0] == C else a for a in a_rows]
        masks = [mm[::-1, ::-1] for mm in masks]
        upper = [u[::-1] for u in upper]
    amat = np.concatenate(a_rows, 0).astype(np.float32)
    msk = np.stack(masks, 0).astype(np.float32)
    upv = np.stack(upper, 0).astype(np.float32)[:, :, None]
    return halves, amat, msk, upv


def _hgrn_kernel(*refs, layer, reverse, C, CB, halves):
    if reverse:
        (hq_ref, hf_ref, hi_ref, lbl_ref, amat_ref, msk_ref, up_ref, of_ref, hg_ref, gain_ref,
         o_ref, st_ref) = refs
    else:
        (hq_ref, hf_ref, hi_ref, lbl_ref, amat_ref, msk_ref, up_ref, o_ref, st_ref) = refs
    H = HGRN_HEADS
    W = hq_ref.shape[-1]
    dk = W // H
    NL = len(halves)
    n_small = sum(1 for m in halves if m < V7X_SUBLANES)

    @pl.when(pl.program_id(1) == 0)
    def _():
        st_ref[...] = jnp.zeros_like(st_ref)

    lgt = lbl_ref[...]
    e = jnp.exp(lgt - jnp.max(lgt, axis=0, keepdims=True))
    p = e / jnp.sum(e, axis=0, keepdims=True)
    lb = jnp.zeros((1, W), F32)
    for j in range(1, layer + 1):
        lb = lb + p[j:j + 1]

    for j in (range(CB - 1, -1, -1) if reverse else range(CB)):
        rows = slice(j * C, (j + 1) * C)
        q = _silu(hq_ref[0, rows].astype(F32))
        f = lb + (1.0 - lb) * _sigmoid(hf_ref[0, rows].astype(F32))
        k = 1.0 - f
        g = jnp.log(f)
        v = hi_ref[0, rows].astype(F32)

        sums = _dot_sel_x(amat_ref[...], g)
        b = sums[n_small * C:(n_small + 1) * C]
        tot = sums[(n_small + 1) * C:(n_small + 1) * C + 1]
        xs = []
        small = 0
        for m in halves:
            if m >= V7X_SUBLANES:
                shp = (C // (2 * m), 2 * m, W)
                b3 = b.reshape(shp)
                r0 = m if reverse else m - 1
                ref = b3[:, r0:r0 + 1, :]
                rowi = lax.broadcasted_iota(jnp.int32, shp, 1)
                qside = (rowi < m) if reverse else (rowi >= m)
                x3 = jnp.where(qside, q.reshape(shp), k.reshape(shp)) * jnp.exp(
                    jnp.where(qside, b3 - ref, ref - b3))
                xs.append(x3.reshape(C, W).astype(BF16))
            else:
                xs.append((jnp.where(up_ref[small] > 0.5, q, k)
                           * jnp.exp(sums[small * C:(small + 1) * C])).astype(BF16))
                small += 1
        qk = q * k
        qb = (q * jnp.exp(b)).astype(BF16)
        kr = (k * jnp.exp(tot - b)).astype(BF16)
        ex_tot = jnp.exp(tot)

        if reverse:
            of = of_ref[0, rows]
            hg = hg_ref[0, rows].astype(F32)
            gain = gain_ref[...]
        for h in range(H):
            sl = slice(h * dk, (h + 1) * dk)
            sc = msk_ref[NL] * jnp.sum(qk[:, sl], axis=-1, keepdims=True)
            for l in range(NL):
                xl = xs[l][:, sl]
                sc = sc + msk_ref[l] * _dot_nt(xl, xl)
            vh = v[:, sl]
            st = st_ref[h]
            o_h = _dot_nt(qb[:, sl], st.astype(BF16)) + _dot(sc.astype(BF16), vh.astype(BF16))
            st_ref[h] = st * ex_tot[:, sl] + _dot(vh.T.astype(BF16), kr[:, sl])
            if reverse:
                tot_o = of[:, sl] + o_h
                ms = jnp.mean(tot_o * tot_o, axis=-1, keepdims=True)
                o_ref[0, rows, sl] = (tot_o * lax.rsqrt(ms + RMS_EPS) * gain
                                      * _silu(hg[:, sl])).astype(o_ref.dtype)
            else:
                o_ref[0, rows, sl] = o_h


def _hgrn_pass(phg, lb_logits, norm_gain, of, layer, reverse):
    B, S, W5 = phg.shape
    W = W5 // 5
    C = min(C_HGRN, S)
    CB = min(CB_HGRN, S // C)
    TB = CB * C
    nc = S // TB
    halves, amat, msk, upv = _hgrn_consts(C, reverse)
    L = lb_logits.shape[0]
    dk = W // HGRN_HEADS

    def cmap(i):
        return nc - 1 - i if reverse else i

    fcol = 2 if reverse else 1
    in_specs = [
        pl.BlockSpec((1, TB, W), lambda b, i: (b, cmap(i), 0)),
        pl.BlockSpec((1, TB, W), lambda b, i: (b, cmap(i), fcol)),
        pl.BlockSpec((1, TB, W), lambda b, i: (b, cmap(i), 3)),
        pl.BlockSpec((L, W), lambda b, i: (0, 0)),
        pl.BlockSpec(amat.shape, lambda b, i: (0, 0)),
        pl.BlockSpec(msk.shape, lambda b, i: (0, 0, 0)),
        pl.BlockSpec(upv.shape, lambda b, i: (0, 0, 0)),
    ]
    args = [phg, phg, phg, lb_logits, jnp.asarray(amat, BF16), jnp.asarray(msk, F32), jnp.asarray(upv, F32)]
    if reverse:
        in_specs += [pl.BlockSpec((1, TB, W), lambda b, i: (b, cmap(i), 0)),
                     pl.BlockSpec((1, TB, W), lambda b, i: (b, cmap(i), 4)),
                     pl.BlockSpec((1, dk), lambda b, i: (0, 0))]
        args += [of, phg, norm_gain.reshape(1, dk)]
    return pl.pallas_call(
        functools.partial(_hgrn_kernel, layer=layer, reverse=reverse, C=C, CB=CB, halves=tuple(halves)),
        grid=(B, nc),
        in_specs=in_specs,
        out_specs=pl.BlockSpec((1, TB, W), lambda b, i: (b, cmap(i), 0)),
        out_shape=jax.ShapeDtypeStruct((B, S, W), BF16 if reverse else F32),
        scratch_shapes=[pltpu.VMEM((HGRN_HEADS, dk, dk), F32)],
        compiler_params=_cparams(("parallel", "arbitrary")),
        name="hgrn_bwd" if reverse else "hgrn_fwd",
    )(*args)


def _merge_kernel(ya_ref, yl_ref, yh_ref, g0_ref, g1_ref, g2_ref, x_ref, mod_ref,
                  wa_ref, wl_ref, wh_ref, wo_ref, lng_ref, lnb_ref, wr_ref,
                  x1_ref, h2_ref, route_ref, *, alpha, n_groups, e_per):
    m = mod_ref[0]
    merged = (_sigmoid(g0_ref[0].astype(F32)) * _dot(ya_ref[0], wa_ref[...])
              + _sigmoid(g1_ref[0].astype(F32)) * _dot(yl_ref[0], wl_ref[...])
              + _sigmoid(g2_ref[0].astype(F32)) * _dot(yh_ref[0], wh_ref[...]))
    mix = _dot(merged.astype(BF16), wo_ref[...])
    x1 = _layernorm(alpha * x_ref[0] + (1.0 + m[2:3]) * mix, lng_ref[...], lnb_ref[...])
    x1_ref[0] = x1
    h2 = x1 * (1.0 + m[4:5]) + m[3:4]
    h2_ref[0] = h2

    xh, xl = _split2(h2)
    r = _dot(xh, wr_ref[...])
    logits = r[:, :V7X_LANES] + r[:, V7X_LANES:] + _dot(xl, wr_ref[:, :V7X_LANES])
    lane = lax.broadcasted_iota(jnp.int32, logits.shape, 1)
    big = jnp.int32(V7X_LANES)
    neg = jnp.float32(-jnp.inf)
    gl = jnp.where(lane < n_groups, logits, neg)
    gmax = jnp.max(gl, axis=-1, keepdims=True)
    gsel = jnp.min(jnp.where(gl == gmax, lane, big), axis=-1, keepdims=True)
    gp = 1.0 / jnp.sum(jnp.exp(gl - gmax), axis=-1, keepdims=True)
    lo = n_groups + gsel * e_per
    el = jnp.where((lane >= lo) & (lane < lo + e_per), logits, neg)
    m1 = jnp.max(el, axis=-1, keepdims=True)
    i1 = jnp.min(jnp.where(el == m1, lane, big), axis=-1, keepdims=True)
    el2 = jnp.where(lane == i1, neg, el)
    m2 = jnp.max(el2, axis=-1, keepdims=True)
    i2 = jnp.min(jnp.where(el2 == m2, lane, big), axis=-1, keepdims=True)
    z = jnp.sum(jnp.exp(el - m1), axis=-1, keepdims=True)
    p1 = 1.0 / z
    p2 = jnp.exp(m2 - m1) / z
    w1 = gp * (p1 / (p1 + p2))
    w2 = gp * (p2 / (p1 + p2))
    e1 = (i1 - n_groups).astype(F32)
    e2 = (i2 - n_groups).astype(F32)
    route_ref[0] = jnp.where(lane == 0, w1, jnp.where(lane == 1, w2,
                             jnp.where(lane == 2, e1, jnp.where(lane == 3, e2, 0.0))))


def _merge(ya, yl, yh, pmg, x, mod_l, wa, wl, wh, wo, lng, lnb, wr, n_groups, e_per, alpha):
    B, S, D = x.shape
    tm = min(TM_MERGE, S)
    bw = ya.shape[-1]
    tok = lambda b, i: (b, i, 0)
    const = lambda b, i: (0, 0)
    return pl.pallas_call(
        functools.partial(_merge_kernel, alpha=alpha, n_groups=n_groups, e_per=e_per),
        grid=(B, S // tm),
        in_specs=[pl.BlockSpec((1, tm, bw), tok), pl.BlockSpec((1, tm, bw), tok), pl.BlockSpec((1, tm, bw), tok),
                  pl.BlockSpec((1, tm, D), lambda b, i: (b, i, 0)),
                  pl.BlockSpec((1, tm, D), lambda b, i: (b, i, 1)),
                  pl.BlockSpec((1, tm, D), lambda b, i: (b, i, 2)),
                  pl.BlockSpec((1, tm, D), tok),
                  pl.BlockSpec((1, 6, D), lambda b, i: (b, 0, 0)),
                  pl.BlockSpec((bw, D), const), pl.BlockSpec((bw, D), const), pl.BlockSpec((bw, D), const),
                  pl.BlockSpec((D, D), const),
                  pl.BlockSpec((1, D), const), pl.BlockSpec((1, D), const),
                  pl.BlockSpec((D, 2 * V7X_LANES), const)],
        out_specs=[pl.BlockSpec((1, tm, D), tok), pl.BlockSpec((1, tm, D), tok),
                   pl.BlockSpec((1, tm, V7X_LANES), tok)],
        out_shape=[jax.ShapeDtypeStruct((B, S, D), F32), jax.ShapeDtypeStruct((B, S, D), F32),
                   jax.ShapeDtypeStruct((B, S, V7X_LANES), F32)],
        compiler_params=_cparams(("parallel", "parallel")),
        name="merge",
    )(ya, yl, yh, pmg, pmg, pmg, x, mod_l, wa, wl, wh, wo, lng, lnb, wr)


def _dispatch_kernel(dest_ref, h_ref, xs_in, xs_hbm, sem, *, TM):
    del xs_in

    def start(r, c):
        for a in range(TOP_K):
            pltpu.make_async_copy(h_ref.at[pl.ds(r, 1)],
                                  xs_hbm.at[pl.ds(dest_ref[0, 0, TOP_K * r + a], 1)], sem).start()
        return c
    lax.fori_loop(0, TM, start, 0, unroll=4)
    for a in range(TOP_K):
        pltpu.make_async_copy(h_ref, xs_hbm.at[pl.ds(0, TM)], sem).wait()


def _dispatch(h2, dest, xs_prev, S):
    N, D = h2.shape
    TM = min(TM_DISPATCH, S)
    nt = N // TM
    return pl.pallas_call(
        functools.partial(_dispatch_kernel, TM=TM),
        grid=(nt,),
        in_specs=[pl.BlockSpec((1, 1, TOP_K * TM), lambda i: (i, 0, 0), memory_space=pltpu.SMEM),
                  pl.BlockSpec((TM, D), lambda i: (i, 0)),
                  pl.BlockSpec(memory_space=pl.ANY)],
        out_specs=pl.BlockSpec(memory_space=pl.ANY),
        out_shape=jax.ShapeDtypeStruct(xs_prev.shape, xs_prev.dtype),
        scratch_shapes=[pltpu.SemaphoreType.DMA(())],
        input_output_aliases={2: 0},
        compiler_params=_cparams(("arbitrary",)),
        name="moe_dispatch",
    )(dest.reshape(nt, 1, TOP_K * TM), h2, xs_prev)


def _expert_kernel(blk_e_ref, nused_ref, xs_ref, w13_ref, w2_ref, ys_ref, *, F):
    i = pl.program_id(0)

    @pl.when(i < nused_ref[0])
    def _():
        ab = _dot(xs_ref[...].astype(BF16), w13_ref[0])
        hmid = (_silu(ab[:, :F]) * ab[:, F:]).astype(BF16)
        ys_ref[...] = _dot(hmid, w2_ref[0])

    @pl.when(i >= nused_ref[0])
    def _():
        ys_ref[...] = jnp.zeros_like(ys_ref)


def _experts(xs, blk_e, nused, w13, w2):
    P, D = xs.shape
    R = R_MOE
    nb = P // R
    F = w2.shape[1]
    grid_spec = pltpu.PrefetchScalarGridSpec(
        num_scalar_prefetch=2,
        grid=(nb,),
        in_specs=[pl.BlockSpec((R, D), lambda i, be, nu: (i, 0)),
                  pl.BlockSpec((1, D, 2 * F), lambda i, be, nu: (be[i], 0, 0)),
                  pl.BlockSpec((1, F, D), lambda i, be, nu: (be[i], 0, 0))],
        out_specs=pl.BlockSpec((R, D), lambda i, be, nu: (i, 0)),
    )
    return pl.pallas_call(
        functools.partial(_expert_kernel, F=F),
        grid_spec=grid_spec,
        out_shape=jax.ShapeDtypeStruct((P, D), F32),
        compiler_params=_cparams(("arbitrary",)),
        name="moe_experts",
    )(blk_e, nused, xs, w13, w2)


def _combine_kernel(pos_ref, posn_ref, ys_hbm, route_ref, x_ref, mod_ref, lng_ref, lnb_ref, o_ref,
                    ybuf, sems, *, TC, alpha):
    i = pl.program_id(0)
    n = pl.num_programs(0)
    slot = i % 2

    def issue(p_ref, s):
        def body(r, c):
            for a in range(TOP_K):
                pltpu.make_async_copy(ys_hbm.at[pl.ds(p_ref[0, 0, TOP_K * r + a], 1)],
                                      ybuf.at[s, a, pl.ds(r, 1)], sems.at[s]).start()
            return c
        lax.fori_loop(0, TC, body, 0, unroll=4)

    @pl.when(i == 0)
    def _():
        issue(pos_ref, 0)

    @pl.when(i + 1 < n)
    def _():
        issue(posn_ref, 1 - slot)

    for a in range(TOP_K):
        pltpu.make_async_copy(ys_hbm.at[pl.ds(0, TC)], ybuf.at[slot, a], sems.at[slot]).wait()

    m = mod_ref[0]
    w = route_ref[...]
    ffn = w[:, 0:1] * ybuf[slot, 0]
    for a in range(1, TOP_K):
        ffn = ffn + w[:, a:a + 1] * ybuf[slot, a]
    o_ref[...] = _layernorm(alpha * x_ref[...] + (1.0 + m[5:6]) * ffn, lng_ref[...], lnb_ref[...])


def _combine(ys, dest, route, x1, mod_l, lng, lnb, S, alpha):
    N, D = x1.shape
    TC = min(TC_MOE, S)
    nt = N // TC
    per_b = S // TC
    pos = dest.reshape(nt, 1, TOP_K * TC)
    return pl.pallas_call(
        functools.partial(_combine_kernel, TC=TC, alpha=alpha),
        grid=(nt,),
        in_specs=[pl.BlockSpec((1, 1, TOP_K * TC), lambda i: (i, 0, 0), memory_space=pltpu.SMEM),
                  pl.BlockSpec((1, 1, TOP_K * TC), lambda i: (jnp.minimum(i + 1, nt - 1), 0, 0),
                               memory_space=pltpu.SMEM),
                  pl.BlockSpec(memory_space=pl.ANY),
                  pl.BlockSpec((TC, V7X_LANES), lambda i: (i, 0)),
                  pl.BlockSpec((TC, D), lambda i: (i, 0)),
                  pl.BlockSpec((1, 6, D), lambda i: (i // per_b, 0, 0)),
                  pl.BlockSpec((1, D), lambda i: (0, 0)),
                  pl.BlockSpec((1, D), lambda i: (0, 0))],
        out_specs=pl.BlockSpec((TC, D), lambda i: (i, 0)),
        out_shape=jax.ShapeDtypeStruct((N, D), F32),
        scratch_shapes=[pltpu.VMEM((2, TOP_K, TC, D), F32), pltpu.SemaphoreType.DMA((2,))],
        compiler_params=_cparams(("arbitrary",)),
        name="moe_combine",
    )(pos, pos, ys, route, x1, mod_l, lng, lnb)


def _dispatch_plan(route, n_experts, R):
    eid = route[:, TOP_K:2 * TOP_K].astype(jnp.int32)
    hot = (eid[:, :, None] == jnp.arange(n_experts, dtype=jnp.int32)).astype(jnp.int32).sum(1)
    csum = jnp.cumsum(hot, axis=0)
    counts = csum[-1]
    rank = jnp.take_along_axis(csum - hot, eid, axis=1)
    pcounts = (counts + R - 1) // R * R
    pends = jnp.cumsum(pcounts)
    pstarts = pends - pcounts
    dest = (pstarts[eid] + rank).astype(jnp.int32)
    nb = (eid.shape[0] * TOP_K + n_experts * R) // R
    starts = jnp.arange(nb, dtype=jnp.int32) * R
    blk_e = jnp.minimum((pends[None, :] <= starts[:, None]).astype(jnp.int32).sum(1), n_experts - 1)
    nused = (pends[-1:] // R).astype(jnp.int32)
    return blk_e, nused, dest


def _prep_weights(p):
    L = p['w_in'].shape[0]
    W = p['lru_conv_w'].shape[-1]
    nblk, bs = p['lru_wa'].shape[2], p['lru_wa'].shape[3]

    def blockdiag(w):
        eye = jnp.eye(nblk, dtype=w.dtype)
        return jnp.einsum('ldnij,nm->ldnimj', w, eye).reshape(L, 2, W, W)

    G, E = p['router_expert_w'].shape[1], p['router_expert_w'].shape[3]
    D = p['w_in'].shape[1]
    wr = jnp.concatenate([p['router_group_w'],
                          p['router_expert_w'].transpose(0, 2, 1, 3).reshape(L, D, G * E)], -1)
    wr = jnp.pad(wr, ((0, 0), (0, 0), (0, V7X_LANES - wr.shape[-1])))
    wr_hi = wr.astype(BF16)
    wr = jnp.concatenate([wr_hi, (wr - wr_hi.astype(F32)).astype(BF16)], -1)
    return dict(
        w_in=p['w_in'].astype(BF16),
        gq=jnp.tile(p['attn_q_gain'], (1, ATTN_HEADS))[:, None, :],
        gk=jnp.tile(p['attn_k_gain'], (1, KV_HEADS))[:, None, :],
        w_attn_o=p['w_attn_o'].astype(BF16),
        conv_w=p['lru_conv_w'], conv_b=p['lru_conv_b'][:, None, :],
        wbd=jnp.concatenate([blockdiag(p['lru_wa']), blockdiag(p['lru_wi'])], -1).astype(BF16),
        gb=jnp.concatenate([p['lru_ba'], p['lru_bi']], -1)[:, :, None, :],
        lam=p['lru_lambda'][:, :, None, :],
        w_lru_o=p['w_lru_o'].astype(BF16),
        w_hgrn_o=p['w_hgrn_o'].astype(BF16),
        w_out=p['w_out'].astype(BF16),
        ln1_g=p['ln1_g'][:, None, :], ln1_b=p['ln1_b'][:, None, :],
        ln2_g=p['ln2_g'][:, None, :], ln2_b=p['ln2_b'][:, None, :],
        wr=wr, n_groups=G, e_per=E,
        w13=p['expert_w13'].astype(BF16), w2=p['expert_w2'].astype(BF16),
    )


def _trunk(x, c, p, wp):
    B, S, D = x.shape
    L = p['w_in'].shape[0]
    alpha = (2 * L) ** 0.25
    QW = ATTN_HEADS * HEAD_DIM
    KW = KV_HEADS * HEAD_DIM
    LW = p['lru_conv_w'].shape[-1]
    HW = p['hgrn_lb_logits'].shape[-1]
    widths = (QW, 2 * KW, 2 * LW, 5 * HW, N_BRANCH * D)
    assert sum(widths) == p['w_in'].shape[-1]
    n_experts = wp['n_groups'] * wp['e_per']

    mod = _ada_mod(c, p['ada_w'], p['ada_b'])
    cos_t, sin_t = _rope_tables(S)
    bind, psw = _head_consts()
    cos_t, sin_t = jnp.asarray(cos_t), jnp.asarray(sin_t)
    bind, psw = jnp.asarray(bind, BF16), jnp.asarray(psw, BF16)

    xs = None
    for l in range(L):
        pq, pkv, plru, phg, pmg = _inproj(x, mod[l], wp['w_in'][l], widths)
        qt, k, vt = _attn_prep(pq, pkv, cos_t, sin_t, wp['gq'][l], wp['gk'][l], bind, psw)
        ya = _attention(qt, k, vt)
        lru_args = (plru, wp['conv_w'][l], wp['conv_b'][l], wp['wbd'][l], wp['gb'][l], wp['lam'][l])
        hf = _lru_pass(*lru_args, None, reverse=False)
        yl = _lru_pass(*lru_args, hf, reverse=True)
        of = _hgrn_pass(phg, p['hgrn_lb_logits'], p['hgrn_norm_gain'][l], None, l, reverse=False)
        yh = _hgrn_pass(phg, p['hgrn_lb_logits'], p['hgrn_norm_gain'][l], of, l, reverse=True)
        x1, h2, route = _merge(ya, yl, yh, pmg, x, mod[l], wp['w_attn_o'][l], wp['w_lru_o'][l],
                               wp['w_hgrn_o'][l], wp['w_out'][l], wp['ln1_g'][l], wp['ln1_b'][l],
                               wp['wr'][l], wp['n_groups'], wp['e_per'], alpha)
        N = B * S
        route = route.reshape(N, V7X_LANES)
        blk_e, nused, dest = _dispatch_plan(route, n_experts, R_MOE)
        if xs is None:
            xs = jnp.zeros((N * TOP_K + n_experts * R_MOE, D), F32)
        xs = _dispatch(h2.reshape(N, D), dest, xs, S)
        ys = _experts(xs, blk_e, nused, wp['w13'][l], wp['w2'][l])
        x = _combine(ys, dest, route, x1.reshape(N, D), mod[l], wp['ln2_g'][l], wp['ln2_b'][l], S,
                     alpha).reshape(B, S, D)
    return x


def kernel(x_prompt, x_sample, c_prompt, c_sample, ada_w, ada_b, w_in, attn_q_gain, attn_k_gain, w_attn_o, lru_conv_w, lru_conv_b, lru_wa, lru_ba, lru_wi, lru_bi, lru_lambda, w_lru_o, hgrn_lb_logits, hgrn_norm_gain, w_hgrn_o, w_out, ln1_g, ln1_b, router_group_w, router_expert_w, expert_w13, expert_w2, ln2_g, ln2_b):
    p = dict(ada_w=ada_w, ada_b=ada_b, w_in=w_in, attn_q_gain=attn_q_gain, attn_k_gain=attn_k_gain,
             w_attn_o=w_attn_o, lru_conv_w=lru_conv_w, lru_conv_b=lru_conv_b, lru_wa=lru_wa, lru_ba=lru_ba,
             lru_wi=lru_wi, lru_bi=lru_bi, lru_lambda=lru_lambda, w_lru_o=w_lru_o,
             hgrn_lb_logits=hgrn_lb_logits, hgrn_norm_gain=hgrn_norm_gain, w_hgrn_o=w_hgrn_o, w_out=w_out,
             ln1_g=ln1_g, ln1_b=ln1_b, router_group_w=router_group_w, router_expert_w=router_expert_w,
             expert_w13=expert_w13, expert_w2=expert_w2, ln2_g=ln2_g, ln2_b=ln2_b)
    wp = _prep_weights(p)
    return (_trunk(x_prompt, c_prompt, p, wp), _trunk(x_sample, c_sample, p, wp))
```

```python
import functools
import math

import numpy as np
import jax
import jax.numpy as jnp
from jax import lax
from jax.experimental import pallas as pl
from jax.experimental.pallas import tpu as pltpu

F32 = jnp.float32
BF16 = jnp.bfloat16

GRID_W = 64
CONV_W = 4
CONV_LEFT = 2
LRU_C = 8.0
ATTN_HEADS = 8
KV_HEADS = 2
HEAD_DIM = 64
GQA_GROUP = ATTN_HEADS // KV_HEADS
AXIAL_DIM = HEAD_DIM // 2
AXIAL_FREQS = AXIAL_DIM // 2
ROPE_THETA = 10000.0
HGRN_HEADS = 4
N_BRANCH = 3
TOP_K = 2
LN_EPS = 1e-5
RMS_EPS = 1e-6
LOG2_E = 1.4426950408889634

V7X_LANES = 128
V7X_SUBLANES = 8
V7X_VMEM_LIMIT_BYTES = 56 * 1024 * 1024

TM_INPROJ = 512
V_AUG_ROWS = HEAD_DIM + 16
TQ_ATTN = 512
T_LRU = 256
C_HGRN = 128
CB_HGRN = 4
TK_ATTN = 512
TM_MERGE = 512
R_MOE = 256
TM_DISPATCH = 512
TC_MOE = 256


def _cparams(sem):
    return pltpu.CompilerParams(dimension_semantics=sem, vmem_limit_bytes=V7X_VMEM_LIMIT_BYTES)


def _dot(a, b):
    return jnp.dot(a, b, preferred_element_type=F32)


def _dot_nt(a, b):
    return lax.dot_general(a, b, (((1,), (1,)), ((), ())), preferred_element_type=F32)


def _dot_hi(a, b):
    return jnp.dot(a, b, preferred_element_type=F32, precision=lax.Precision.HIGHEST)


def _split2(x):
    hi = x.astype(BF16)
    lo = (x - hi.astype(F32)).astype(BF16)
    return hi, lo


def _dot_x_sel(x, m):
    hi, lo = _split2(x)
    return _dot(hi, m) + _dot(lo, m)


def _dot_sel_x(m, x):
    hi, lo = _split2(x)
    return _dot(m, hi) + _dot(m, lo)


def _sigmoid(x):
    return 1.0 / (1.0 + jnp.exp(-x))


def _silu(x):
    return x * _sigmoid(x)


def _layernorm(y, g, b):
    mu = jnp.mean(y, axis=-1, keepdims=True)
    d = y - mu
    var = jnp.mean(d * d, axis=-1, keepdims=True)
    return d * lax.rsqrt(var + LN_EPS) * g + b


def _ada_kernel(c_ref, w_ref, b_ref, o_ref):
    c = c_ref[...]
    o_ref[0] = _dot_hi(_silu(c), w_ref[0]) + b_ref[0]


def _ada_mod(c, ada_w, ada_b):
    L, D, N6 = ada_w.shape
    B = c.shape[0]
    tn = N6 // 6
    out = pl.pallas_call(
        _ada_kernel,
        grid=(L, N6 // tn),
        in_specs=[pl.BlockSpec((B, D), lambda l, j: (0, 0)),
                  pl.BlockSpec((1, D, tn), lambda l, j: (l, 0, j)),
                  pl.BlockSpec((1, 1, tn), lambda l, j: (l, 0, j))],
        out_specs=pl.BlockSpec((1, B, tn), lambda l, j: (l, 0, j)),
        out_shape=jax.ShapeDtypeStruct((L, B, N6), F32),
        compiler_params=_cparams(("arbitrary", "arbitrary")),
        name="ada_mod",
    )(c, ada_w, ada_b.reshape(L, 1, N6))
    return out.reshape(L, B, 6, N6 // 6)


def _inproj_kernel(x_ref, mod_ref, w_ref, *o_refs, chunk):
    m = mod_ref[0]
    h = (x_ref[0] * (1.0 + m[1:2]) + m[0:1]).astype(BF16)
    off = 0
    for o_ref in o_refs:
        n = o_ref.shape[-1]
        for c0 in range(0, n, chunk):
            c1 = min(c0 + chunk, n)
            o_ref[0, :, c0:c1] = _dot(h, w_ref[:, off + c0:off + c1]).astype(o_ref.dtype)
        off += n


def _inproj(x, mod_l, w_in_bf, widths):
    B, S, D = x.shape
    N = w_in_bf.shape[1]
    tm = min(TM_INPROJ, S)
    out_shape = [jax.ShapeDtypeStruct((B, S, n), BF16) for n in widths]
    out_specs = [pl.BlockSpec((1, tm, n), lambda b, i: (b, i, 0)) for n in widths]
    return pl.pallas_call(
        functools.partial(_inproj_kernel, chunk=512),
        grid=(B, S // tm),
        in_specs=[pl.BlockSpec((1, tm, D), lambda b, i: (b, i, 0)),
                  pl.BlockSpec((1, 6, D), lambda b, i: (b, 0, 0)),
                  pl.BlockSpec((D, N), lambda b, i: (0, 0), pipeline_mode=pl.Buffered(1))],
        out_specs=out_specs,
        out_shape=out_shape,
        compiler_params=_cparams(("parallel", "parallel")),
        name="in_proj",
    )(x, mod_l, w_in_bf)


def _rope_tables(S):
    t = np.arange(S)
    row = (t // GRID_W).astype(np.float64)
    col = (t % GRID_W).astype(np.float64)
    inv = ROPE_THETA ** (-np.arange(AXIAL_FREQS, dtype=np.float64) / AXIAL_FREQS)
    ang_r = (row[:, None].astype(np.float32) * inv.astype(np.float32)[None, :]).astype(np.float32)
    ang_c = (col[:, None].astype(np.float32) * inv.astype(np.float32)[None, :]).astype(np.float32)
    cos_h = np.concatenate([np.cos(ang_r), np.cos(ang_r), np.cos(ang_c), np.cos(ang_c)], -1)
    sin_h = np.concatenate([np.sin(ang_r), np.sin(ang_r), np.sin(ang_c), np.sin(ang_c)], -1)
    cos_t = np.tile(cos_h, (1, ATTN_HEADS)).astype(np.float32)
    sin_t = np.tile(sin_h, (1, ATTN_HEADS)).astype(np.float32)
    return cos_t, sin_t


def _head_consts():
    W = ATTN_HEADS * HEAD_DIM
    d = np.arange(W)
    bind = (d[:, None] // HEAD_DIM == d[None, :] // HEAD_DIM).astype(np.float32) / HEAD_DIM
    psw = np.zeros((W, W), np.float32)
    lowhalf = (d % AXIAL_DIM) < AXIAL_FREQS
    for j in d:
        if lowhalf[j]:
            psw[j + AXIAL_FREQS, j] = -1.0
        else:
            psw[j - AXIAL_FREQS, j] = 1.0
    return bind, psw


def _prep_kernel(pq_ref, pkv_ref, cos_ref, sin_ref, gq_ref, gk_ref, bind_ref, psw_ref,
                 qt_ref, k_ref, vt_ref):
    KW = KV_HEADS * HEAD_DIM
    cos = cos_ref[...]
    sin = sin_ref[...]

    def norm_rope(u, gain, w):
        ms = _dot_x_sel(u * u, bind_ref[:w, :w])
        un = u * lax.rsqrt(ms + RMS_EPS) * gain
        return un * cos[:, :w] + _dot_x_sel(un, psw_ref[:w, :w]) * sin[:, :w]

    q = norm_rope(pq_ref[0].astype(F32), gq_ref[...], ATTN_HEADS * HEAD_DIM)
    qt = (q * (HEAD_DIM ** -0.5 * LOG2_E)).T
    zeros = jnp.zeros((HEAD_DIM, qt.shape[1]), F32)
    for h in range(ATTN_HEADS):
        rows = [zeros] * KV_HEADS
        rows[h // GQA_GROUP] = qt[h * HEAD_DIM:(h + 1) * HEAD_DIM]
        qt_ref[0, h] = jnp.concatenate(rows, axis=0).astype(BF16)
    kv = pkv_ref[0]
    k_ref[0, 0] = norm_rope(kv[:, :KW].astype(F32), gk_ref[...], KW).astype(BF16)
    vt = kv[:, KW:].astype(F32).T
    tm = vt.shape[1]
    extra = (lax.broadcasted_iota(jnp.int32, (V_AUG_ROWS - HEAD_DIM, tm), 0) == 0).astype(F32)
    vt_ref[0, 0] = jnp.concatenate(
        [blk for g in range(KV_HEADS) for blk in (vt[g * HEAD_DIM:(g + 1) * HEAD_DIM], extra)],
        axis=0).astype(BF16)


def _attn_prep(pq, pkv, cos_t, sin_t, gq, gk, bind, psw):
    B, S, QW = pq.shape
    KW = KV_HEADS * HEAD_DIM
    tm = min(TK_ATTN, S)
    return pl.pallas_call(
        _prep_kernel,
        grid=(B, S // tm),
        in_specs=[pl.BlockSpec((1, tm, QW), lambda b, i: (b, i, 0)),
                  pl.BlockSpec((1, tm, 2 * KW), lambda b, i: (b, i, 0)),
                  pl.BlockSpec((tm, QW), lambda b, i: (i, 0)),
                  pl.BlockSpec((tm, QW), lambda b, i: (i, 0)),
                  pl.BlockSpec((1, QW), lambda b, i: (0, 0)),
                  pl.BlockSpec((1, KW), lambda b, i: (0, 0)),
                  pl.BlockSpec((QW, QW), lambda b, i: (0, 0)),
                  pl.BlockSpec((QW, QW), lambda b, i: (0, 0))],
        out_specs=[pl.BlockSpec((1, ATTN_HEADS, KW, tm), lambda b, i: (b, 0, 0, i)),
                   pl.BlockSpec((1, 1, tm, KW), lambda b, i: (b, i, 0, 0)),
                   pl.BlockSpec((1, 1, KV_HEADS * V_AUG_ROWS, tm), lambda b, i: (b, i, 0, 0))],
        out_shape=[jax.ShapeDtypeStruct((B, ATTN_HEADS, KW, S), BF16),
                   jax.ShapeDtypeStruct((B, S // tm, tm, KW), BF16),
                   jax.ShapeDtypeStruct((B, S // tm, KV_HEADS * V_AUG_ROWS, tm), BF16)],
        compiler_params=_cparams(("parallel", "parallel")),
        name="attn_prep",
    )(pq, pkv, cos_t, sin_t, gq, gk, bind, psw)


def _attn_kernel(qt_ref, k_ref, vt_ref, o_ref, st_scr, m_scr, acc_scr, *, nk):
    def scores_to(slot, c):
        kc = k_ref[0, c]
        for h in range(GQA_GROUP):
            st_scr[slot, h] = _dot(kc, qt_ref[0, h])

    def consume(slot, c):
        vc = vt_ref[0, c]
        for h in range(GQA_GROUP):
            st = st_scr[slot, h]
            m_old = m_scr[h]
            m_new = jnp.maximum(m_old, jnp.max(st, axis=0, keepdims=True))
            p = jnp.exp2(st - m_new).astype(BF16)
            acc_scr[h] = jnp.exp2(m_old - m_new) * acc_scr[h] + _dot(vc, p)
            m_scr[h] = m_new

    m_scr[...] = jnp.full(m_scr.shape, -jnp.inf, F32)
    acc_scr[...] = jnp.zeros(acc_scr.shape, F32)
    scores_to(0, 0)

    def step(nslot, cn, cslot, cc):
        kc = k_ref[0, cn]
        vc = vt_ref[0, cc]
        for h in range(GQA_GROUP):
            st_scr[nslot, h] = _dot(kc, qt_ref[0, h])
            st = st_scr[cslot, h]
            m_old = m_scr[h]
            m_new = jnp.maximum(m_old, jnp.max(st, axis=0, keepdims=True))
            p = jnp.exp2(st - m_new).astype(BF16)
            acc_scr[h] = jnp.exp2(m_old - m_new) * acc_scr[h] + _dot(vc, p)
            m_scr[h] = m_new

    def body(j, carry):
        c0 = 2 * j
        step(1, c0 + 1, 0, c0)
        step(0, c0 + 2, 1, c0 + 1)
        return carry
    lax.fori_loop(0, nk // 2 - 1, body, 0)
    step(1, nk - 1, 0, nk - 2)
    consume(1, nk - 1)

    outs = []
    for h in range(GQA_GROUP):
        acc = acc_scr[h]
        outs.append((acc[:HEAD_DIM] / acc[HEAD_DIM:HEAD_DIM + 1]).T)
    o_ref[0] = jnp.concatenate(outs, axis=-1).astype(o_ref.dtype)


def _attention(qt, k, vt):
    B, nk, tk, KW = k.shape
    S = nk * tk
    GW = GQA_GROUP * HEAD_DIM
    tq = min(TQ_ATTN, S)
    assert nk % 2 == 0
    scratch = [pltpu.VMEM((2, GQA_GROUP, tk, tq), F32), pltpu.VMEM((GQA_GROUP, 1, tq), F32),
               pltpu.VMEM((GQA_GROUP, V_AUG_ROWS, tq), F32)]
    return pl.pallas_call(
        functools.partial(_attn_kernel, nk=nk),
        scratch_shapes=scratch,
        grid=(B, KV_HEADS, S // tq),
        in_specs=[pl.BlockSpec((1, GQA_GROUP, KW, tq), lambda b, g, i: (b, g, 0, i)),
                  pl.BlockSpec((1, nk, tk, KW), lambda b, g, i: (b, 0, 0, 0)),
                  pl.BlockSpec((1, nk, V_AUG_ROWS, tk), lambda b, g, i: (b, 0, g, 0))],
        out_specs=pl.BlockSpec((1, tq, GW), lambda b, g, i: (b, i, g)),
        out_shape=jax.ShapeDtypeStruct((B, S, ATTN_HEADS * HEAD_DIM), BF16),
        compiler_params=_cparams(("parallel", "parallel", "parallel")),
        name="attention",
    )(qt, k, vt)


def _log1p(y):
    u = 1.0 + y
    return jnp.where(u == 1.0, y, jnp.log(u) * (y / (u - 1.0)))


def _softplus(z):
    return jnp.maximum(z, 0.0) + _log1p(jnp.exp(-jnp.abs(z)))


def _neg_expm1_2x(z, ez):
    return -jnp.tanh(z) * (ez * ez + 1.0)


def _gelu_tanh(x):
    return 0.5 * x * (1.0 + jnp.tanh(math.sqrt(2.0 / math.pi) * (x + 0.044715 * (x * x * x))))


def _lru_kernel(*refs, reverse, T):
    if reverse:
        (x_ref, xp_ref, xn_ref, cw_ref, cb_ref, wbd_ref, gb_ref, lam_ref, hf_ref, lg_ref,
         o_ref, carry_ref) = refs
    else:
        (x_ref, xp_ref, xn_ref, cw_ref, cb_ref, wbd_ref, gb_ref, lam_ref,
         o_ref, carry_ref) = refs
    i = pl.program_id(1)
    nt = pl.num_programs(1)
    tile = nt - 1 - i if reverse else i
    W = x_ref.shape[-1]

    @pl.when(i == 0)
    def _():
        carry_ref[...] = jnp.zeros_like(carry_ref)

    x = x_ref[0].astype(F32)
    prev = jnp.where(tile == 0, 0.0, xp_ref[0].astype(F32))
    nxt = jnp.where(tile == nt - 1, 0.0, xn_ref[0].astype(F32))
    row8 = lax.broadcasted_iota(jnp.int32, (V7X_SUBLANES, W), 0)

    def shifted(k):
        if k == 0:
            return x
        r = pltpu.roll(x, (-k) % T, axis=0)
        if k < 0:
            fill = pltpu.roll(prev, (-k) % V7X_SUBLANES, axis=0)
            head = jnp.where(row8 < -k, fill, r[:V7X_SUBLANES])
            return jnp.concatenate([head, r[V7X_SUBLANES:]], axis=0)
        fill = pltpu.roll(nxt, (-k) % V7X_SUBLANES, axis=0)
        tail = jnp.where(row8 >= V7X_SUBLANES - k, fill, r[T - V7X_SUBLANES:])
        return jnp.concatenate([r[:T - V7X_SUBLANES], tail], axis=0)

    cw = cw_ref[...]
    xc = cb_ref[...] + sum(shifted(j - CONV_LEFT) * cw[j:j + 1] for j in range(CONV_W))

    gates = _dot(xc.astype(BF16), wbd_ref[0]) + gb_ref[0]
    r = _sigmoid(gates[:, :W])
    ig = _sigmoid(gates[:, W:])
    log_a = (-LRU_C) * r * _softplus(-lam_ref[0])
    a = jnp.exp(log_a)
    u = jnp.sqrt(_neg_expm1_2x(log_a, a)) * (ig * xc)

    row = lax.broadcasted_iota(jnp.int32, (T, W), 0)

    def neighbour(z, d, fill):
        if d % V7X_SUBLANES == 0:
            pad = jnp.full((d, W), fill, F32)
            return jnp.concatenate([z[d:], pad] if reverse else [pad, z[:T - d]], axis=0)
        if reverse:
            return jnp.where(row < T - d, pltpu.roll(z, T - d, axis=0), fill)
        return jnp.where(row >= d, pltpu.roll(z, d, axis=0), fill)

    A, U = a, u
    d = 1
    while d < T:
        U = A * neighbour(U, d, 0.0) + U
        A = A * neighbour(A, d, 1.0)
        d *= 2
    h = A * carry_ref[...] + U
    if reverse:
        carry_ref[...] = h[0:1]
        o_ref[0] = ((hf_ref[0] + h) * _gelu_tanh(lg_ref[0].astype(F32))).astype(o_ref.dtype)
    else:
        carry_ref[...] = h[T - 1:T]
        o_ref[0] = h


def _lru_pass(plru, conv_w, conv_b, wbd, gb, lam, hf, reverse):
    B, S, W2 = plru.shape
    W = W2 // 2
    T = min(T_LRU, S)
    nt = S // T
    r8 = T // V7X_SUBLANES
    nb8 = S // V7X_SUBLANES
    d = 1 if reverse else 0

    def tmap(i):
        return nt - 1 - i if reverse else i

    in_specs = [
        pl.BlockSpec((1, T, W), lambda b, i: (b, tmap(i), 0)),
        pl.BlockSpec((1, V7X_SUBLANES, W), lambda b, i: (b, jnp.maximum(tmap(i) * r8 - 1, 0), 0)),
        pl.BlockSpec((1, V7X_SUBLANES, W), lambda b, i: (b, jnp.minimum((tmap(i) + 1) * r8, nb8 - 1), 0)),
        pl.BlockSpec((CONV_W, W), lambda b, i: (0, 0)),
        pl.BlockSpec((1, W), lambda b, i: (0, 0)),
        pl.BlockSpec((1, W, 2 * W), lambda b, i: (d, 0, 0)),
        pl.BlockSpec((1, 1, 2 * W), lambda b, i: (d, 0, 0)),
        pl.BlockSpec((1, 1, W), lambda b, i: (d, 0, 0)),
    ]
    args = [plru, plru, plru, conv_w, conv_b, wbd, gb, lam]
    if reverse:
        in_specs += [pl.BlockSpec((1, T, W), lambda b, i: (b, tmap(i), 0)),
                     pl.BlockSpec((1, T, W), lambda b, i: (b, tmap(i), 1))]
        args += [hf, plru]
    return pl.pallas_call(
        functools.partial(_lru_kernel, reverse=reverse, T=T),
        grid=(B, nt),
        in_specs=in_specs,
        out_specs=pl.BlockSpec((1, T, W), lambda b, i: (b, tmap(i), 0)),
        out_shape=jax.ShapeDtypeStruct((B, S, W), BF16 if reverse else F32),
        scratch_shapes=[pltpu.VMEM((1, W), F32)],
        compiler_params=_cparams(("parallel", "arbitrary")),
        name="lru_bwd" if reverse else "lru_fwd",
    )(*args)


def _hgrn_consts(C, reverse):
    nl = int(round(math.log2(C)))
    t = np.arange(C)
    halves = [C >> (lvl + 1) for lvl in range(nl)]
    a_rows, masks, upper = [], [], []
    for m in halves:
        blk = t // (2 * m)
        up = (t % (2 * m)) >= m
        mid = blk * 2 * m + m
        r = t[None, :]
        masks.append((blk[:, None] == blk[None, :]) & up[:, None] & (~up)[None, :])
        if m < V7X_SUBLANES:
            a_up = (r >= mid[:, None]) & (r <= t[:, None])
            a_lo = (r > t[:, None]) & (r <= mid[:, None] - 1)
            a_rows.append(np.where(up[:, None], a_up, a_lo))
            upper.append(up)
    a_rows.append(t[None, :] <= t[:, None])
    a_rows.append(np.ones((V7X_SUBLANES, C), bool))
    masks.append(np.eye(C, dtype=bool))
    if reverse:
        a_rows = [a[::-1, ::-1] if a.shape[0] == C else a for a in a_rows]
        masks = [mm[::-1, ::-1] for mm in masks]
        upper = [u[::-1] for u in upper]
    amat = np.concatenate(a_rows, 0).astype(np.float32)
    msk = np.stack(masks, 0).astype(np.float32)
    upv = np.stack(upper, 0).astype(np.float32)[:, :, None]
    return halves, amat, msk, upv


def _hgrn_kernel(*refs, layer, reverse, C, CB, halves):
    if reverse:
        (hq_ref, hf_ref, hi_ref, lbl_ref, amat_ref, msk_ref, up_ref, of_ref, hg_ref, gain_ref,
         o_ref, st_ref) = refs
    else:
        (hq_ref, hf_ref, hi_ref, lbl_ref, amat_ref, msk_ref, up_ref, o_ref, st_ref) = refs
    H = HGRN_HEADS
    W = hq_ref.shape[-1]
    dk = W // H
    NL = len(halves)
    n_small = sum(1 for m in halves if m < V7X_SUBLANES)

    @pl.when(pl.program_id(1) == 0)
    def _():
        st_ref[...] = jnp.zeros_like(st_ref)

    lgt = lbl_ref[...]
    e = jnp.exp(lgt - jnp.max(lgt, axis=0, keepdims=True))
    p = e / jnp.sum(e, axis=0, keepdims=True)
    lb = jnp.zeros((1, W), F32)
    for j in range(1, layer + 1):
        lb = lb + p[j:j + 1]

    def gates_and_factors(j):
        rows = slice(j * C, (j + 1) * C)
        q = _silu(hq_ref[0, rows].astype(F32))
        f = lb + (1.0 - lb) * _sigmoid(hf_ref[0, rows].astype(F32))
        k = 1.0 - f
        g = jnp.log(f)
        v = hi_ref[0, rows].astype(F32)

        sums = _dot_sel_x(amat_ref[...], g)
        b = sums[n_small * C:(n_small + 1) * C]
        tot = sums[(n_small + 1) * C:(n_small + 1) * C + 1]
        xs = []
        small = 0
        for m in halves:
            if m >= V7X_SUBLANES:
                shp = (C // (2 * m), 2 * m, W)
                b3 = b.reshape(shp)
                r0 = m if reverse else m - 1
                ref = b3[:, r0:r0 + 1, :]
                rowi = lax.broadcasted_iota(jnp.int32, shp, 1)
                qside = (rowi < m) if reverse else (rowi >= m)
                x3 = jnp.where(qside, q.reshape(shp), k.reshape(shp)) * jnp.exp(
                    jnp.where(qside, b3 - ref, ref - b3))
                xs.append(x3.reshape(C, W).astype(BF16))
            else:
                xs.append((jnp.where(up_ref[small] > 0.5, q, k)
                           * jnp.exp(sums[small * C:(small + 1) * C])).astype(BF16))
                small += 1
        return dict(rows=rows, v=v.astype(BF16), vt=v.T.astype(BF16), xs=xs, qk=q * k,
                    qb=(q * jnp.exp(b)).astype(BF16), kr=(k * jnp.exp(tot - b)).astype(BF16),
                    ex_tot=jnp.exp(tot))

    def scores_and_state(a):
        scs, inters = [], []
        for h in range(H):
            sl = slice(h * dk, (h + 1) * dk)
            parts = [_dot_nt(a['xs'][l][:, sl], a['xs'][l][:, sl]) for l in range(NL)]
            st = st_ref[h]
            inters.append(_dot_nt(a['qb'][:, sl], st.astype(BF16)))
            st_ref[h] = st * a['ex_tot'][:, sl] + _dot(a['vt'][sl, :], a['kr'][:, sl])
            sc = msk_ref[NL] * jnp.sum(a['qk'][:, sl], axis=-1, keepdims=True)
            for l in range(NL):
                sc = sc + msk_ref[l] * parts[l]
            scs.append(sc.astype(BF16))
        return scs, inters

    def outputs(a, scs, inters):
        rows = a['rows']
        if reverse:
            of = of_ref[0, rows]
            hg = hg_ref[0, rows].astype(F32)
            gain = gain_ref[...]
        for h in range(H):
            sl = slice(h * dk, (h + 1) * dk)
            o_h = inters[h] + _dot(scs[h], a['v'][:, sl])
            if reverse:
                tot_o = of[:, sl] + o_h
                ms = jnp.mean(tot_o * tot_o, axis=-1, keepdims=True)
                o_ref[0, rows, sl] = (tot_o * lax.rsqrt(ms + RMS_EPS) * gain
                                      * _silu(hg[:, sl])).astype(o_ref.dtype)
            else:
                o_ref[0, rows, sl] = o_h

    order = list(range(CB - 1, -1, -1) if reverse else range(CB))
    cur = gates_and_factors(order[0])
    for n, j in enumerate(order):
        scs, inters = scores_and_state(cur)
        nxt = gates_and_factors(order[n + 1]) if n + 1 < CB else None
        outputs(cur, scs, inters)
        cur = nxt


def _hgrn_pass(phg, lb_logits, norm_gain, of, layer, reverse):
    B, S, W5 = phg.shape
    W = W5 // 5
    C = min(C_HGRN, S)
    CB = min(CB_HGRN, S // C)
    TB = CB * C
    nc = S // TB
    halves, amat, msk, upv = _hgrn_consts(C, reverse)
    L = lb_logits.shape[0]
    dk = W // HGRN_HEADS

    def cmap(i):
        return nc - 1 - i if reverse else i

    fcol = 2 if reverse else 1
    in_specs = [
        pl.BlockSpec((1, TB, W), lambda b, i: (b, cmap(i), 0)),
        pl.BlockSpec((1, TB, W), lambda b, i: (b, cmap(i), fcol)),
        pl.BlockSpec((1, TB, W), lambda b, i: (b, cmap(i), 3)),
        pl.BlockSpec((L, W), lambda b, i: (0, 0)),
        pl.BlockSpec(amat.shape, lambda b, i: (0, 0)),
        pl.BlockSpec(msk.shape, lambda b, i: (0, 0, 0)),
        pl.BlockSpec(upv.shape, lambda b, i: (0, 0, 0)),
    ]
    args = [phg, phg, phg, lb_logits, jnp.asarray(amat, BF16), jnp.asarray(msk, F32), jnp.asarray(upv, F32)]
    if reverse:
        in_specs += [pl.BlockSpec((1, TB, W), lambda b, i: (b, cmap(i), 0)),
                     pl.BlockSpec((1, TB, W), lambda b, i: (b, cmap(i), 4)),
                     pl.BlockSpec((1, dk), lambda b, i: (0, 0))]
        args += [of, phg, norm_gain.reshape(1, dk)]
    return pl.pallas_call(
        functools.partial(_hgrn_kernel, layer=layer, reverse=reverse, C=C, CB=CB, halves=tuple(halves)),
        grid=(B, nc),
        in_specs=in_specs,
        out_specs=pl.BlockSpec((1, TB, W), lambda b, i: (b, cmap(i), 0)),
        out_shape=jax.ShapeDtypeStruct((B, S, W), BF16 if reverse else F32),
        scratch_shapes=[pltpu.VMEM((HGRN_HEADS, dk, dk), F32)],
        compiler_params=_cparams(("parallel", "arbitrary")),
        name="hgrn_bwd" if reverse else "hgrn_fwd",
    )(*args)


def _merge_kernel(ya_ref, yl_ref, yh_ref, g0_ref, g1_ref, g2_ref, x_ref, mod_ref,
                  wa_ref, wl_ref, wh_ref, wo_ref, lng_ref, lnb_ref, wr_ref,
                  x1_ref, h2_ref, route_ref, *, alpha, n_groups, e_per):
    m = mod_ref[0]
    half = x_ref.shape[1] // 2
    halves = [slice(r0, r0 + half) for r0 in (0, half)]

    branch = [(_dot(ya_ref[0, rs], wa_ref[...]), _dot(yl_ref[0, rs], wl_ref[...]),
               _dot(yh_ref[0, rs], wh_ref[...])) for rs in halves]
    mixes = []
    for rs, (pa, pl_, ph) in zip(halves, branch):
        merged = (_sigmoid(g0_ref[0, rs].astype(F32)) * pa + _sigmoid(g1_ref[0, rs].astype(F32)) * pl_
                  + _sigmoid(g2_ref[0, rs].astype(F32)) * ph)
        mixes.append(_dot(merged.astype(BF16), wo_ref[...]))
    logit_halves = []
    for rs, mix in zip(halves, mixes):
        x1 = _layernorm(alpha * x_ref[0, rs] + (1.0 + m[2:3]) * mix, lng_ref[...], lnb_ref[...])
        x1_ref[0, rs] = x1
        h2 = x1 * (1.0 + m[4:5]) + m[3:4]
        h2_ref[0, rs] = h2
        xh, xl = _split2(h2)
        r = _dot(xh, wr_ref[...])
        logit_halves.append(r[:, :V7X_LANES] + r[:, V7X_LANES:] + _dot(xl, wr_ref[:, :V7X_LANES]))
    for rs, logits in zip(halves, logit_halves):
        route_ref[0, rs] = _route(logits, n_groups, e_per)


def _route(logits, n_groups, e_per):
    lane = lax.broadcasted_iota(jnp.int32, logits.shape, 1)
    big = jnp.int32(V7X_LANES)
    neg = jnp.float32(-jnp.inf)
    gl = jnp.where(lane < n_groups, logits, neg)
    gmax = jnp.max(gl, axis=-1, keepdims=True)
    gsel = jnp.min(jnp.where(gl == gmax, lane, big), axis=-1, keepdims=True)
    gp = 1.0 / jnp.sum(jnp.exp(gl - gmax), axis=-1, keepdims=True)
    lo = n_groups + gsel * e_per
    el = jnp.where((lane >= lo) & (lane < lo + e_per), logits, neg)
    m1 = jnp.max(el, axis=-1, keepdims=True)
    i1 = jnp.min(jnp.where(el == m1, lane, big), axis=-1, keepdims=True)
    el2 = jnp.where(lane == i1, neg, el)
    m2 = jnp.max(el2, axis=-1, keepdims=True)
    i2 = jnp.min(jnp.where(el2 == m2, lane, big), axis=-1, keepdims=True)
    z = jnp.sum(jnp.exp(el - m1), axis=-1, keepdims=True)
    p1 = 1.0 / z
    p2 = jnp.exp(m2 - m1) / z
    w1 = gp * (p1 / (p1 + p2))
    w2 = gp * (p2 / (p1 + p2))
    e1 = (i1 - n_groups).astype(F32)
    e2 = (i2 - n_groups).astype(F32)
    return jnp.where(lane == 0, w1, jnp.where(lane == 1, w2,
                     jnp.where(lane == 2, e1, jnp.where(lane == 3, e2, 0.0))))


def _merge(ya, yl, yh, pmg, x, mod_l, wa, wl, wh, wo, lng, lnb, wr, n_groups, e_per, alpha):
    B, S, D = x.shape
    tm = min(TM_MERGE, S)
    bw = ya.shape[-1]
    tok = lambda b, i: (b, i, 0)
    const = lambda b, i: (0, 0)
    return pl.pallas_call(
        functools.partial(_merge_kernel, alpha=alpha, n_groups=n_groups, e_per=e_per),
        grid=(B, S // tm),
        in_specs=[pl.BlockSpec((1, tm, bw), tok), pl.BlockSpec((1, tm, bw), tok), pl.BlockSpec((1, tm, bw), tok),
                  pl.BlockSpec((1, tm, D), lambda b, i: (b, i, 0)),
                  pl.BlockSpec((1, tm, D), lambda b, i: (b, i, 1)),
                  pl.BlockSpec((1, tm, D), lambda b, i: (b, i, 2)),
                  pl.BlockSpec((1, tm, D), tok),
                  pl.BlockSpec((1, 6, D), lambda b, i: (b, 0, 0)),
                  pl.BlockSpec((bw, D), const), pl.BlockSpec((bw, D), const), pl.BlockSpec((bw, D), const),
                  pl.BlockSpec((D, D), const),
                  pl.BlockSpec((1, D), const), pl.BlockSpec((1, D), const),
                  pl.BlockSpec((D, 2 * V7X_LANES), const)],
        out_specs=[pl.BlockSpec((1, tm, D), tok), pl.BlockSpec((1, tm, D), tok),
                   pl.BlockSpec((1, tm, V7X_LANES), tok)],
        out_shape=[jax.ShapeDtypeStruct((B, S, D), F32), jax.ShapeDtypeStruct((B, S, D), F32),
                   jax.ShapeDtypeStruct((B, S, V7X_LANES), F32)],
        compiler_params=_cparams(("parallel", "parallel")),
        name="merge",
    )(ya, yl, yh, pmg, pmg, pmg, x, mod_l, wa, wl, wh, wo, lng, lnb, wr)


def _dispatch_kernel(dest_ref, h_ref, xs_in, xs_hbm, sem, *, TM):
    del xs_in

    def start(r, c):
        for a in range(TOP_K):
            pltpu.make_async_copy(h_ref.at[pl.ds(r, 1)],
                                  xs_hbm.at[pl.ds(dest_ref[0, 0, TOP_K * r + a], 1)], sem).start()
        return c
    lax.fori_loop(0, TM, start, 0, unroll=4)
    for a in range(TOP_K):
        pltpu.make_async_copy(h_ref, xs_hbm.at[pl.ds(0, TM)], sem).wait()


def _dispatch(h2, dest, xs_prev, S):
    N, D = h2.shape
    TM = min(TM_DISPATCH, S)
    nt = N // TM
    return pl.pallas_call(
        functools.partial(_dispatch_kernel, TM=TM),
        grid=(nt,),
        in_specs=[pl.BlockSpec((1, 1, TOP_K * TM), lambda i: (i, 0, 0), memory_space=pltpu.SMEM),
                  pl.BlockSpec((TM, D), lambda i: (i, 0)),
                  pl.BlockSpec(memory_space=pl.ANY)],
        out_specs=pl.BlockSpec(memory_space=pl.ANY),
        out_shape=jax.ShapeDtypeStruct(xs_prev.shape, xs_prev.dtype),
        scratch_shapes=[pltpu.SemaphoreType.DMA(())],
        input_output_aliases={2: 0},
        compiler_params=_cparams(("arbitrary",)),
        name="moe_dispatch",
    )(dest.reshape(nt, 1, TOP_K * TM), h2, xs_prev)


def _expert_kernel(blk_e_ref, nused_ref, xs_ref, w13_ref, w2_ref, ys_ref, *, F):
    i = pl.program_id(0)

    @pl.when(i < nused_ref[0])
    def _():
        half = xs_ref.shape[0] // 2
        abs_ = [_dot(xs_ref[r0:r0 + half, :].astype(BF16), w13_ref[0]) for r0 in (0, half)]
        for n, r0 in enumerate((0, half)):
            hmid = (_silu(abs_[n][:, :F]) * abs_[n][:, F:]).astype(BF16)
            ys_ref[r0:r0 + half, :] = _dot(hmid, w2_ref[0])

    @pl.when(i >= nused_ref[0])
    def _():
        ys_ref[...] = jnp.zeros_like(ys_ref)


def _experts(xs, blk_e, nused, w13, w2):
    P, D = xs.shape
    R = R_MOE
    nb = P // R
    F = w2.shape[1]
    grid_spec = pltpu.PrefetchScalarGridSpec(
        num_scalar_prefetch=2,
        grid=(nb,),
        in_specs=[pl.BlockSpec((R, D), lambda i, be, nu: (i, 0)),
                  pl.BlockSpec((1, D, 2 * F), lambda i, be, nu: (be[i], 0, 0)),
                  pl.BlockSpec((1, F, D), lambda i, be, nu: (be[i], 0, 0))],
        out_specs=pl.BlockSpec((R, D), lambda i, be, nu: (i, 0)),
    )
    return pl.pallas_call(
        functools.partial(_expert_kernel, F=F),
        grid_spec=grid_spec,
        out_shape=jax.ShapeDtypeStruct((P, D), F32),
        compiler_params=_cparams(("arbitrary",)),
        name="moe_experts",
    )(blk_e, nused, xs, w13, w2)


def _combine_kernel(pos_ref, posn_ref, ys_hbm, route_ref, x_ref, mod_ref, lng_ref, lnb_ref, o_ref,
                    ybuf, sems, *, TC, alpha):
    i = pl.program_id(0)
    n = pl.num_programs(0)
    slot = i % 2

    def issue(p_ref, s):
        def body(r, c):
            for a in range(TOP_K):
                pltpu.make_async_copy(ys_hbm.at[pl.ds(p_ref[0, 0, TOP_K * r + a], 1)],
                                      ybuf.at[s, a, pl.ds(r, 1)], sems.at[s]).start()
            return c
        lax.fori_loop(0, TC, body, 0, unroll=4)

    @pl.when(i == 0)
    def _():
        issue(pos_ref, 0)

    @pl.when(i + 1 < n)
    def _():
        issue(posn_ref, 1 - slot)

    for a in range(TOP_K):
        pltpu.make_async_copy(ys_hbm.at[pl.ds(0, TC)], ybuf.at[slot, a], sems.at[slot]).wait()

    m = mod_ref[0]
    w = route_ref[...]
    ffn = w[:, 0:1] * ybuf[slot, 0]
    for a in range(1, TOP_K):
        ffn = ffn + w[:, a:a + 1] * ybuf[slot, a]
    o_ref[...] = _layernorm(alpha * x_ref[...] + (1.0 + m[5:6]) * ffn, lng_ref[...], lnb_ref[...])


def _combine(ys, dest, route, x1, mod_l, lng, lnb, S, alpha):
    N, D = x1.shape
    TC = min(TC_MOE, S)
    nt = N // TC
    per_b = S // TC
    pos = dest.reshape(nt, 1, TOP_K * TC)
    return pl.pallas_call(
        functools.partial(_combine_kernel, TC=TC, alpha=alpha),
        grid=(nt,),
        in_specs=[pl.BlockSpec((1, 1, TOP_K * TC), lambda i: (i, 0, 0), memory_space=pltpu.SMEM),
                  pl.BlockSpec((1, 1, TOP_K * TC), lambda i: (jnp.minimum(i + 1, nt - 1), 0, 0),
                               memory_space=pltpu.SMEM),
                  pl.BlockSpec(memory_space=pl.ANY),
                  pl.BlockSpec((TC, V7X_LANES), lambda i: (i, 0)),
                  pl.BlockSpec((TC, D), lambda i: (i, 0)),
                  pl.BlockSpec((1, 6, D), lambda i: (i // per_b, 0, 0)),
                  pl.BlockSpec((1, D), lambda i: (0, 0)),
                  pl.BlockSpec((1, D), lambda i: (0, 0))],
        out_specs=pl.BlockSpec((TC, D), lambda i: (i, 0)),
        out_shape=jax.ShapeDtypeStruct((N, D), F32),
        scratch_shapes=[pltpu.VMEM((2, TOP_K, TC, D), F32), pltpu.SemaphoreType.DMA((2,))],
        compiler_params=_cparams(("arbitrary",)),
        name="moe_combine",
    )(pos, pos, ys, route, x1, mod_l, lng, lnb)


def _dispatch_plan(route, n_experts, R):
    eid = route[:, TOP_K:2 * TOP_K].astype(jnp.int32)
    hot = (eid[:, :, None] == jnp.arange(n_experts, dtype=jnp.int32)).astype(jnp.int32).sum(1)
    csum = jnp.cumsum(hot, axis=0)
    counts = csum[-1]
    rank = jnp.take_along_axis(csum - hot, eid, axis=1)
    pcounts = (counts + R - 1) // R * R
    pends = jnp.cumsum(pcounts)
    pstarts = pends - pcounts
    dest = (pstarts[eid] + rank).astype(jnp.int32)
    nb = (eid.shape[0] * TOP_K + n_experts * R) // R
    starts = jnp.arange(nb, dtype=jnp.int32) * R
    blk_e = jnp.minimum((pends[None, :] <= starts[:, None]).astype(jnp.int32).sum(1), n_experts - 1)
    nused = (pends[-1:] // R).astype(jnp.int32)
    return blk_e, nused, dest


def _prep_weights(p):
    L = p['w_in'].shape[0]
    W = p['lru_conv_w'].shape[-1]
    nblk, bs = p['lru_wa'].shape[2], p['lru_wa'].shape[3]

    def blockdiag(w):
        eye = jnp.eye(nblk, dtype=w.dtype)
        return jnp.einsum('ldnij,nm->ldnimj', w, eye).reshape(L, 2, W, W)

    G, E = p['router_expert_w'].shape[1], p['router_expert_w'].shape[3]
    D = p['w_in'].shape[1]
    wr = jnp.concatenate([p['router_group_w'],
                          p['router_expert_w'].transpose(0, 2, 1, 3).reshape(L, D, G * E)], -1)
    wr = jnp.pad(wr, ((0, 0), (0, 0), (0, V7X_LANES - wr.shape[-1])))
    wr_hi = wr.astype(BF16)
    wr = jnp.concatenate([wr_hi, (wr - wr_hi.astype(F32)).astype(BF16)], -1)
    return dict(
        w_in=p['w_in'].astype(BF16),
        gq=jnp.tile(p['attn_q_gain'], (1, ATTN_HEADS))[:, None, :],
        gk=jnp.tile(p['attn_k_gain'], (1, KV_HEADS))[:, None, :],
        w_attn_o=p['w_attn_o'].astype(BF16),
        conv_w=p['lru_conv_w'], conv_b=p['lru_conv_b'][:, None, :],
        wbd=jnp.concatenate([blockdiag(p['lru_wa']), blockdiag(p['lru_wi'])], -1).astype(BF16),
        gb=jnp.concatenate([p['lru_ba'], p['lru_bi']], -1)[:, :, None, :],
        lam=p['lru_lambda'][:, :, None, :],
        w_lru_o=p['w_lru_o'].astype(BF16),
        w_hgrn_o=p['w_hgrn_o'].astype(BF16),
        w_out=p['w_out'].astype(BF16),
        ln1_g=p['ln1_g'][:, None, :], ln1_b=p['ln1_b'][:, None, :],
        ln2_g=p['ln2_g'][:, None, :], ln2_b=p['ln2_b'][:, None, :],
        wr=wr, n_groups=G, e_per=E,
        w13=p['expert_w13'].astype(BF16), w2=p['expert_w2'].astype(BF16),
    )


def _trunk(x, c, p, wp):
    B, S, D = x.shape
    L = p['w_in'].shape[0]
    alpha = (2 * L) ** 0.25
    QW = ATTN_HEADS * HEAD_DIM
    KW = KV_HEADS * HEAD_DIM
    LW = p['lru_conv_w'].shape[-1]
    HW = p['hgrn_lb_logits'].shape[-1]
    widths = (QW, 2 * KW, 2 * LW, 5 * HW, N_BRANCH * D)
    assert sum(widths) == p['w_in'].shape[-1]
    n_experts = wp['n_groups'] * wp['e_per']

    mod = _ada_mod(c, p['ada_w'], p['ada_b'])
    cos_t, sin_t = _rope_tables(S)
    bind, psw = _head_consts()
    cos_t, sin_t = jnp.asarray(cos_t), jnp.asarray(sin_t)
    bind, psw = jnp.asarray(bind, BF16), jnp.asarray(psw, BF16)

    xs = None
    for l in range(L):
        pq, pkv, plru, phg, pmg = _inproj(x, mod[l], wp['w_in'][l], widths)
        qt, k, vt = _attn_prep(pq, pkv, cos_t, sin_t, wp['gq'][l], wp['gk'][l], bind, psw)
        ya = _attention(qt, k, vt)
        lru_args = (plru, wp['conv_w'][l], wp['conv_b'][l], wp['wbd'][l], wp['gb'][l], wp['lam'][l])
        hf = _lru_pass(*lru_args, None, reverse=False)
        yl = _lru_pass(*lru_args, hf, reverse=True)
        of = _hgrn_pass(phg, p['hgrn_lb_logits'], p['hgrn_norm_gain'][l], None, l, reverse=False)
        yh = _hgrn_pass(phg, p['hgrn_lb_logits'], p['hgrn_norm_gain'][l], of, l, reverse=True)
        x1, h2, route = _merge(ya, yl, yh, pmg, x, mod[l], wp['w_attn_o'][l], wp['w_lru_o'][l],
                               wp['w_hgrn_o'][l], wp['w_out'][l], wp['ln1_g'][l], wp['ln1_b'][l],
                               wp['wr'][l], wp['n_groups'], wp['e_per'], alpha)
        N = B * S
        route = route.reshape(N, V7X_LANES)
        blk_e, nused, dest = _dispatch_plan(route, n_experts, R_MOE)
        if xs is None:
            xs = jnp.zeros((N * TOP_K + n_experts * R_MOE, D), F32)
        xs = _dispatch(h2.reshape(N, D), dest, xs, S)
        ys = _experts(xs, blk_e, nused, wp['w13'][l], wp['w2'][l])
        x = _combine(ys, dest, route, x1.reshape(N, D), mod[l], wp['ln2_g'][l], wp['ln2_b'][l], S,
                     alpha).reshape(B, S, D)
    return x


def kernel(x_prompt, x_sample, c_prompt, c_sample, ada_w, ada_b, w_in, attn_q_gain, attn_k_gain, w_attn_o, lru_conv_w, lru_conv_b, lru_wa, lru_ba, lru_wi, lru_bi, lru_lambda, w_lru_o, hgrn_lb_logits, hgrn_norm_gain, w_hgrn_o, w_out, ln1_g, ln1_b, router_group_w, router_expert_w, expert_w13, expert_w2, ln2_g, ln2_b):
    p = dict(ada_w=ada_w, ada_b=ada_b, w_in=w_in, attn_q_gain=attn_q_gain, attn_k_gain=attn_k_gain,
             w_attn_o=w_attn_o, lru_conv_w=lru_conv_w, lru_conv_b=lru_conv_b, lru_wa=lru_wa, lru_ba=lru_ba,
             lru_wi=lru_wi, lru_bi=lru_bi, lru_lambda=lru_lambda, w_lru_o=w_lru_o,
             hgrn_lb_logits=hgrn_lb_logits, hgrn_norm_gain=hgrn_norm_gain, w_hgrn_o=w_hgrn_o, w_out=w_out,
             ln1_g=ln1_g, ln1_b=ln1_b, router_group_w=router_group_w, router_expert_w=router_expert_w,
             expert_w13=expert_w13, expert_w2=expert_w2, ln2_g=ln2_g, ln2_b=ln2_b)
    wp = _prep_weights(p)
    return (_trunk(x_prompt, c_prompt, p, wp), _trunk(x_sample, c_sample, p, wp))
```

```python
import functools
import math

import numpy as np
import jax
import jax.numpy as jnp
from jax import lax
from jax.experimental import pallas as pl
from jax.experimental.pallas import tpu as pltpu

F32 = jnp.float32
BF16 = jnp.bfloat16

GRID_W = 64
CONV_W = 4
CONV_LEFT = 2
LRU_C = 8.0
ATTN_HEADS = 8
KV_HEADS = 2
HEAD_DIM = 64
GQA_GROUP = ATTN_HEADS // KV_HEADS
AXIAL_DIM = HEAD_DIM // 2
AXIAL_FREQS = AXIAL_DIM // 2
ROPE_THETA = 10000.0
HGRN_HEADS = 4
N_BRANCH = 3
TOP_K = 2
LN_EPS = 1e-5
RMS_EPS = 1e-6
LOG2_E = 1.4426950408889634

V7X_LANES = 128
V7X_SUBLANES = 8
V7X_VMEM_LIMIT_BYTES = 56 * 1024 * 1024

TM_INPROJ = 512
V_AUG_ROWS = HEAD_DIM + 16
TQ_ATTN = 512
T_LRU = 256
C_HGRN = 128
CB_HGRN = 4
TK_ATTN = 512
TM_MERGE = 512
R_MOE = 256
TC_MOE = 512


def _cparams(sem):
    return pltpu.CompilerParams(dimension_semantics=sem, vmem_limit_bytes=V7X_VMEM_LIMIT_BYTES)


def _dot(a, b):
    return jnp.dot(a, b, preferred_element_type=F32)


def _dot_nt(a, b):
    return lax.dot_general(a, b, (((1,), (1,)), ((), ())), preferred_element_type=F32)


def _dot_hi(a, b):
    return jnp.dot(a, b, preferred_element_type=F32, precision=lax.Precision.HIGHEST)


def _split2(x):
    hi = x.astype(BF16)
    lo = (x - hi.astype(F32)).astype(BF16)
    return hi, lo


def _dot_x_sel(x, m):
    hi, lo = _split2(x)
    return _dot(hi, m) + _dot(lo, m)


def _dot_sel_x(m, x):
    hi, lo = _split2(x)
    return _dot(m, hi) + _dot(m, lo)


def _sigmoid(x):
    return 1.0 / (1.0 + jnp.exp(-x))


def _silu(x):
    return x * _sigmoid(x)


def _layernorm(y, g, b):
    mu = jnp.mean(y, axis=-1, keepdims=True)
    d = y - mu
    var = jnp.mean(d * d, axis=-1, keepdims=True)
    return d * lax.rsqrt(var + LN_EPS) * g + b


def _ada_kernel(c_ref, w_ref, b_ref, o_ref):
    c = c_ref[...]
    o_ref[0] = _dot_hi(_silu(c), w_ref[0]) + b_ref[0]


def _ada_mod(c, ada_w, ada_b):
    L, D, N6 = ada_w.shape
    B = c.shape[0]
    tn = N6 // 6
    out = pl.pallas_call(
        _ada_kernel,
        grid=(L, N6 // tn),
        in_specs=[pl.BlockSpec((B, D), lambda l, j: (0, 0)),
                  pl.BlockSpec((1, D, tn), lambda l, j: (l, 0, j)),
                  pl.BlockSpec((1, 1, tn), lambda l, j: (l, 0, j))],
        out_specs=pl.BlockSpec((1, B, tn), lambda l, j: (l, 0, j)),
        out_shape=jax.ShapeDtypeStruct((L, B, N6), F32),
        compiler_params=_cparams(("arbitrary", "arbitrary")),
        name="ada_mod",
    )(c, ada_w, ada_b.reshape(L, 1, N6))
    return out.reshape(L, B, 6, N6 // 6)


def _inproj_kernel(x_ref, mod_ref, w_ref, *o_refs, chunk):
    m = mod_ref[0]
    h = (x_ref[0] * (1.0 + m[1:2]) + m[0:1]).astype(BF16)
    off = 0
    for o_ref in o_refs:
        n = o_ref.shape[-1]
        for c0 in range(0, n, chunk):
            c1 = min(c0 + chunk, n)
            o_ref[0, :, c0:c1] = _dot(h, w_ref[:, off + c0:off + c1]).astype(o_ref.dtype)
        off += n


def _inproj(x, mod_l, w_in_bf, widths):
    B, S, D = x.shape
    N = w_in_bf.shape[1]
    tm = min(TM_INPROJ, S)
    out_shape = [jax.ShapeDtypeStruct((B, S, n), BF16) for n in widths]
    out_specs = [pl.BlockSpec((1, tm, n), lambda b, i: (b, i, 0)) for n in widths]
    return pl.pallas_call(
        functools.partial(_inproj_kernel, chunk=512),
        grid=(B, S // tm),
        in_specs=[pl.BlockSpec((1, tm, D), lambda b, i: (b, i, 0)),
                  pl.BlockSpec((1, 6, D), lambda b, i: (b, 0, 0)),
                  pl.BlockSpec((D, N), lambda b, i: (0, 0), pipeline_mode=pl.Buffered(1))],
        out_specs=out_specs,
        out_shape=out_shape,
        compiler_params=_cparams(("parallel", "parallel")),
        name="in_proj",
    )(x, mod_l, w_in_bf)


def _rope_tables(S):
    t = np.arange(S)
    row = (t // GRID_W).astype(np.float64)
    col = (t % GRID_W).astype(np.float64)
    inv = ROPE_THETA ** (-np.arange(AXIAL_FREQS, dtype=np.float64) / AXIAL_FREQS)
    ang_r = (row[:, None].astype(np.float32) * inv.astype(np.float32)[None, :]).astype(np.float32)
    ang_c = (col[:, None].astype(np.float32) * inv.astype(np.float32)[None, :]).astype(np.float32)
    cos_h = np.concatenate([np.cos(ang_r), np.cos(ang_r), np.cos(ang_c), np.cos(ang_c)], -1)
    sin_h = np.concatenate([np.sin(ang_r), np.sin(ang_r), np.sin(ang_c), np.sin(ang_c)], -1)
    cos_t = np.tile(cos_h, (1, ATTN_HEADS)).astype(np.float32)
    sin_t = np.tile(sin_h, (1, ATTN_HEADS)).astype(np.float32)
    return cos_t, sin_t


def _head_consts():
    W = ATTN_HEADS * HEAD_DIM
    d = np.arange(W)
    bind = (d[:, None] // HEAD_DIM == d[None, :] // HEAD_DIM).astype(np.float32) / HEAD_DIM
    psw = np.zeros((W, W), np.float32)
    lowhalf = (d % AXIAL_DIM) < AXIAL_FREQS
    for j in d:
        if lowhalf[j]:
            psw[j + AXIAL_FREQS, j] = -1.0
        else:
            psw[j - AXIAL_FREQS, j] = 1.0
    return bind, psw


def _prep_kernel(pq_ref, pkv_ref, cos_ref, sin_ref, gq_ref, gk_ref, bind_ref, psw_ref,
                 qt_ref, k_ref, vt_ref):
    KW = KV_HEADS * HEAD_DIM
    cos = cos_ref[...]
    sin = sin_ref[...]

    def norm_rope(u, gain, w):
        ms = _dot_x_sel(u * u, bind_ref[:w, :w])
        un = u * lax.rsqrt(ms + RMS_EPS) * gain
        return un * cos[:, :w] + _dot_x_sel(un, psw_ref[:w, :w]) * sin[:, :w]

    q = norm_rope(pq_ref[0].astype(F32), gq_ref[...], ATTN_HEADS * HEAD_DIM)
    qt = (q * (HEAD_DIM ** -0.5 * LOG2_E)).T
    zeros = jnp.zeros((HEAD_DIM, qt.shape[1]), F32)
    for h in range(ATTN_HEADS):
        rows = [zeros] * KV_HEADS
        rows[h // GQA_GROUP] = qt[h * HEAD_DIM:(h + 1) * HEAD_DIM]
        qt_ref[0, h] = jnp.concatenate(rows, axis=0).astype(BF16)
    kv = pkv_ref[0]
    k_ref[0, 0] = norm_rope(kv[:, :KW].astype(F32), gk_ref[...], KW).astype(BF16)
    vt = kv[:, KW:].astype(F32).T
    tm = vt.shape[1]
    extra = (lax.broadcasted_iota(jnp.int32, (V_AUG_ROWS - HEAD_DIM, tm), 0) == 0).astype(F32)
    vt_ref[0, 0] = jnp.concatenate(
        [blk for g in range(KV_HEADS) for blk in (vt[g * HEAD_DIM:(g + 1) * HEAD_DIM], extra)],
        axis=0).astype(BF16)


def _attn_prep(pq, pkv, cos_t, sin_t, gq, gk, bind, psw):
    B, S, QW = pq.shape
    KW = KV_HEADS * HEAD_DIM
    tm = min(TK_ATTN, S)
    return pl.pallas_call(
        _prep_kernel,
        grid=(B, S // tm),
        in_specs=[pl.BlockSpec((1, tm, QW), lambda b, i: (b, i, 0)),
                  pl.BlockSpec((1, tm, 2 * KW), lambda b, i: (b, i, 0)),
                  pl.BlockSpec((tm, QW), lambda b, i: (i, 0)),
                  pl.BlockSpec((tm, QW), lambda b, i: (i, 0)),
                  pl.BlockSpec((1, QW), lambda b, i: (0, 0)),
                  pl.BlockSpec((1, KW), lambda b, i: (0, 0)),
                  pl.BlockSpec((QW, QW), lambda b, i: (0, 0)),
                  pl.BlockSpec((QW, QW), lambda b, i: (0, 0))],
        out_specs=[pl.BlockSpec((1, ATTN_HEADS, KW, tm), lambda b, i: (b, 0, 0, i)),
                   pl.BlockSpec((1, 1, tm, KW), lambda b, i: (b, i, 0, 0)),
                   pl.BlockSpec((1, 1, KV_HEADS * V_AUG_ROWS, tm), lambda b, i: (b, i, 0, 0))],
        out_shape=[jax.ShapeDtypeStruct((B, ATTN_HEADS, KW, S), BF16),
                   jax.ShapeDtypeStruct((B, S // tm, tm, KW), BF16),
                   jax.ShapeDtypeStruct((B, S // tm, KV_HEADS * V_AUG_ROWS, tm), BF16)],
        compiler_params=_cparams(("parallel", "parallel")),
        name="attn_prep",
    )(pq, pkv, cos_t, sin_t, gq, gk, bind, psw)


def _attn_kernel(qt_ref, k_ref, vt_ref, o_ref, st_scr, m_scr, acc_scr, *, nk):
    def scores_to(slot, c):
        kc = k_ref[0, c]
        for h in range(GQA_GROUP):
            st_scr[slot, h] = _dot(kc, qt_ref[0, h])

    def consume(slot, c):
        vc = vt_ref[0, c]
        for h in range(GQA_GROUP):
            st = st_scr[slot, h]
            m_old = m_scr[h]
            m_new = jnp.maximum(m_old, jnp.max(st, axis=0, keepdims=True))
            p = jnp.exp2(st - m_new).astype(BF16)
            acc_scr[h] = jnp.exp2(m_old - m_new) * acc_scr[h] + _dot(vc, p)
            m_scr[h] = m_new

    m_scr[...] = jnp.full(m_scr.shape, -jnp.inf, F32)
    acc_scr[...] = jnp.zeros(acc_scr.shape, F32)
    scores_to(0, 0)

    def step(nslot, cn, cslot, cc):
        kc = k_ref[0, cn]
        vc = vt_ref[0, cc]
        for h in range(GQA_GROUP):
            st_scr[nslot, h] = _dot(kc, qt_ref[0, h])
            st = st_scr[cslot, h]
            m_old = m_scr[h]
            m_new = jnp.maximum(m_old, jnp.max(st, axis=0, keepdims=True))
            p = jnp.exp2(st - m_new).astype(BF16)
            acc_scr[h] = jnp.exp2(m_old - m_new) * acc_scr[h] + _dot(vc, p)
            m_scr[h] = m_new

    def body(j, carry):
        c0 = 2 * j
        step(1, c0 + 1, 0, c0)
        step(0, c0 + 2, 1, c0 + 1)
        return carry
    lax.fori_loop(0, nk // 2 - 1, body, 0)
    step(1, nk - 1, 0, nk - 2)
    consume(1, nk - 1)

    outs = []
    for h in range(GQA_GROUP):
        acc = acc_scr[h]
        outs.append((acc[:HEAD_DIM] / acc[HEAD_DIM:HEAD_DIM + 1]).T)
    o_ref[0] = jnp.concatenate(outs, axis=-1).astype(o_ref.dtype)


def _attention(qt, k, vt):
    B, nk, tk, KW = k.shape
    S = nk * tk
    GW = GQA_GROUP * HEAD_DIM
    tq = min(TQ_ATTN, S)
    assert nk % 2 == 0
    scratch = [pltpu.VMEM((2, GQA_GROUP, tk, tq), F32), pltpu.VMEM((GQA_GROUP, 1, tq), F32),
               pltpu.VMEM((GQA_GROUP, V_AUG_ROWS, tq), F32)]
    return pl.pallas_call(
        functools.partial(_attn_kernel, nk=nk),
        scratch_shapes=scratch,
        grid=(B, KV_HEADS, S // tq),
        in_specs=[pl.BlockSpec((1, GQA_GROUP, KW, tq), lambda b, g, i: (b, g, 0, i)),
                  pl.BlockSpec((1, nk, tk, KW), lambda b, g, i: (b, 0, 0, 0)),
                  pl.BlockSpec((1, nk, V_AUG_ROWS, tk), lambda b, g, i: (b, 0, g, 0))],
        out_specs=pl.BlockSpec((1, tq, GW), lambda b, g, i: (b, i, g)),
        out_shape=jax.ShapeDtypeStruct((B, S, ATTN_HEADS * HEAD_DIM), BF16),
        compiler_params=_cparams(("parallel", "parallel", "parallel")),
        name="attention",
    )(qt, k, vt)


def _log1p(y):
    u = 1.0 + y
    return jnp.where(u == 1.0, y, jnp.log(u) * (y / (u - 1.0)))


def _softplus(z):
    return jnp.maximum(z, 0.0) + _log1p(jnp.exp(-jnp.abs(z)))


def _neg_expm1_2x(z, ez):
    return -jnp.tanh(z) * (ez * ez + 1.0)


def _gelu_tanh(x):
    return 0.5 * x * (1.0 + jnp.tanh(math.sqrt(2.0 / math.pi) * (x + 0.044715 * (x * x * x))))


def _lru_kernel(*refs, reverse, T):
    if reverse:
        (x_ref, xp_ref, xn_ref, cw_ref, cb_ref, wbd_ref, gb_ref, lam_ref, hf_ref, lg_ref,
         o_ref, carry_ref) = refs
    else:
        (x_ref, xp_ref, xn_ref, cw_ref, cb_ref, wbd_ref, gb_ref, lam_ref,
         o_ref, carry_ref) = refs
    i = pl.program_id(1)
    nt = pl.num_programs(1)
    tile = nt - 1 - i if reverse else i
    W = x_ref.shape[-1]

    @pl.when(i == 0)
    def _():
        carry_ref[...] = jnp.zeros_like(carry_ref)

    x = x_ref[0].astype(F32)
    prev = jnp.where(tile == 0, 0.0, xp_ref[0].astype(F32))
    nxt = jnp.where(tile == nt - 1, 0.0, xn_ref[0].astype(F32))
    row8 = lax.broadcasted_iota(jnp.int32, (V7X_SUBLANES, W), 0)

    def shifted(k):
        if k == 0:
            return x
        r = pltpu.roll(x, (-k) % T, axis=0)
        if k < 0:
            fill = pltpu.roll(prev, (-k) % V7X_SUBLANES, axis=0)
            head = jnp.where(row8 < -k, fill, r[:V7X_SUBLANES])
            return jnp.concatenate([head, r[V7X_SUBLANES:]], axis=0)
        fill = pltpu.roll(nxt, (-k) % V7X_SUBLANES, axis=0)
        tail = jnp.where(row8 >= V7X_SUBLANES - k, fill, r[T - V7X_SUBLANES:])
        return jnp.concatenate([r[:T - V7X_SUBLANES], tail], axis=0)

    cw = cw_ref[...]
    xc = cb_ref[...] + sum(shifted(j - CONV_LEFT) * cw[j:j + 1] for j in range(CONV_W))

    gates = _dot(xc.astype(BF16), wbd_ref[0]) + gb_ref[0]
    r = _sigmoid(gates[:, :W])
    ig = _sigmoid(gates[:, W:])
    log_a = (-LRU_C) * r * _softplus(-lam_ref[0])
    a = jnp.exp(log_a)
    u = jnp.sqrt(_neg_expm1_2x(log_a, a)) * (ig * xc)

    row = lax.broadcasted_iota(jnp.int32, (T, W), 0)

    def neighbour(z, d, fill):
        if d % V7X_SUBLANES == 0:
            pad = jnp.full((d, W), fill, F32)
            return jnp.concatenate([z[d:], pad] if reverse else [pad, z[:T - d]], axis=0)
        if reverse:
            return jnp.where(row < T - d, pltpu.roll(z, T - d, axis=0), fill)
        return jnp.where(row >= d, pltpu.roll(z, d, axis=0), fill)

    A, U = a, u
    d = 1
    while d < T:
        U = A * neighbour(U, d, 0.0) + U
        A = A * neighbour(A, d, 1.0)
        d *= 2
    h = A * carry_ref[...] + U
    if reverse:
        carry_ref[...] = h[0:1]
        o_ref[0] = ((hf_ref[0] + h) * _gelu_tanh(lg_ref[0].astype(F32))).astype(o_ref.dtype)
    else:
        carry_ref[...] = h[T - 1:T]
        o_ref[0] = h


def _lru_pass(plru, conv_w, conv_b, wbd, gb, lam, hf, reverse):
    B, S, W2 = plru.shape
    W = W2 // 2
    T = min(T_LRU, S)
    nt = S // T
    r8 = T // V7X_SUBLANES
    nb8 = S // V7X_SUBLANES
    d = 1 if reverse else 0

    def tmap(i):
        return nt - 1 - i if reverse else i

    in_specs = [
        pl.BlockSpec((1, T, W), lambda b, i: (b, tmap(i), 0)),
        pl.BlockSpec((1, V7X_SUBLANES, W), lambda b, i: (b, jnp.maximum(tmap(i) * r8 - 1, 0), 0)),
        pl.BlockSpec((1, V7X_SUBLANES, W), lambda b, i: (b, jnp.minimum((tmap(i) + 1) * r8, nb8 - 1), 0)),
        pl.BlockSpec((CONV_W, W), lambda b, i: (0, 0)),
        pl.BlockSpec((1, W), lambda b, i: (0, 0)),
        pl.BlockSpec((1, W, 2 * W), lambda b, i: (d, 0, 0)),
        pl.BlockSpec((1, 1, 2 * W), lambda b, i: (d, 0, 0)),
        pl.BlockSpec((1, 1, W), lambda b, i: (d, 0, 0)),
    ]
    args = [plru, plru, plru, conv_w, conv_b, wbd, gb, lam]
    if reverse:
        in_specs += [pl.BlockSpec((1, T, W), lambda b, i: (b, tmap(i), 0)),
                     pl.BlockSpec((1, T, W), lambda b, i: (b, tmap(i), 1))]
        args += [hf, plru]
    return pl.pallas_call(
        functools.partial(_lru_kernel, reverse=reverse, T=T),
        grid=(B, nt),
        in_specs=in_specs,
        out_specs=pl.BlockSpec((1, T, W), lambda b, i: (b, tmap(i), 0)),
        out_shape=jax.ShapeDtypeStruct((B, S, W), BF16 if reverse else F32),
        scratch_shapes=[pltpu.VMEM((1, W), F32)],
        compiler_params=_cparams(("parallel", "arbitrary")),
        name="lru_bwd" if reverse else "lru_fwd",
    )(*args)


def _hgrn_consts(C, reverse):
    nl = int(round(math.log2(C)))
    t = np.arange(C)
    halves = [C >> (lvl + 1) for lvl in range(nl)]
    a_rows, masks, upper = [], [], []
    for m in halves:
        blk = t // (2 * m)
        up = (t % (2 * m)) >= m
        mid = blk * 2 * m + m
        r = t[None, :]
        masks.append((blk[:, None] == blk[None, :]) & up[:, None] & (~up)[None, :])
        if m < V7X_SUBLANES:
            a_up = (r >= mid[:, None]) & (r <= t[:, None])
            a_lo = (r > t[:, None]) & (r <= mid[:, None] - 1)
            a_rows.append(np.where(up[:, None], a_up, a_lo))
            upper.append(up)
    a_rows.append(t[None, :] <= t[:, None])
    a_rows.append(np.ones((V7X_SUBLANES, C), bool))
    masks.append(np.eye(C, dtype=bool))
    if reverse:
        a_rows = [a[::-1, ::-1] if a.shape[0] == C else a for a in a_rows]
        masks = [mm[::-1, ::-1] for mm in masks]
        upper = [u[::-1] for u in upper]
    amat = np.concatenate(a_rows, 0).astype(np.float32)
    msk = np.stack(masks, 0).astype(np.float32)
    upv = np.stack(upper, 0).astype(np.float32)[:, :, None]
    return halves, amat, msk, upv


def _hgrn_kernel(*refs, layer, reverse, C, CB, halves):
    if reverse:
        (hq_ref, hf_ref, hi_ref, lbl_ref, amat_ref, msk_ref, up_ref, of_ref, hg_ref, gain_ref,
         o_ref, st_ref) = refs
    else:
        (hq_ref, hf_ref, hi_ref, lbl_ref, amat_ref, msk_ref, up_ref, o_ref, st_ref) = refs
    H = HGRN_HEADS
    W = hq_ref.shape[-1]
    dk = W // H
    NL = len(halves)
    n_small = sum(1 for m in halves if m < V7X_SUBLANES)

    @pl.when(pl.program_id(1) == 0)
    def _():
        st_ref[...] = jnp.zeros_like(st_ref)

    lgt = lbl_ref[...]
    e = jnp.exp(lgt - jnp.max(lgt, axis=0, keepdims=True))
    p = e / jnp.sum(e, axis=0, keepdims=True)
    lb = jnp.zeros((1, W), F32)
    for j in range(1, layer + 1):
        lb = lb + p[j:j + 1]

    def gates_and_factors(j):
        rows = slice(j * C, (j + 1) * C)
        q = _silu(hq_ref[0, rows].astype(F32))
        f = lb + (1.0 - lb) * _sigmoid(hf_ref[0, rows].astype(F32))
        k = 1.0 - f
        g = jnp.log(f)
        v = hi_ref[0, rows].astype(F32)

        sums = _dot_sel_x(amat_ref[...], g)
        b = sums[n_small * C:(n_small + 1) * C]
        tot = sums[(n_small + 1) * C:(n_small + 1) * C + 1]
        xs = []
        small = 0
        for m in halves:
            if m >= V7X_SUBLANES:
                shp = (C // (2 * m), 2 * m, W)
                b3 = b.reshape(shp)
                r0 = m if reverse else m - 1
                ref = b3[:, r0:r0 + 1, :]
                rowi = lax.broadcasted_iota(jnp.int32, shp, 1)
                qside = (rowi < m) if reverse else (rowi >= m)
                x3 = jnp.where(qside, q.reshape(shp), k.reshape(shp)) * jnp.exp(
                    jnp.where(qside, b3 - ref, ref - b3))
                xs.append(x3.reshape(C, W).astype(BF16))
            else:
                xs.append((jnp.where(up_ref[small] > 0.5, q, k)
                           * jnp.exp(sums[small * C:(small + 1) * C])).astype(BF16))
                small += 1
        return dict(rows=rows, v=v.astype(BF16), vt=v.T.astype(BF16), xs=xs, qk=q * k,
                    qb=(q * jnp.exp(b)).astype(BF16), kr=(k * jnp.exp(tot - b)).astype(BF16),
                    ex_tot=jnp.exp(tot))

    def scores_and_state(a):
        scs, inters = [], []
        for h in range(H):
            sl = slice(h * dk, (h + 1) * dk)
            parts = [_dot_nt(a['xs'][l][:, sl], a['xs'][l][:, sl]) for l in range(NL)]
            st = st_ref[h]
            inters.append(_dot_nt(a['qb'][:, sl], st.astype(BF16)))
            st_ref[h] = st * a['ex_tot'][:, sl] + _dot(a['vt'][sl, :], a['kr'][:, sl])
            sc = msk_ref[NL] * jnp.sum(a['qk'][:, sl], axis=-1, keepdims=True)
            for l in range(NL):
                sc = sc + msk_ref[l] * parts[l]
            scs.append(sc.astype(BF16))
        return scs, inters

    def outputs(a, scs, inters):
        rows = a['rows']
        if reverse:
            of = of_ref[0, rows]
            hg = hg_ref[0, rows].astype(F32)
            gain = gain_ref[...]
        for h in range(H):
            sl = slice(h * dk, (h + 1) * dk)
            o_h = inters[h] + _dot(scs[h], a['v'][:, sl])
            if reverse:
                tot_o = of[:, sl] + o_h
                ms = jnp.mean(tot_o * tot_o, axis=-1, keepdims=True)
                o_ref[0, rows, sl] = (tot_o * lax.rsqrt(ms + RMS_EPS) * gain
                                      * _silu(hg[:, sl])).astype(o_ref.dtype)
            else:
                o_ref[0, rows, sl] = o_h

    order = list(range(CB - 1, -1, -1) if reverse else range(CB))
    cur = gates_and_factors(order[0])
    for n, j in enumerate(order):
        scs, inters = scores_and_state(cur)
        nxt = gates_and_factors(order[n + 1]) if n + 1 < CB else None
        outputs(cur, scs, inters)
        cur = nxt


def _hgrn_pass(phg, lb_logits, norm_gain, of, layer, reverse):
    B, S, W5 = phg.shape
    W = W5 // 5
    C = min(C_HGRN, S)
    CB = min(CB_HGRN, S // C)
    TB = CB * C
    nc = S // TB
    halves, amat, msk, upv = _hgrn_consts(C, reverse)
    L = lb_logits.shape[0]
    dk = W // HGRN_HEADS

    def cmap(i):
        return nc - 1 - i if reverse else i

    fcol = 2 if reverse else 1
    in_specs = [
        pl.BlockSpec((1, TB, W), lambda b, i: (b, cmap(i), 0)),
        pl.BlockSpec((1, TB, W), lambda b, i: (b, cmap(i), fcol)),
        pl.BlockSpec((1, TB, W), lambda b, i: (b, cmap(i), 3)),
        pl.BlockSpec((L, W), lambda b, i: (0, 0)),
        pl.BlockSpec(amat.shape, lambda b, i: (0, 0)),
        pl.BlockSpec(msk.shape, lambda b, i: (0, 0, 0)),
        pl.BlockSpec(upv.shape, lambda b, i: (0, 0, 0)),
    ]
    args = [phg, phg, phg, lb_logits, jnp.asarray(amat, BF16), jnp.asarray(msk, F32), jnp.asarray(upv, F32)]
    if reverse:
        in_specs += [pl.BlockSpec((1, TB, W), lambda b, i: (b, cmap(i), 0)),
                     pl.BlockSpec((1, TB, W), lambda b, i: (b, cmap(i), 4)),
                     pl.BlockSpec((1, dk), lambda b, i: (0, 0))]
        args += [of, phg, norm_gain.reshape(1, dk)]
    return pl.pallas_call(
        functools.partial(_hgrn_kernel, layer=layer, reverse=reverse, C=C, CB=CB, halves=tuple(halves)),
        grid=(B, nc),
        in_specs=in_specs,
        out_specs=pl.BlockSpec((1, TB, W), lambda b, i: (b, cmap(i), 0)),
        out_shape=jax.ShapeDtypeStruct((B, S, W), BF16 if reverse else F32),
        scratch_shapes=[pltpu.VMEM((HGRN_HEADS, dk, dk), F32)],
        compiler_params=_cparams(("parallel", "arbitrary")),
        name="hgrn_bwd" if reverse else "hgrn_fwd",
    )(*args)


def _merge_kernel(ya_ref, yl_ref, yh_ref, g0_ref, g1_ref, g2_ref, x_ref, mod_ref,
                  wa_ref, wl_ref, wh_ref, wo_ref, lng_ref, lnb_ref, wr_ref,
                  x1_ref, h2_ref, route_ref, *, alpha, n_groups, e_per):
    m = mod_ref[0]
    half = x_ref.shape[1] // 2
    halves = [slice(r0, r0 + half) for r0 in (0, half)]

    branch = [(_dot(ya_ref[0, rs], wa_ref[...]), _dot(yl_ref[0, rs], wl_ref[...]),
               _dot(yh_ref[0, rs], wh_ref[...])) for rs in halves]
    mixes = []
    for rs, (pa, pl_, ph) in zip(halves, branch):
        merged = (_sigmoid(g0_ref[0, rs].astype(F32)) * pa + _sigmoid(g1_ref[0, rs].astype(F32)) * pl_
                  + _sigmoid(g2_ref[0, rs].astype(F32)) * ph)
        mixes.append(_dot(merged.astype(BF16), wo_ref[...]))
    logit_halves = []
    for rs, mix in zip(halves, mixes):
        x1 = _layernorm(alpha * x_ref[0, rs] + (1.0 + m[2:3]) * mix, lng_ref[...], lnb_ref[...])
        x1_ref[0, rs] = x1
        h2 = x1 * (1.0 + m[4:5]) + m[3:4]
        h2_ref[0, rs] = h2
        xh, xl = _split2(h2)
        r = _dot(xh, wr_ref[...])
        logit_halves.append(r[:, :V7X_LANES] + r[:, V7X_LANES:] + _dot(xl, wr_ref[:, :V7X_LANES]))
    for rs, logits in zip(halves, logit_halves):
        route_ref[0, rs] = _route(logits, n_groups, e_per)


def _route(logits, n_groups, e_per):
    lane = lax.broadcasted_iota(jnp.int32, logits.shape, 1)
    big = jnp.int32(V7X_LANES)
    neg = jnp.float32(-jnp.inf)
    gl = jnp.where(lane < n_groups, logits, neg)
    gmax = jnp.max(gl, axis=-1, keepdims=True)
    gsel = jnp.min(jnp.where(gl == gmax, lane, big), axis=-1, keepdims=True)
    gp = 1.0 / jnp.sum(jnp.exp(gl - gmax), axis=-1, keepdims=True)
    lo = n_groups + gsel * e_per
    el = jnp.where((lane >= lo) & (lane < lo + e_per), logits, neg)
    m1 = jnp.max(el, axis=-1, keepdims=True)
    i1 = jnp.min(jnp.where(el == m1, lane, big), axis=-1, keepdims=True)
    el2 = jnp.where(lane == i1, neg, el)
    m2 = jnp.max(el2, axis=-1, keepdims=True)
    i2 = jnp.min(jnp.where(el2 == m2, lane, big), axis=-1, keepdims=True)
    z = jnp.sum(jnp.exp(el - m1), axis=-1, keepdims=True)
    p1 = 1.0 / z
    p2 = jnp.exp(m2 - m1) / z
    w1 = gp * (p1 / (p1 + p2))
    w2 = gp * (p2 / (p1 + p2))
    e1 = (i1 - n_groups).astype(F32)
    e2 = (i2 - n_groups).astype(F32)
    return jnp.where(lane == 0, w1, jnp.where(lane == 1, w2,
                     jnp.where(lane == 2, e1, jnp.where(lane == 3, e2, 0.0))))


def _merge(ya, yl, yh, pmg, x, mod_l, wa, wl, wh, wo, lng, lnb, wr, n_groups, e_per, alpha):
    B, S, D = x.shape
    tm = min(TM_MERGE, S)
    bw = ya.shape[-1]
    tok = lambda b, i: (b, i, 0)
    const = lambda b, i: (0, 0)
    return pl.pallas_call(
        functools.partial(_merge_kernel, alpha=alpha, n_groups=n_groups, e_per=e_per),
        grid=(B, S // tm),
        in_specs=[pl.BlockSpec((1, tm, bw), tok), pl.BlockSpec((1, tm, bw), tok), pl.BlockSpec((1, tm, bw), tok),
                  pl.BlockSpec((1, tm, D), lambda b, i: (b, i, 0)),
                  pl.BlockSpec((1, tm, D), lambda b, i: (b, i, 1)),
                  pl.BlockSpec((1, tm, D), lambda b, i: (b, i, 2)),
                  pl.BlockSpec((1, tm, D), tok),
                  pl.BlockSpec((1, 6, D), lambda b, i: (b, 0, 0)),
                  pl.BlockSpec((bw, D), const), pl.BlockSpec((bw, D), const), pl.BlockSpec((bw, D), const),
                  pl.BlockSpec((D, D), const),
                  pl.BlockSpec((1, D), const), pl.BlockSpec((1, D), const),
                  pl.BlockSpec((D, 2 * V7X_LANES), const)],
        out_specs=[pl.BlockSpec((1, tm, D), tok), pl.BlockSpec((1, tm, D), tok),
                   pl.BlockSpec((1, tm, V7X_LANES), tok)],
        out_shape=[jax.ShapeDtypeStruct((B, S, D), F32), jax.ShapeDtypeStruct((B, S, D), F32),
                   jax.ShapeDtypeStruct((B, S, V7X_LANES), F32)],
        compiler_params=_cparams(("parallel", "parallel")),
        name="merge",
    )(ya, yl, yh, pmg, pmg, pmg, x, mod_l, wa, wl, wh, wo, lng, lnb, wr)


def _moe_block(gidx_ref, sidx_ref, h_hbm, z_hbm, w13_ref, w2_ref, xcur, xnext, ycur, yprev,
               gsem_next, ssem_prev, *, R, F):
    half = R // 2
    abs_ = []
    for r0 in (0, half):
        abs_.append(_dot(xcur[r0:r0 + half, :].astype(BF16), w13_ref[0]))
        for r in range(r0, r0 + half):
            pltpu.make_async_copy(h_hbm.at[pl.ds(gidx_ref[0, 0, r], 1)], xnext.at[pl.ds(r, 1)],
                                  gsem_next).start()
    for n, r0 in enumerate((0, half)):
        hmid = (_silu(abs_[n][:, :F]) * abs_[n][:, F:]).astype(BF16)
        ycur[r0:r0 + half, :] = _dot(hmid, w2_ref[0])
        for r in range(r0, r0 + half):
            pltpu.make_async_copy(yprev.at[pl.ds(r, 1)], z_hbm.at[pl.ds(sidx_ref[0, 0, r], 1)],
                                  ssem_prev).start()


def _expert_kernel(blk_e_ref, g0_ref, gidx_ref, sidx_ref, h_hbm, w13_ref, w2_ref, z_hbm,
                   xbuf, ybuf, gsem, ssem, *, R, F):
    del blk_e_ref
    i = pl.program_id(0)
    last = pl.num_programs(0) - 1

    @pl.when(i == 0)
    def _():
        ybuf[1] = jnp.zeros(ybuf.shape[1:], F32)

        def start(r, c):
            pltpu.make_async_copy(h_hbm.at[pl.ds(g0_ref[0, 0, r], 1)], xbuf.at[0, pl.ds(r, 1)],
                                  gsem.at[0]).start()
            return c
        lax.fori_loop(0, R, start, 0)

    for cur in (0, 1):
        oth = 1 - cur

        @pl.when(i % 2 == cur)
        def _(cur=cur, oth=oth):
            pltpu.make_async_copy(h_hbm.at[pl.ds(0, R)], xbuf.at[cur], gsem.at[cur]).wait()

            @pl.when(i >= 1)
            def _():
                pltpu.make_async_copy(ybuf.at[cur], z_hbm.at[pl.ds(0, R)], ssem.at[cur]).wait()

            _moe_block(gidx_ref, sidx_ref, h_hbm, z_hbm, w13_ref, w2_ref, xbuf.at[cur], xbuf.at[oth],
                       ybuf.at[cur], ybuf.at[oth], gsem.at[oth], ssem.at[oth], R=R, F=F)

            @pl.when(i == last)
            def _():
                pltpu.make_async_copy(h_hbm.at[pl.ds(0, R)], xbuf.at[oth], gsem.at[oth]).wait()
                pltpu.make_async_copy(ybuf.at[oth], z_hbm.at[pl.ds(0, R)], ssem.at[oth]).wait()


def _experts(h2, blk_e, gidx, sidx, w13, w2):
    N, D = h2.shape
    R = R_MOE
    nb = sidx.shape[0] - 1
    F = w2.shape[1]
    grid_spec = pltpu.PrefetchScalarGridSpec(
        num_scalar_prefetch=1,
        grid=(nb + 1,),
        in_specs=[pl.BlockSpec((1, 1, R), lambda i, be: (0, 0, 0), memory_space=pltpu.SMEM),
                  pl.BlockSpec((1, 1, R), lambda i, be: (i + 1, 0, 0), memory_space=pltpu.SMEM),
                  pl.BlockSpec((1, 1, R), lambda i, be: (i, 0, 0), memory_space=pltpu.SMEM),
                  pl.BlockSpec(memory_space=pl.ANY),
                  pl.BlockSpec((1, D, 2 * F), lambda i, be: (be[i], 0, 0)),
                  pl.BlockSpec((1, F, D), lambda i, be: (be[i], 0, 0))],
        out_specs=pl.BlockSpec(memory_space=pl.ANY),
        scratch_shapes=[pltpu.VMEM((2, R, D), F32), pltpu.VMEM((2, R, D), F32),
                        pltpu.SemaphoreType.DMA((2,)), pltpu.SemaphoreType.DMA((2,))],
    )
    return pl.pallas_call(
        functools.partial(_expert_kernel, R=R, F=F),
        grid_spec=grid_spec,
        out_shape=jax.ShapeDtypeStruct((TOP_K * N + 2 * R, D), F32),
        compiler_params=_cparams(("arbitrary",)),
        name="moe_experts",
    )(blk_e, gidx, gidx, sidx, h2, w13, w2)


def _combine_kernel(*refs, alpha):
    z_refs = refs[:TOP_K]
    route_ref, x_ref, mod_ref, lng_ref, lnb_ref, o_ref = refs[TOP_K:]
    m = mod_ref[0]
    w = route_ref[...]
    ffn = w[:, 0:1] * z_refs[0][...]
    for a in range(1, TOP_K):
        ffn = ffn + w[:, a:a + 1] * z_refs[a][...]
    o_ref[...] = _layernorm(alpha * x_ref[...] + (1.0 + m[5:6]) * ffn, lng_ref[...], lnb_ref[...])


def _combine(z, route, x1, mod_l, lng, lnb, S, alpha):
    N, D = x1.shape
    TC = min(TC_MOE, S)
    nt = N // TC
    per_b = S // TC
    z_specs = [pl.BlockSpec((TC, D), lambda i, a=a: (i + a * nt, 0)) for a in range(TOP_K)]
    return pl.pallas_call(
        functools.partial(_combine_kernel, alpha=alpha),
        grid=(nt,),
        in_specs=z_specs + [pl.BlockSpec((TC, V7X_LANES), lambda i: (i, 0)),
                            pl.BlockSpec((TC, D), lambda i: (i, 0)),
                            pl.BlockSpec((1, 6, D), lambda i: (i // per_b, 0, 0)),
                            pl.BlockSpec((1, D), lambda i: (0, 0)),
                            pl.BlockSpec((1, D), lambda i: (0, 0))],
        out_specs=pl.BlockSpec((TC, D), lambda i: (i, 0)),
        out_shape=jax.ShapeDtypeStruct((N, D), F32),
        compiler_params=_cparams(("parallel",)),
        name="moe_combine",
    )(*([z] * TOP_K), route, x1, mod_l, lng, lnb)


def _dispatch_plan(route, n_experts, R):
    N = route.shape[0]
    A = N * TOP_K
    eid_f = route[:, TOP_K:2 * TOP_K].astype(jnp.int32).reshape(A)
    order = jnp.argsort(eid_f, stable=True).astype(jnp.int32)
    counts = (eid_f[:, None] == jnp.arange(n_experts, dtype=jnp.int32)).astype(jnp.int32).sum(0)
    starts = jnp.cumsum(counts) - counts
    pcounts = (counts + R - 1) // R * R
    pends = jnp.cumsum(pcounts)
    pstarts = pends - pcounts
    nb = (A + n_experts * R) // R
    bstart = jnp.arange(nb, dtype=jnp.int32) * R
    blk_e = jnp.minimum((pends[None, :] <= bstart[:, None]).astype(jnp.int32).sum(1), n_experts - 1)
    p = jnp.arange(nb * R, dtype=jnp.int32)
    blk = p // R
    e_p = blk_e[blk]
    j = p - pstarts[e_p]
    valid = (j >= 0) & (j < counts[e_p])
    aid = order[jnp.clip(starts[e_p] + j, 0, A - 1)]
    tok, slot = aid // TOP_K, aid % TOP_K
    gidx = jnp.where(valid, tok, 0).reshape(nb, 1, R)
    spare = TOP_K * N + (blk % 2) * R + p % R
    sidx = jnp.where(valid, slot * N + tok, spare).reshape(nb, 1, R)
    gidx = jnp.concatenate([gidx, jnp.zeros((2, 1, R), jnp.int32)], 0)
    lead = (TOP_K * N + R + jnp.arange(R, dtype=jnp.int32)).reshape(1, 1, R)
    sidx = jnp.concatenate([lead, sidx], 0)
    blk_e = jnp.concatenate([blk_e, blk_e[-1:]], 0)
    return blk_e, gidx, sidx


def _prep_weights(p):
    L = p['w_in'].shape[0]
    W = p['lru_conv_w'].shape[-1]
    nblk, bs = p['lru_wa'].shape[2], p['lru_wa'].shape[3]

    def blockdiag(w):
        eye = jnp.eye(nblk, dtype=w.dtype)
        return jnp.einsum('ldnij,nm->ldnimj', w, eye).reshape(L, 2, W, W)

    G, E = p['router_expert_w'].shape[1], p['router_expert_w'].shape[3]
    D = p['w_in'].shape[1]
    wr = jnp.concatenate([p['router_group_w'],
                          p['router_expert_w'].transpose(0, 2, 1, 3).reshape(L, D, G * E)], -1)
    wr = jnp.pad(wr, ((0, 0), (0, 0), (0, V7X_LANES - wr.shape[-1])))
    wr_hi = wr.astype(BF16)
    wr = jnp.concatenate([wr_hi, (wr - wr_hi.astype(F32)).astype(BF16)], -1)
    return dict(
        w_in=p['w_in'].astype(BF16),
        gq=jnp.tile(p['attn_q_gain'], (1, ATTN_HEADS))[:, None, :],
        gk=jnp.tile(p['attn_k_gain'], (1, KV_HEADS))[:, None, :],
        w_attn_o=p['w_attn_o'].astype(BF16),
        conv_w=p['lru_conv_w'], conv_b=p['lru_conv_b'][:, None, :],
        wbd=jnp.concatenate([blockdiag(p['lru_wa']), blockdiag(p['lru_wi'])], -1).astype(BF16),
        gb=jnp.concatenate([p['lru_ba'], p['lru_bi']], -1)[:, :, None, :],
        lam=p['lru_lambda'][:, :, None, :],
        w_lru_o=p['w_lru_o'].astype(BF16),
        w_hgrn_o=p['w_hgrn_o'].astype(BF16),
        w_out=p['w_out'].astype(BF16),
        ln1_g=p['ln1_g'][:, None, :], ln1_b=p['ln1_b'][:, None, :],
        ln2_g=p['ln2_g'][:, None, :], ln2_b=p['ln2_b'][:, None, :],
        wr=wr, n_groups=G, e_per=E,
        w13=p['expert_w13'].astype(BF16), w2=p['expert_w2'].astype(BF16),
    )


def _trunk(x, c, p, wp):
    B, S, D = x.shape
    L = p['w_in'].shape[0]
    alpha = (2 * L) ** 0.25
    QW = ATTN_HEADS * HEAD_DIM
    KW = KV_HEADS * HEAD_DIM
    LW = p['lru_conv_w'].shape[-1]
    HW = p['hgrn_lb_logits'].shape[-1]
    widths = (QW, 2 * KW, 2 * LW, 5 * HW, N_BRANCH * D)
    assert sum(widths) == p['w_in'].shape[-1]
    n_experts = wp['n_groups'] * wp['e_per']

    mod = _ada_mod(c, p['ada_w'], p['ada_b'])
    cos_t, sin_t = _rope_tables(S)
    bind, psw = _head_consts()
    cos_t, sin_t = jnp.asarray(cos_t), jnp.asarray(sin_t)
    bind, psw = jnp.asarray(bind, BF16), jnp.asarray(psw, BF16)

    for l in range(L):
        pq, pkv, plru, phg, pmg = _inproj(x, mod[l], wp['w_in'][l], widths)
        qt, k, vt = _attn_prep(pq, pkv, cos_t, sin_t, wp['gq'][l], wp['gk'][l], bind, psw)
        ya = _attention(qt, k, vt)
        lru_args = (plru, wp['conv_w'][l], wp['conv_b'][l], wp['wbd'][l], wp['gb'][l], wp['lam'][l])
        hf = _lru_pass(*lru_args, None, reverse=False)
        yl = _lru_pass(*lru_args, hf, reverse=True)
        of = _hgrn_pass(phg, p['hgrn_lb_logits'], p['hgrn_norm_gain'][l], None, l, reverse=False)
        yh = _hgrn_pass(phg, p['hgrn_lb_logits'], p['hgrn_norm_gain'][l], of, l, reverse=True)
        x1, h2, route = _merge(ya, yl, yh, pmg, x, mod[l], wp['w_attn_o'][l], wp['w_lru_o'][l],
                               wp['w_hgrn_o'][l], wp['w_out'][l], wp['ln1_g'][l], wp['ln1_b'][l],
                               wp['wr'][l], wp['n_groups'], wp['e_per'], alpha)
        N = B * S
        route = route.reshape(N, V7X_LANES)
        blk_e, gidx, sidx = _dispatch_plan(route, n_experts, R_MOE)
        z = _experts(h2.reshape(N, D), blk_e, gidx, sidx, wp['w13'][l], wp['w2'][l])
        x = _combine(z, route, x1.reshape(N, D), mod[l], wp['ln2_g'][l], wp['ln2_b'][l], S,
                     alpha).reshape(B, S, D)
    return x


def kernel(x_prompt, x_sample, c_prompt, c_sample, ada_w, ada_b, w_in, attn_q_gain, attn_k_gain, w_attn_o, lru_conv_w, lru_conv_b, lru_wa, lru_ba, lru_wi, lru_bi, lru_lambda, w_lru_o, hgrn_lb_logits, hgrn_norm_gain, w_hgrn_o, w_out, ln1_g, ln1_b, router_group_w, router_expert_w, expert_w13, expert_w2, ln2_g, ln2_b):
    p = dict(ada_w=ada_w, ada_b=ada_b, w_in=w_in, attn_q_gain=attn_q_gain, attn_k_gain=attn_k_gain,
             w_attn_o=w_attn_o, lru_conv_w=lru_conv_w, lru_conv_b=lru_conv_b, lru_wa=lru_wa, lru_ba=lru_ba,
             lru_wi=lru_wi, lru_bi=lru_bi, lru_lambda=lru_lambda, w_lru_o=w_lru_o,
             hgrn_lb_logits=hgrn_lb_logits, hgrn_norm_gain=hgrn_norm_gain, w_hgrn_o=w_hgrn_o, w_out=w_out,
             ln1_g=ln1_g, ln1_b=ln1_b, router_group_w=router_group_w, router_expert_w=router_expert_w,
             expert_w13=expert_w13, expert_w2=expert_w2, ln2_g=ln2_g, ln2_b=ln2_b)
    wp = _prep_weights(p)
    return (_trunk(x_prompt, c_prompt, p, wp), _trunk(x_sample, c_sample, p, wp))
```

```python
import functools
import math

import numpy as np
import jax
import jax.numpy as jnp
from jax import lax
from jax.experimental import pallas as pl
from jax.experimental.pallas import tpu as pltpu

F32 = jnp.float32
BF16 = jnp.bfloat16

GRID_W = 64
CONV_W = 4
CONV_LEFT = 2
LRU_C = 8.0
ATTN_HEADS = 8
KV_HEADS = 2
HEAD_DIM = 64
GQA_GROUP = ATTN_HEADS // KV_HEADS
AXIAL_DIM = HEAD_DIM // 2
AXIAL_FREQS = AXIAL_DIM // 2
ROPE_THETA = 10000.0
HGRN_HEADS = 4
N_BRANCH = 3
TOP_K = 2
LN_EPS = 1e-5
RMS_EPS = 1e-6
LOG2_E = 1.4426950408889634

V7X_LANES = 128
V7X_SUBLANES = 8
V7X_VMEM_LIMIT_BYTES = 56 * 1024 * 1024

TM_INPROJ = 512
V_AUG_ROWS = HEAD_DIM + 16
TQ_ATTN = 512
T_LRU = 256
C_HGRN = 128
CB_HGRN = 4
TK_ATTN = 512
TM_MERGE = 512
R_MOE = 256
TM_DISPATCH = 256
TC_MOE = 256


def _cparams(sem):
    return pltpu.CompilerParams(dimension_semantics=sem, vmem_limit_bytes=V7X_VMEM_LIMIT_BYTES)


def _dot(a, b):
    return jnp.dot(a, b, preferred_element_type=F32)


def _dot_nt(a, b):
    return lax.dot_general(a, b, (((1,), (1,)), ((), ())), preferred_element_type=F32)


def _dot_hi(a, b):
    return jnp.dot(a, b, preferred_element_type=F32, precision=lax.Precision.HIGHEST)


def _split2(x):
    hi = x.astype(BF16)
    lo = (x - hi.astype(F32)).astype(BF16)
    return hi, lo


def _dot_x_sel(x, m):
    hi, lo = _split2(x)
    return _dot(hi, m) + _dot(lo, m)


def _dot_sel_x(m, x):
    hi, lo = _split2(x)
    return _dot(m, hi) + _dot(m, lo)


def _sigmoid(x):
    return 1.0 / (1.0 + jnp.exp(-x))


def _silu(x):
    return x * _sigmoid(x)


def _layernorm(y, g, b):
    mu = jnp.mean(y, axis=-1, keepdims=True)
    d = y - mu
    var = jnp.mean(d * d, axis=-1, keepdims=True)
    return d * lax.rsqrt(var + LN_EPS) * g + b


def _ada_kernel(c_ref, w_ref, b_ref, o_ref):
    c = c_ref[...]
    o_ref[0] = _dot_hi(_silu(c), w_ref[0]) + b_ref[0]


def _ada_mod(c, ada_w, ada_b):
    L, D, N6 = ada_w.shape
    B = c.shape[0]
    tn = N6 // 6
    out = pl.pallas_call(
        _ada_kernel,
        grid=(L, N6 // tn),
        in_specs=[pl.BlockSpec((B, D), lambda l, j: (0, 0)),
                  pl.BlockSpec((1, D, tn), lambda l, j: (l, 0, j)),
                  pl.BlockSpec((1, 1, tn), lambda l, j: (l, 0, j))],
        out_specs=pl.BlockSpec((1, B, tn), lambda l, j: (l, 0, j)),
        out_shape=jax.ShapeDtypeStruct((L, B, N6), F32),
        compiler_params=_cparams(("arbitrary", "arbitrary")),
        name="ada_mod",
    )(c, ada_w, ada_b.reshape(L, 1, N6))
    return out.reshape(L, B, 6, N6 // 6)


def _inproj_kernel(x_ref, mod_ref, w_ref, *o_refs, chunk):
    m = mod_ref[0]
    h = (x_ref[0] * (1.0 + m[1:2]) + m[0:1]).astype(BF16)
    off = 0
    for o_ref in o_refs:
        n = o_ref.shape[-1]
        for c0 in range(0, n, chunk):
            c1 = min(c0 + chunk, n)
            o_ref[0, :, c0:c1] = _dot(h, w_ref[:, off + c0:off + c1]).astype(o_ref.dtype)
        off += n


def _inproj(x, mod_l, w_in_bf, widths):
    B, S, D = x.shape
    N = w_in_bf.shape[1]
    tm = min(TM_INPROJ, S)
    out_shape = [jax.ShapeDtypeStruct((B, S, n), BF16) for n in widths]
    out_specs = [pl.BlockSpec((1, tm, n), lambda b, i: (b, i, 0)) for n in widths]
    return pl.pallas_call(
        functools.partial(_inproj_kernel, chunk=512),
        grid=(B, S // tm),
        in_specs=[pl.BlockSpec((1, tm, D), lambda b, i: (b, i, 0)),
                  pl.BlockSpec((1, 6, D), lambda b, i: (b, 0, 0)),
                  pl.BlockSpec((D, N), lambda b, i: (0, 0), pipeline_mode=pl.Buffered(1))],
        out_specs=out_specs,
        out_shape=out_shape,
        compiler_params=_cparams(("parallel", "parallel")),
        name="in_proj",
    )(x, mod_l, w_in_bf)


def _rope_tables(S):
    t = np.arange(S)
    row = (t // GRID_W).astype(np.float64)
    col = (t % GRID_W).astype(np.float64)
    inv = ROPE_THETA ** (-np.arange(AXIAL_FREQS, dtype=np.float64) / AXIAL_FREQS)
    ang_r = (row[:, None].astype(np.float32) * inv.astype(np.float32)[None, :]).astype(np.float32)
    ang_c = (col[:, None].astype(np.float32) * inv.astype(np.float32)[None, :]).astype(np.float32)
    cos_h = np.concatenate([np.cos(ang_r), np.cos(ang_r), np.cos(ang_c), np.cos(ang_c)], -1)
    sin_h = np.concatenate([np.sin(ang_r), np.sin(ang_r), np.sin(ang_c), np.sin(ang_c)], -1)
    cos_t = np.tile(cos_h, (1, ATTN_HEADS)).astype(np.float32)
    sin_t = np.tile(sin_h, (1, ATTN_HEADS)).astype(np.float32)
    return cos_t, sin_t


def _head_consts():
    W = ATTN_HEADS * HEAD_DIM
    d = np.arange(W)
    bind = (d[:, None] // HEAD_DIM == d[None, :] // HEAD_DIM).astype(np.float32) / HEAD_DIM
    psw = np.zeros((W, W), np.float32)
    lowhalf = (d % AXIAL_DIM) < AXIAL_FREQS
    for j in d:
        if lowhalf[j]:
            psw[j + AXIAL_FREQS, j] = -1.0
        else:
            psw[j - AXIAL_FREQS, j] = 1.0
    return bind, psw


def _prep_kernel(pq_ref, pkv_ref, cos_ref, sin_ref, gq_ref, gk_ref, bind_ref, psw_ref,
                 qt_ref, k_ref, vt_ref):
    KW = KV_HEADS * HEAD_DIM
    cos = cos_ref[...]
    sin = sin_ref[...]

    def norm_rope(u, gain, w):
        ms = _dot_x_sel(u * u, bind_ref[:w, :w])
        un = u * lax.rsqrt(ms + RMS_EPS) * gain
        return un * cos[:, :w] + _dot_x_sel(un, psw_ref[:w, :w]) * sin[:, :w]

    q = norm_rope(pq_ref[0].astype(F32), gq_ref[...], ATTN_HEADS * HEAD_DIM)
    qt = (q * (HEAD_DIM ** -0.5 * LOG2_E)).T
    zeros = jnp.zeros((HEAD_DIM, qt.shape[1]), F32)
    for h in range(ATTN_HEADS):
        rows = [zeros] * KV_HEADS
        rows[h // GQA_GROUP] = qt[h * HEAD_DIM:(h + 1) * HEAD_DIM]
        qt_ref[0, h] = jnp.concatenate(rows, axis=0).astype(BF16)
    kv = pkv_ref[0]
    k_ref[0, 0] = norm_rope(kv[:, :KW].astype(F32), gk_ref[...], KW).astype(BF16)
    vt = kv[:, KW:].astype(F32).T
    tm = vt.shape[1]
    extra = (lax.broadcasted_iota(jnp.int32, (V_AUG_ROWS - HEAD_DIM, tm), 0) == 0).astype(F32)
    vt_ref[0, 0] = jnp.concatenate(
        [blk for g in range(KV_HEADS) for blk in (vt[g * HEAD_DIM:(g + 1) * HEAD_DIM], extra)],
        axis=0).astype(BF16)


def _attn_prep(pq, pkv, cos_t, sin_t, gq, gk, bind, psw):
    B, S, QW = pq.shape
    KW = KV_HEADS * HEAD_DIM
    tm = min(TK_ATTN, S)
    return pl.pallas_call(
        _prep_kernel,
        grid=(B, S // tm),
        in_specs=[pl.BlockSpec((1, tm, QW), lambda b, i: (b, i, 0)),
                  pl.BlockSpec((1, tm, 2 * KW), lambda b, i: (b, i, 0)),
                  pl.BlockSpec((tm, QW), lambda b, i: (i, 0)),
                  pl.BlockSpec((tm, QW), lambda b, i: (i, 0)),
                  pl.BlockSpec((1, QW), lambda b, i: (0, 0)),
                  pl.BlockSpec((1, KW), lambda b, i: (0, 0)),
                  pl.BlockSpec((QW, QW), lambda b, i: (0, 0)),
                  pl.BlockSpec((QW, QW), lambda b, i: (0, 0))],
        out_specs=[pl.BlockSpec((1, ATTN_HEADS, KW, tm), lambda b, i: (b, 0, 0, i)),
                   pl.BlockSpec((1, 1, tm, KW), lambda b, i: (b, i, 0, 0)),
                   pl.BlockSpec((1, 1, KV_HEADS * V_AUG_ROWS, tm), lambda b, i: (b, i, 0, 0))],
        out_shape=[jax.ShapeDtypeStruct((B, ATTN_HEADS, KW, S), BF16),
                   jax.ShapeDtypeStruct((B, S // tm, tm, KW), BF16),
                   jax.ShapeDtypeStruct((B, S // tm, KV_HEADS * V_AUG_ROWS, tm), BF16)],
        compiler_params=_cparams(("parallel", "parallel")),
        name="attn_prep",
    )(pq, pkv, cos_t, sin_t, gq, gk, bind, psw)


def _attn_kernel(qt_ref, k_ref, vt_ref, o_ref, st_scr, m_scr, acc_scr, *, nk):
    def scores_to(slot, c):
        kc = k_ref[0, c]
        for h in range(GQA_GROUP):
            st_scr[slot, h] = _dot(kc, qt_ref[0, h])

    def consume(slot, c):
        vc = vt_ref[0, c]
        for h in range(GQA_GROUP):
            st = st_scr[slot, h]
            m_old = m_scr[h]
            m_new = jnp.maximum(m_old, jnp.max(st, axis=0, keepdims=True))
            p = jnp.exp2(st - m_new).astype(BF16)
            acc_scr[h] = jnp.exp2(m_old - m_new) * acc_scr[h] + _dot(vc, p)
            m_scr[h] = m_new

    m_scr[...] = jnp.full(m_scr.shape, -jnp.inf, F32)
    acc_scr[...] = jnp.zeros(acc_scr.shape, F32)
    scores_to(0, 0)

    def step(nslot, cn, cslot, cc):
        kc = k_ref[0, cn]
        vc = vt_ref[0, cc]
        for h in range(GQA_GROUP):
            st_scr[nslot, h] = _dot(kc, qt_ref[0, h])
            st = st_scr[cslot, h]
            m_old = m_scr[h]
            m_new = jnp.maximum(m_old, jnp.max(st, axis=0, keepdims=True))
            p = jnp.exp2(st - m_new).astype(BF16)
            acc_scr[h] = jnp.exp2(m_old - m_new) * acc_scr[h] + _dot(vc, p)
            m_scr[h] = m_new

    def body(j, carry):
        c0 = 2 * j
        step(1, c0 + 1, 0, c0)
        step(0, c0 + 2, 1, c0 + 1)
        return carry
    lax.fori_loop(0, nk // 2 - 1, body, 0)
    step(1, nk - 1, 0, nk - 2)
    consume(1, nk - 1)

    outs = []
    for h in range(GQA_GROUP):
        acc = acc_scr[h]
        outs.append((acc[:HEAD_DIM] / acc[HEAD_DIM:HEAD_DIM + 1]).T)
    o_ref[0] = jnp.concatenate(outs, axis=-1).astype(o_ref.dtype)


def _attention(qt, k, vt):
    B, nk, tk, KW = k.shape
    S = nk * tk
    GW = GQA_GROUP * HEAD_DIM
    tq = min(TQ_ATTN, S)
    assert nk % 2 == 0
    scratch = [pltpu.VMEM((2, GQA_GROUP, tk, tq), F32), pltpu.VMEM((GQA_GROUP, 1, tq), F32),
               pltpu.VMEM((GQA_GROUP, V_AUG_ROWS, tq), F32)]
    return pl.pallas_call(
        functools.partial(_attn_kernel, nk=nk),
        scratch_shapes=scratch,
        grid=(B, KV_HEADS, S // tq),
        in_specs=[pl.BlockSpec((1, GQA_GROUP, KW, tq), lambda b, g, i: (b, g, 0, i)),
                  pl.BlockSpec((1, nk, tk, KW), lambda b, g, i: (b, 0, 0, 0)),
                  pl.BlockSpec((1, nk, V_AUG_ROWS, tk), lambda b, g, i: (b, 0, g, 0))],
        out_specs=pl.BlockSpec((1, tq, GW), lambda b, g, i: (b, i, g)),
        out_shape=jax.ShapeDtypeStruct((B, S, ATTN_HEADS * HEAD_DIM), BF16),
        compiler_params=_cparams(("parallel", "parallel", "parallel")),
        name="attention",
    )(qt, k, vt)


def _log1p(y):
    u = 1.0 + y
    return jnp.where(u == 1.0, y, jnp.log(u) * (y / (u - 1.0)))


def _softplus(z):
    return jnp.maximum(z, 0.0) + _log1p(jnp.exp(-jnp.abs(z)))


def _neg_expm1_2x(z, ez):
    return -jnp.tanh(z) * (ez * ez + 1.0)


def _gelu_tanh(x):
    return 0.5 * x * (1.0 + jnp.tanh(math.sqrt(2.0 / math.pi) * (x + 0.044715 * (x * x * x))))


def _lru_kernel(*refs, reverse, T):
    if reverse:
        (x_ref, xp_ref, xn_ref, cw_ref, cb_ref, wbd_ref, gb_ref, lam_ref, hf_ref, lg_ref,
         o_ref, carry_ref) = refs
    else:
        (x_ref, xp_ref, xn_ref, cw_ref, cb_ref, wbd_ref, gb_ref, lam_ref,
         o_ref, carry_ref) = refs
    i = pl.program_id(1)
    nt = pl.num_programs(1)
    tile = nt - 1 - i if reverse else i
    W = x_ref.shape[-1]

    @pl.when(i == 0)
    def _():
        carry_ref[...] = jnp.zeros_like(carry_ref)

    x = x_ref[0].astype(F32)
    prev = jnp.where(tile == 0, 0.0, xp_ref[0].astype(F32))
    nxt = jnp.where(tile == nt - 1, 0.0, xn_ref[0].astype(F32))
    row8 = lax.broadcasted_iota(jnp.int32, (V7X_SUBLANES, W), 0)

    def shifted(k):
        if k == 0:
            return x
        r = pltpu.roll(x, (-k) % T, axis=0)
        if k < 0:
            fill = pltpu.roll(prev, (-k) % V7X_SUBLANES, axis=0)
            head = jnp.where(row8 < -k, fill, r[:V7X_SUBLANES])
            return jnp.concatenate([head, r[V7X_SUBLANES:]], axis=0)
        fill = pltpu.roll(nxt, (-k) % V7X_SUBLANES, axis=0)
        tail = jnp.where(row8 >= V7X_SUBLANES - k, fill, r[T - V7X_SUBLANES:])
        return jnp.concatenate([r[:T - V7X_SUBLANES], tail], axis=0)

    cw = cw_ref[...]
    xc = cb_ref[...] + sum(shifted(j - CONV_LEFT) * cw[j:j + 1] for j in range(CONV_W))

    gates = _dot(xc.astype(BF16), wbd_ref[0]) + gb_ref[0]
    r = _sigmoid(gates[:, :W])
    ig = _sigmoid(gates[:, W:])
    log_a = (-LRU_C) * r * _softplus(-lam_ref[0])
    a = jnp.exp(log_a)
    u = jnp.sqrt(_neg_expm1_2x(log_a, a)) * (ig * xc)

    row = lax.broadcasted_iota(jnp.int32, (T, W), 0)

    def neighbour(z, d, fill):
        if d % V7X_SUBLANES == 0:
            pad = jnp.full((d, W), fill, F32)
            return jnp.concatenate([z[d:], pad] if reverse else [pad, z[:T - d]], axis=0)
        if reverse:
            return jnp.where(row < T - d, pltpu.roll(z, T - d, axis=0), fill)
        return jnp.where(row >= d, pltpu.roll(z, d, axis=0), fill)

    A, U = a, u
    d = 1
    while d < T:
        U = A * neighbour(U, d, 0.0) + U
        A = A * neighbour(A, d, 1.0)
        d *= 2
    h = A * carry_ref[...] + U
    if reverse:
        carry_ref[...] = h[0:1]
        o_ref[0] = ((hf_ref[0] + h) * _gelu_tanh(lg_ref[0].astype(F32))).astype(o_ref.dtype)
    else:
        carry_ref[...] = h[T - 1:T]
        o_ref[0] = h


def _lru_pass(plru, conv_w, conv_b, wbd, gb, lam, hf, reverse):
    B, S, W2 = plru.shape
    W = W2 // 2
    T = min(T_LRU, S)
    nt = S // T
    r8 = T // V7X_SUBLANES
    nb8 = S // V7X_SUBLANES
    d = 1 if reverse else 0

    def tmap(i):
        return nt - 1 - i if reverse else i

    in_specs = [
        pl.BlockSpec((1, T, W), lambda b, i: (b, tmap(i), 0)),
        pl.BlockSpec((1, V7X_SUBLANES, W), lambda b, i: (b, jnp.maximum(tmap(i) * r8 - 1, 0), 0)),
        pl.BlockSpec((1, V7X_SUBLANES, W), lambda b, i: (b, jnp.minimum((tmap(i) + 1) * r8, nb8 - 1), 0)),
        pl.BlockSpec((CONV_W, W), lambda b, i: (0, 0)),
        pl.BlockSpec((1, W), lambda b, i: (0, 0)),
        pl.BlockSpec((1, W, 2 * W), lambda b, i: (d, 0, 0)),
        pl.BlockSpec((1, 1, 2 * W), lambda b, i: (d, 0, 0)),
        pl.BlockSpec((1, 1, W), lambda b, i: (d, 0, 0)),
    ]
    args = [plru, plru, plru, conv_w, conv_b, wbd, gb, lam]
    if reverse:
        in_specs += [pl.BlockSpec((1, T, W), lambda b, i: (b, tmap(i), 0)),
                     pl.BlockSpec((1, T, W), lambda b, i: (b, tmap(i), 1))]
        args += [hf, plru]
    return pl.pallas_call(
        functools.partial(_lru_kernel, reverse=reverse, T=T),
        grid=(B, nt),
        in_specs=in_specs,
        out_specs=pl.BlockSpec((1, T, W), lambda b, i: (b, tmap(i), 0)),
        out_shape=jax.ShapeDtypeStruct((B, S, W), BF16 if reverse else F32),
        scratch_shapes=[pltpu.VMEM((1, W), F32)],
        compiler_params=_cparams(("parallel", "arbitrary")),
        name="lru_bwd" if reverse else "lru_fwd",
    )(*args)


def _hgrn_consts(C, reverse):
    nl = int(round(math.log2(C)))
    t = np.arange(C)
    halves = [C >> (lvl + 1) for lvl in range(nl)]
    a_rows, masks, upper = [], [], []
    for m in halves:
        blk = t // (2 * m)
        up = (t % (2 * m)) >= m
        mid = blk * 2 * m + m
        r = t[None, :]
        masks.append((blk[:, None] == blk[None, :]) & up[:, None] & (~up)[None, :])
        if m < V7X_SUBLANES:
            a_up = (r >= mid[:, None]) & (r <= t[:, None])
            a_lo = (r > t[:, None]) & (r <= mid[:, None] - 1)
            a_rows.append(np.where(up[:, None], a_up, a_lo))
            upper.append(up)
    a_rows.append(t[None, :] <= t[:, None])
    a_rows.append(np.ones((V7X_SUBLANES, C), bool))
    masks.append(np.eye(C, dtype=bool))
    if reverse:
        a_rows = [a[::-1, ::-1] if a.shape[0] == C else a for a in a_rows]
        masks = [mm[::-1, ::-1] for mm in masks]
        upper = [u[::-1] for u in upper]
    amat = np.concatenate(a_rows, 0).astype(np.float32)
    msk = np.stack(masks, 0).astype(np.float32)
    upv = np.stack(upper, 0).astype(np.float32)[:, :, None]
    return halves, amat, msk, upv


def _hgrn_kernel(*refs, layer, reverse, C, CB, halves):
    if reverse:
        (hq_ref, hf_ref, hi_ref, lbl_ref, amat_ref, msk_ref, up_ref, of_ref, hg_ref, gain_ref,
         o_ref, st_ref) = refs
    else:
        (hq_ref, hf_ref, hi_ref, lbl_ref, amat_ref, msk_ref, up_ref, o_ref, st_ref) = refs
    H = HGRN_HEADS
    W = hq_ref.shape[-1]
    dk = W // H
    NL = len(halves)
    n_small = sum(1 for m in halves if m < V7X_SUBLANES)

    @pl.when(pl.program_id(1) == 0)
    def _():
        st_ref[...] = jnp.zeros_like(st_ref)

    lgt = lbl_ref[...]
    e = jnp.exp(lgt - jnp.max(lgt, axis=0, keepdims=True))
    p = e / jnp.sum(e, axis=0, keepdims=True)
    lb = jnp.zeros((1, W), F32)
    for j in range(1, layer + 1):
        lb = lb + p[j:j + 1]

    def gates_and_factors(j):
        rows = slice(j * C, (j + 1) * C)
        q = _silu(hq_ref[0, rows].astype(F32))
        f = lb + (1.0 - lb) * _sigmoid(hf_ref[0, rows].astype(F32))
        k = 1.0 - f
        g = jnp.log2(f)
        v = hi_ref[0, rows].astype(F32)

        sums = _dot_sel_x(amat_ref[...], g)
        b = sums[n_small * C:(n_small + 1) * C]
        tot = sums[(n_small + 1) * C:(n_small + 1) * C + 1]
        xs = []
        small = 0
        for m in halves:
            if m >= V7X_SUBLANES:
                shp = (C // (2 * m), 2 * m, W)
                b3 = b.reshape(shp)
                r0 = m if reverse else m - 1
                ref = b3[:, r0:r0 + 1, :]
                rowi = lax.broadcasted_iota(jnp.int32, shp, 1)
                qside = (rowi < m) if reverse else (rowi >= m)
                x3 = jnp.where(qside, q.reshape(shp), k.reshape(shp)) * jnp.exp2(
                    jnp.where(qside, b3 - ref, ref - b3))
                xs.append(x3.reshape(C, W).astype(BF16))
            else:
                xs.append((jnp.where(up_ref[small] > 0.5, q, k)
                           * jnp.exp2(sums[small * C:(small + 1) * C])).astype(BF16))
                small += 1
        return dict(rows=rows, v=v.astype(BF16), vt=v.T.astype(BF16), xs=xs, qk=q * k,
                    qb=(q * jnp.exp2(b)).astype(BF16), kr=(k * jnp.exp2(tot - b)).astype(BF16),
                    ex_tot=jnp.exp2(tot))

    def scores_and_state(a):
        scs, inters = [], []
        for h in range(H):
            sl = slice(h * dk, (h + 1) * dk)
            parts = [_dot_nt(a['xs'][l][:, sl], a['xs'][l][:, sl]) for l in range(NL)]
            st = st_ref[h]
            inters.append(_dot_nt(a['qb'][:, sl], st.astype(BF16)))
            st_ref[h] = st * a['ex_tot'][:, sl] + _dot(a['vt'][sl, :], a['kr'][:, sl])
            sc = msk_ref[NL] * jnp.sum(a['qk'][:, sl], axis=-1, keepdims=True)
            for l in range(NL):
                sc = sc + msk_ref[l] * parts[l]
            scs.append(sc.astype(BF16))
        return scs, inters

    def outputs(a, scs, inters):
        rows = a['rows']
        if reverse:
            of = of_ref[0, rows]
            hg = hg_ref[0, rows].astype(F32)
            gain = gain_ref[...]
        for h in range(H):
            sl = slice(h * dk, (h + 1) * dk)
            o_h = inters[h] + _dot(scs[h], a['v'][:, sl])
            if reverse:
                tot_o = of[:, sl] + o_h
                ms = jnp.mean(tot_o * tot_o, axis=-1, keepdims=True)
                o_ref[0, rows, sl] = (tot_o * lax.rsqrt(ms + RMS_EPS) * gain
                                      * _silu(hg[:, sl])).astype(o_ref.dtype)
            else:
                o_ref[0, rows, sl] = o_h

    order = list(range(CB - 1, -1, -1) if reverse else range(CB))
    cur = gates_and_factors(order[0])
    for n, j in enumerate(order):
        scs, inters = scores_and_state(cur)
        nxt = gates_and_factors(order[n + 1]) if n + 1 < CB else None
        outputs(cur, scs, inters)
        cur = nxt


def _hgrn_pass(phg, lb_logits, norm_gain, of, layer, reverse):
    B, S, W5 = phg.shape
    W = W5 // 5
    C = min(C_HGRN, S)
    CB = min(CB_HGRN, S // C)
    TB = CB * C
    nc = S // TB
    halves, amat, msk, upv = _hgrn_consts(C, reverse)
    L = lb_logits.shape[0]
    dk = W // HGRN_HEADS

    def cmap(i):
        return nc - 1 - i if reverse else i

    fcol = 2 if reverse else 1
    in_specs = [
        pl.BlockSpec((1, TB, W), lambda b, i: (b, cmap(i), 0)),
        pl.BlockSpec((1, TB, W), lambda b, i: (b, cmap(i), fcol)),
        pl.BlockSpec((1, TB, W), lambda b, i: (b, cmap(i), 3)),
        pl.BlockSpec((L, W), lambda b, i: (0, 0)),
        pl.BlockSpec(amat.shape, lambda b, i: (0, 0)),
        pl.BlockSpec(msk.shape, lambda b, i: (0, 0, 0)),
        pl.BlockSpec(upv.shape, lambda b, i: (0, 0, 0)),
    ]
    args = [phg, phg, phg, lb_logits, jnp.asarray(amat, BF16), jnp.asarray(msk, F32), jnp.asarray(upv, F32)]
    if reverse:
        in_specs += [pl.BlockSpec((1, TB, W), lambda b, i: (b, cmap(i), 0)),
                     pl.BlockSpec((1, TB, W), lambda b, i: (b, cmap(i), 4)),
                     pl.BlockSpec((1, dk), lambda b, i: (0, 0))]
        args += [of, phg, norm_gain.reshape(1, dk)]
    return pl.pallas_call(
        functools.partial(_hgrn_kernel, layer=layer, reverse=reverse, C=C, CB=CB, halves=tuple(halves)),
        grid=(B, nc),
        in_specs=in_specs,
        out_specs=pl.BlockSpec((1, TB, W), lambda b, i: (b, cmap(i), 0)),
        out_shape=jax.ShapeDtypeStruct((B, S, W), BF16 if reverse else F32),
        scratch_shapes=[pltpu.VMEM((HGRN_HEADS, dk, dk), F32)],
        compiler_params=_cparams(("parallel", "arbitrary")),
        name="hgrn_bwd" if reverse else "hgrn_fwd",
    )(*args)


def _merge_kernel(ya_ref, yl_ref, yh_ref, g0_ref, g1_ref, g2_ref, x_ref, mod_ref,
                  wa_ref, wl_ref, wh_ref, wo_ref, lng_ref, lnb_ref, wr_ref,
                  x1_ref, h2_ref, route_ref, *, alpha, n_groups, e_per):
    m = mod_ref[0]
    half = x_ref.shape[1] // 2
    halves = [slice(r0, r0 + half) for r0 in (0, half)]

    branch = [(_dot(ya_ref[0, rs], wa_ref[...]), _dot(yl_ref[0, rs], wl_ref[...]),
               _dot(yh_ref[0, rs], wh_ref[...])) for rs in halves]
    mixes = []
    for rs, (pa, pl_, ph) in zip(halves, branch):
        merged = (_sigmoid(g0_ref[0, rs].astype(F32)) * pa + _sigmoid(g1_ref[0, rs].astype(F32)) * pl_
                  + _sigmoid(g2_ref[0, rs].astype(F32)) * ph)
        mixes.append(_dot(merged.astype(BF16), wo_ref[...]))
    logit_halves = []
    for rs, mix in zip(halves, mixes):
        x1 = _layernorm(alpha * x_ref[0, rs] + (1.0 + m[2:3]) * mix, lng_ref[...], lnb_ref[...])
        x1_ref[0, rs] = x1
        h2 = x1 * (1.0 + m[4:5]) + m[3:4]
        h2_ref[0, rs] = h2
        xh, xl = _split2(h2)
        r = _dot(xh, wr_ref[...])
        logit_halves.append(r[:, :V7X_LANES] + r[:, V7X_LANES:] + _dot(xl, wr_ref[:, :V7X_LANES]))
    for rs, logits in zip(halves, logit_halves):
        route_ref[0, rs] = _route(logits, n_groups, e_per)


def _route(logits, n_groups, e_per):
    lane = lax.broadcasted_iota(jnp.int32, logits.shape, 1)
    big = jnp.int32(V7X_LANES)
    neg = jnp.float32(-jnp.inf)
    gl = jnp.where(lane < n_groups, logits, neg)
    gmax = jnp.max(gl, axis=-1, keepdims=True)
    gsel = jnp.min(jnp.where(gl == gmax, lane, big), axis=-1, keepdims=True)
    gp = 1.0 / jnp.sum(jnp.exp(gl - gmax), axis=-1, keepdims=True)
    lo = n_groups + gsel * e_per
    el = jnp.where((lane >= lo) & (lane < lo + e_per), logits, neg)
    m1 = jnp.max(el, axis=-1, keepdims=True)
    i1 = jnp.min(jnp.where(el == m1, lane, big), axis=-1, keepdims=True)
    el2 = jnp.where(lane == i1, neg, el)
    m2 = jnp.max(el2, axis=-1, keepdims=True)
    i2 = jnp.min(jnp.where(el2 == m2, lane, big), axis=-1, keepdims=True)
    z = jnp.sum(jnp.exp(el - m1), axis=-1, keepdims=True)
    p1 = 1.0 / z
    p2 = jnp.exp(m2 - m1) / z
    w1 = gp * (p1 / (p1 + p2))
    w2 = gp * (p2 / (p1 + p2))
    e1 = (i1 - n_groups).astype(F32)
    e2 = (i2 - n_groups).astype(F32)
    return jnp.where(lane == 0, w1, jnp.where(lane == 1, w2,
                     jnp.where(lane == 2, e1, jnp.where(lane == 3, e2, 0.0))))


def _merge(ya, yl, yh, pmg, x, mod_l, wa, wl, wh, wo, lng, lnb, wr, n_groups, e_per, alpha):
    B, S, D = x.shape
    tm = min(TM_MERGE, S)
    bw = ya.shape[-1]
    tok = lambda b, i: (b, i, 0)
    const = lambda b, i: (0, 0)
    return pl.pallas_call(
        functools.partial(_merge_kernel, alpha=alpha, n_groups=n_groups, e_per=e_per),
        grid=(B, S // tm),
        in_specs=[pl.BlockSpec((1, tm, bw), tok), pl.BlockSpec((1, tm, bw), tok), pl.BlockSpec((1, tm, bw), tok),
                  pl.BlockSpec((1, tm, D), lambda b, i: (b, i, 0)),
                  pl.BlockSpec((1, tm, D), lambda b, i: (b, i, 1)),
                  pl.BlockSpec((1, tm, D), lambda b, i: (b, i, 2)),
                  pl.BlockSpec((1, tm, D), tok),
                  pl.BlockSpec((1, 6, D), lambda b, i: (b, 0, 0)),
                  pl.BlockSpec((bw, D), const), pl.BlockSpec((bw, D), const), pl.BlockSpec((bw, D), const),
                  pl.BlockSpec((D, D), const),
                  pl.BlockSpec((1, D), const), pl.BlockSpec((1, D), const),
                  pl.BlockSpec((D, 2 * V7X_LANES), const)],
        out_specs=[pl.BlockSpec((1, tm, D), tok), pl.BlockSpec((1, tm, D), tok),
                   pl.BlockSpec((1, tm, V7X_LANES), tok)],
        out_shape=[jax.ShapeDtypeStruct((B, S, D), F32), jax.ShapeDtypeStruct((B, S, D), F32),
                   jax.ShapeDtypeStruct((B, S, V7X_LANES), F32)],
        compiler_params=_cparams(("parallel", "parallel")),
        name="merge",
    )(ya, yl, yh, pmg, pmg, pmg, x, mod_l, wa, wl, wh, wo, lng, lnb, wr)


def _dispatch_kernel(dest_ref, h_ref, xs_in, xs_hbm, sem, *, TM):
    del xs_in

    for r in range(TM):
        for a in range(TOP_K):
            pltpu.make_async_copy(h_ref.at[pl.ds(r, 1)],
                                  xs_hbm.at[pl.ds(dest_ref[0, 0, TOP_K * r + a], 1)],
                                  sem).start(priority=a % 2)
    for a in range(TOP_K):
        pltpu.make_async_copy(h_ref, xs_hbm.at[pl.ds(0, TM)], sem).wait()


def _dispatch(h2, dest, xs_prev, S):
    N, D = h2.shape
    TM = min(TM_DISPATCH, S)
    nt = N // TM
    return pl.pallas_call(
        functools.partial(_dispatch_kernel, TM=TM),
        grid=(nt,),
        in_specs=[pl.BlockSpec((1, 1, TOP_K * TM), lambda i: (i, 0, 0), memory_space=pltpu.SMEM),
                  pl.BlockSpec((TM, D), lambda i: (i, 0)),
                  pl.BlockSpec(memory_space=pl.ANY)],
        out_specs=pl.BlockSpec(memory_space=pl.ANY),
        out_shape=jax.ShapeDtypeStruct(xs_prev.shape, xs_prev.dtype),
        scratch_shapes=[pltpu.SemaphoreType.DMA(())],
        input_output_aliases={2: 0},
        compiler_params=_cparams(("arbitrary",)),
        name="moe_dispatch",
    )(dest.reshape(nt, 1, TOP_K * TM), h2, xs_prev)


def _expert_kernel(blk_e_ref, nused_ref, xs_ref, w13_ref, w2_ref, ys_ref, *, F):
    i = pl.program_id(0)

    @pl.when(i < nused_ref[0])
    def _():
        half = xs_ref.shape[0] // 2
        abs_ = [_dot(xs_ref[r0:r0 + half, :].astype(BF16), w13_ref[0]) for r0 in (0, half)]
        for n, r0 in enumerate((0, half)):
            hmid = (_silu(abs_[n][:, :F]) * abs_[n][:, F:]).astype(BF16)
            ys_ref[r0:r0 + half, :] = _dot(hmid, w2_ref[0])

    @pl.when(i >= nused_ref[0])
    def _():
        ys_ref[...] = jnp.zeros_like(ys_ref)


def _experts(xs, blk_e, nused, w13, w2):
    P, D = xs.shape
    R = R_MOE
    nb = P // R
    F = w2.shape[1]
    grid_spec = pltpu.PrefetchScalarGridSpec(
        num_scalar_prefetch=2,
        grid=(nb,),
        in_specs=[pl.BlockSpec((R, D), lambda i, be, nu: (i, 0)),
                  pl.BlockSpec((1, D, 2 * F), lambda i, be, nu: (be[i], 0, 0)),
                  pl.BlockSpec((1, F, D), lambda i, be, nu: (be[i], 0, 0))],
        out_specs=pl.BlockSpec((R, D), lambda i, be, nu: (i, 0)),
    )
    return pl.pallas_call(
        functools.partial(_expert_kernel, F=F),
        grid_spec=grid_spec,
        out_shape=jax.ShapeDtypeStruct((P, D), F32),
        compiler_params=_cparams(("arbitrary",)),
        name="moe_experts",
    )(blk_e, nused, xs, w13, w2)


def _combine_kernel(pos_ref, posn_ref, ys_hbm, route_ref, x_ref, mod_ref, lng_ref, lnb_ref, o_ref,
                    ybuf, sems, *, TC, alpha):
    i = pl.program_id(0)
    n = pl.num_programs(0)
    slot = i % 2

    def issue(p_ref, s):
        for r in range(TC):
            for a in range(TOP_K):
                pltpu.make_async_copy(ys_hbm.at[pl.ds(p_ref[0, 0, TOP_K * r + a], 1)],
                                      ybuf.at[s, a, pl.ds(r, 1)], sems.at[s]).start(priority=a % 2)

    @pl.when(i == 0)
    def _():
        issue(pos_ref, 0)

    for s in range(2):
        @pl.when((i + 1 < n) & (slot == 1 - s))
        def _(s=s):
            issue(posn_ref, s)

    for a in range(TOP_K):
        pltpu.make_async_copy(ys_hbm.at[pl.ds(0, TC)], ybuf.at[slot, a], sems.at[slot]).wait()

    m = mod_ref[0]
    w = route_ref[...]
    ffn = w[:, 0:1] * ybuf[slot, 0]
    for a in range(1, TOP_K):
        ffn = ffn + w[:, a:a + 1] * ybuf[slot, a]
    o_ref[...] = _layernorm(alpha * x_ref[...] + (1.0 + m[5:6]) * ffn, lng_ref[...], lnb_ref[...])


def _combine(ys, dest, route, x1, mod_l, lng, lnb, S, alpha):
    N, D = x1.shape
    TC = min(TC_MOE, S)
    nt = N // TC
    per_b = S // TC
    pos = dest.reshape(nt, 1, TOP_K * TC)
    return pl.pallas_call(
        functools.partial(_combine_kernel, TC=TC, alpha=alpha),
        grid=(nt,),
        in_specs=[pl.BlockSpec((1, 1, TOP_K * TC), lambda i: (i, 0, 0), memory_space=pltpu.SMEM),
                  pl.BlockSpec((1, 1, TOP_K * TC), lambda i: (jnp.minimum(i + 1, nt - 1), 0, 0),
                               memory_space=pltpu.SMEM),
                  pl.BlockSpec(memory_space=pl.ANY),
                  pl.BlockSpec((TC, V7X_LANES), lambda i: (i, 0)),
                  pl.BlockSpec((TC, D), lambda i: (i, 0)),
                  pl.BlockSpec((1, 6, D), lambda i: (i // per_b, 0, 0)),
                  pl.BlockSpec((1, D), lambda i: (0, 0)),
                  pl.BlockSpec((1, D), lambda i: (0, 0))],
        out_specs=pl.BlockSpec((TC, D), lambda i: (i, 0)),
        out_shape=jax.ShapeDtypeStruct((N, D), F32),
        scratch_shapes=[pltpu.VMEM((2, TOP_K, TC, D), F32), pltpu.SemaphoreType.DMA((2,))],
        compiler_params=_cparams(("arbitrary",)),
        name="moe_combine",
    )(pos, pos, ys, route, x1, mod_l, lng, lnb)


def _dispatch_plan(route, n_experts, R):
    eid = route[:, TOP_K:2 * TOP_K].astype(jnp.int32)
    hot = (eid[:, :, None] == jnp.arange(n_experts, dtype=jnp.int32)).astype(jnp.int32).sum(1)
    csum = jnp.cumsum(hot, axis=0)
    counts = csum[-1]
    rank = jnp.take_along_axis(csum - hot, eid, axis=1)
    pcounts = (counts + R - 1) // R * R
    pends = jnp.cumsum(pcounts)
    pstarts = pends - pcounts
    dest = (pstarts[eid] + rank).astype(jnp.int32)
    nb = (eid.shape[0] * TOP_K + n_experts * R) // R
    starts = jnp.arange(nb, dtype=jnp.int32) * R
    blk_e = jnp.minimum((pends[None, :] <= starts[:, None]).astype(jnp.int32).sum(1), n_experts - 1)
    nused = (pends[-1:] // R).astype(jnp.int32)
    return blk_e, nused, dest


def _prep_weights(p):
    L = p['w_in'].shape[0]
    W = p['lru_conv_w'].shape[-1]
    nblk, bs = p['lru_wa'].shape[2], p['lru_wa'].shape[3]

    def blockdiag(w):
        eye = jnp.eye(nblk, dtype=w.dtype)
        return jnp.einsum('ldnij,nm->ldnimj', w, eye).reshape(L, 2, W, W)

    G, E = p['router_expert_w'].shape[1], p['router_expert_w'].shape[3]
    D = p['w_in'].shape[1]
    wr = jnp.concatenate([p['router_group_w'],
                          p['router_expert_w'].transpose(0, 2, 1, 3).reshape(L, D, G * E)], -1)
    wr = jnp.pad(wr, ((0, 0), (0, 0), (0, V7X_LANES - wr.shape[-1])))
    wr_hi = wr.astype(BF16)
    wr = jnp.concatenate([wr_hi, (wr - wr_hi.astype(F32)).astype(BF16)], -1)
    return dict(
        w_in=p['w_in'].astype(BF16),
        gq=jnp.tile(p['attn_q_gain'], (1, ATTN_HEADS))[:, None, :],
        gk=jnp.tile(p['attn_k_gain'], (1, KV_HEADS))[:, None, :],
        w_attn_o=p['w_attn_o'].astype(BF16),
        conv_w=p['lru_conv_w'], conv_b=p['lru_conv_b'][:, None, :],
        wbd=jnp.concatenate([blockdiag(p['lru_wa']), blockdiag(p['lru_wi'])], -1).astype(BF16),
        gb=jnp.concatenate([p['lru_ba'], p['lru_bi']], -1)[:, :, None, :],
        lam=p['lru_lambda'][:, :, None, :],
        w_lru_o=p['w_lru_o'].astype(BF16),
        w_hgrn_o=p['w_hgrn_o'].astype(BF16),
        w_out=p['w_out'].astype(BF16),
        ln1_g=p['ln1_g'][:, None, :], ln1_b=p['ln1_b'][:, None, :],
        ln2_g=p['ln2_g'][:, None, :], ln2_b=p['ln2_b'][:, None, :],
        wr=wr, n_groups=G, e_per=E,
        w13=p['expert_w13'].astype(BF16), w2=p['expert_w2'].astype(BF16),
    )


def _trunk(x, c, p, wp):
    B, S, D = x.shape
    L = p['w_in'].shape[0]
    alpha = (2 * L) ** 0.25
    QW = ATTN_HEADS * HEAD_DIM
    KW = KV_HEADS * HEAD_DIM
    LW = p['lru_conv_w'].shape[-1]
    HW = p['hgrn_lb_logits'].shape[-1]
    widths = (QW, 2 * KW, 2 * LW, 5 * HW, N_BRANCH * D)
    assert sum(widths) == p['w_in'].shape[-1]
    n_experts = wp['n_groups'] * wp['e_per']

    mod = _ada_mod(c, p['ada_w'], p['ada_b'])
    cos_t, sin_t = _rope_tables(S)
    bind, psw = _head_consts()
    cos_t, sin_t = jnp.asarray(cos_t), jnp.asarray(sin_t)
    bind, psw = jnp.asarray(bind, BF16), jnp.asarray(psw, BF16)

    xs = None
    for l in range(L):
        pq, pkv, plru, phg, pmg = _inproj(x, mod[l], wp['w_in'][l], widths)
        qt, k, vt = _attn_prep(pq, pkv, cos_t, sin_t, wp['gq'][l], wp['gk'][l], bind, psw)
        ya = _attention(qt, k, vt)
        lru_args = (plru, wp['conv_w'][l], wp['conv_b'][l], wp['wbd'][l], wp['gb'][l], wp['lam'][l])
        hf = _lru_pass(*lru_args, None, reverse=False)
        yl = _lru_pass(*lru_args, hf, reverse=True)
        of = _hgrn_pass(phg, p['hgrn_lb_logits'], p['hgrn_norm_gain'][l], None, l, reverse=False)
        yh = _hgrn_pass(phg, p['hgrn_lb_logits'], p['hgrn_norm_gain'][l], of, l, reverse=True)
        x1, h2, route = _merge(ya, yl, yh, pmg, x, mod[l], wp['w_attn_o'][l], wp['w_lru_o'][l],
                               wp['w_hgrn_o'][l], wp['w_out'][l], wp['ln1_g'][l], wp['ln1_b'][l],
                               wp['wr'][l], wp['n_groups'], wp['e_per'], alpha)
        N = B * S
        route = route.reshape(N, V7X_LANES)
        blk_e, nused, dest = _dispatch_plan(route, n_experts, R_MOE)
        if xs is None:
            xs = jnp.zeros((N * TOP_K + n_experts * R_MOE, D), F32)
        xs = _dispatch(h2.reshape(N, D), dest, xs, S)
        ys = _experts(xs, blk_e, nused, wp['w13'][l], wp['w2'][l])
        x = _combine(ys, dest, route, x1.reshape(N, D), mod[l], wp['ln2_g'][l], wp['ln2_b'][l], S,
                     alpha).reshape(B, S, D)
    return x


def kernel(x_prompt, x_sample, c_prompt, c_sample, ada_w, ada_b, w_in, attn_q_gain, attn_k_gain, w_attn_o, lru_conv_w, lru_conv_b, lru_wa, lru_ba, lru_wi, lru_bi, lru_lambda, w_lru_o, hgrn_lb_logits, hgrn_norm_gain, w_hgrn_o, w_out, ln1_g, ln1_b, router_group_w, router_expert_w, expert_w13, expert_w2, ln2_g, ln2_b):
    p = dict(ada_w=ada_w, ada_b=ada_b, w_in=w_in, attn_q_gain=attn_q_gain, attn_k_gain=attn_k_gain,
             w_attn_o=w_attn_o, lru_conv_w=lru_conv_w, lru_conv_b=lru_conv_b, lru_wa=lru_wa, lru_ba=lru_ba,
             lru_wi=lru_wi, lru_bi=lru_bi, lru_lambda=lru_lambda, w_lru_o=w_lru_o,
             hgrn_lb_logits=hgrn_lb_logits, hgrn_norm_gain=hgrn_norm_gain, w_hgrn_o=w_hgrn_o, w_out=w_out,
             ln1_g=ln1_g, ln1_b=ln1_b, router_group_w=router_group_w, router_expert_w=router_expert_w,
             expert_w13=expert_w13, expert_w2=expert_w2, ln2_g=ln2_g, ln2_b=ln2_b)
    wp = _prep_weights(p)
    return (_trunk(x_prompt, c_prompt, p, wp), _trunk(x_sample, c_sample, p, wp))
```

```python
import functools
import math

import numpy as np
import jax
import jax.numpy as jnp
from jax import lax
from jax.experimental import pallas as pl
from jax.experimental.pallas import tpu as pltpu

F32 = jnp.float32
BF16 = jnp.bfloat16

GRID_W = 64
CONV_W = 4
CONV_LEFT = 2
LRU_C = 8.0
ATTN_HEADS = 8
KV_HEADS = 2
HEAD_DIM = 64
GQA_GROUP = ATTN_HEADS // KV_HEADS
AXIAL_DIM = HEAD_DIM // 2
AXIAL_FREQS = AXIAL_DIM // 2
ROPE_THETA = 10000.0
HGRN_HEADS = 4
N_BRANCH = 3
TOP_K = 2
LN_EPS = 1e-5
RMS_EPS = 1e-6
LOG2_E = 1.4426950408889634

V7X_LANES = 128
V7X_SUBLANES = 8
V7X_VMEM_LIMIT_BYTES = 56 * 1024 * 1024

TM_INPROJ = 512
V_AUG_ROWS = HEAD_DIM + 16
TQ_ATTN = 512
T_LRU = 256
C_HGRN = 128
CB_HGRN = 4
TK_ATTN = 512
TM_MERGE = 512
R_MOE = 512
TM_DISPATCH = 256
TC_MOE = 256


def _cparams(sem):
    return pltpu.CompilerParams(dimension_semantics=sem, vmem_limit_bytes=V7X_VMEM_LIMIT_BYTES)


def _dot(a, b):
    return jnp.dot(a, b, preferred_element_type=F32)


def _dot_nt(a, b):
    return lax.dot_general(a, b, (((1,), (1,)), ((), ())), preferred_element_type=F32)


def _dot_hi(a, b):
    return jnp.dot(a, b, preferred_element_type=F32, precision=lax.Precision.HIGHEST)


def _split2(x):
    hi = x.astype(BF16)
    lo = (x - hi.astype(F32)).astype(BF16)
    return hi, lo


def _dot_x_sel(x, m):
    hi, lo = _split2(x)
    return _dot(hi, m) + _dot(lo, m)


def _dot_sel_x(m, x):
    hi, lo = _split2(x)
    return _dot(m, hi) + _dot(m, lo)


def _sigmoid(x):
    return 1.0 / (1.0 + jnp.exp(-x))


def _silu(x):
    return x * _sigmoid(x)


def _layernorm(y, g, b):
    mu = jnp.mean(y, axis=-1, keepdims=True)
    d = y - mu
    var = jnp.mean(d * d, axis=-1, keepdims=True)
    return d * lax.rsqrt(var + LN_EPS) * g + b


def _ada_kernel(c_ref, w_ref, b_ref, o_ref):
    c = c_ref[...]
    o_ref[0] = _dot_hi(_silu(c), w_ref[0]) + b_ref[0]


def _ada_mod(c, ada_w, ada_b):
    L, D, N6 = ada_w.shape
    B = c.shape[0]
    tn = N6 // 6
    out = pl.pallas_call(
        _ada_kernel,
        grid=(L, N6 // tn),
        in_specs=[pl.BlockSpec((B, D), lambda l, j: (0, 0)),
                  pl.BlockSpec((1, D, tn), lambda l, j: (l, 0, j)),
                  pl.BlockSpec((1, 1, tn), lambda l, j: (l, 0, j))],
        out_specs=pl.BlockSpec((1, B, tn), lambda l, j: (l, 0, j)),
        out_shape=jax.ShapeDtypeStruct((L, B, N6), F32),
        compiler_params=_cparams(("arbitrary", "arbitrary")),
        name="ada_mod",
    )(c, ada_w, ada_b.reshape(L, 1, N6))
    return out.reshape(L, B, 6, N6 // 6)


def _inproj_kernel(x_ref, mod_ref, w_ref, *o_refs, chunk, acts):
    m = mod_ref[0]
    h = (x_ref[0] * (1.0 + m[1:2]) + m[0:1]).astype(BF16)
    off = 0
    for o_ref, segs in zip(o_refs, acts):
        n = o_ref.shape[-1]
        for c0 in range(0, n, chunk):
            c1 = min(c0 + chunk, n)
            act = [f for start, f in segs if start <= c0][-1]
            y = _dot(h, w_ref[:, off + c0:off + c1])
            o_ref[0, :, c0:c1] = (y if act is None else act(y)).astype(o_ref.dtype)
        off += n


def _inproj(x, mod_l, w_in_bf, widths, acts):
    B, S, D = x.shape
    N = w_in_bf.shape[1]
    tm = min(TM_INPROJ, S)
    out_shape = [jax.ShapeDtypeStruct((B, S, n), BF16) for n in widths]
    out_specs = [pl.BlockSpec((1, tm, n), lambda b, i: (b, i, 0)) for n in widths]
    return pl.pallas_call(
        functools.partial(_inproj_kernel, chunk=512, acts=acts),
        grid=(B, S // tm),
        in_specs=[pl.BlockSpec((1, tm, D), lambda b, i: (b, i, 0)),
                  pl.BlockSpec((1, 6, D), lambda b, i: (b, 0, 0)),
                  pl.BlockSpec((D, N), lambda b, i: (0, 0), pipeline_mode=pl.Buffered(1))],
        out_specs=out_specs,
        out_shape=out_shape,
        compiler_params=_cparams(("parallel", "parallel")),
        name="in_proj",
    )(x, mod_l, w_in_bf)


def _rope_tables(S):
    t = np.arange(S)
    row = (t // GRID_W).astype(np.float64)
    col = (t % GRID_W).astype(np.float64)
    inv = ROPE_THETA ** (-np.arange(AXIAL_FREQS, dtype=np.float64) / AXIAL_FREQS)
    ang_r = (row[:, None].astype(np.float32) * inv.astype(np.float32)[None, :]).astype(np.float32)
    ang_c = (col[:, None].astype(np.float32) * inv.astype(np.float32)[None, :]).astype(np.float32)
    cos_h = np.concatenate([np.cos(ang_r), np.cos(ang_r), np.cos(ang_c), np.cos(ang_c)], -1)
    sin_h = np.concatenate([np.sin(ang_r), np.sin(ang_r), np.sin(ang_c), np.sin(ang_c)], -1)
    cos_t = np.tile(cos_h, (1, ATTN_HEADS)).astype(np.float32)
    sin_t = np.tile(sin_h, (1, ATTN_HEADS)).astype(np.float32)
    return cos_t, sin_t


def _head_consts():
    W = ATTN_HEADS * HEAD_DIM
    d = np.arange(W)
    bind = (d[:, None] // HEAD_DIM == d[None, :] // HEAD_DIM).astype(np.float32) / HEAD_DIM
    psw = np.zeros((W, W), np.float32)
    lowhalf = (d % AXIAL_DIM) < AXIAL_FREQS
    for j in d:
        if lowhalf[j]:
            psw[j + AXIAL_FREQS, j] = -1.0
        else:
            psw[j - AXIAL_FREQS, j] = 1.0
    return bind, psw


def _prep_kernel(pq_ref, pkv_ref, cos_ref, sin_ref, gq_ref, gk_ref, bind_ref, psw_ref,
                 qt_ref, k_ref, vt_ref):
    KW = KV_HEADS * HEAD_DIM
    cos = cos_ref[...]
    sin = sin_ref[...]

    def norm_rope(u, gain, w):
        ms = _dot_x_sel(u * u, bind_ref[:w, :w])
        un = u * lax.rsqrt(ms + RMS_EPS) * gain
        return un * cos[:, :w] + _dot_x_sel(un, psw_ref[:w, :w]) * sin[:, :w]

    q = norm_rope(pq_ref[0].astype(F32), gq_ref[...], ATTN_HEADS * HEAD_DIM)
    qt = (q * (HEAD_DIM ** -0.5 * LOG2_E)).T
    zeros = jnp.zeros((HEAD_DIM, qt.shape[1]), F32)
    for h in range(ATTN_HEADS):
        rows = [zeros] * KV_HEADS
        rows[h // GQA_GROUP] = qt[h * HEAD_DIM:(h + 1) * HEAD_DIM]
        qt_ref[0, h] = jnp.concatenate(rows, axis=0).astype(BF16)
    kv = pkv_ref[0]
    k_ref[0, 0] = norm_rope(kv[:, :KW].astype(F32), gk_ref[...], KW).astype(BF16)
    vt = kv[:, KW:].astype(F32).T
    tm = vt.shape[1]
    extra = (lax.broadcasted_iota(jnp.int32, (V_AUG_ROWS - HEAD_DIM, tm), 0) == 0).astype(F32)
    vt_ref[0, 0] = jnp.concatenate(
        [blk for g in range(KV_HEADS) for blk in (vt[g * HEAD_DIM:(g + 1) * HEAD_DIM], extra)],
        axis=0).astype(BF16)


def _attn_prep(pq, pkv, cos_t, sin_t, gq, gk, bind, psw):
    B, S, QW = pq.shape
    KW = KV_HEADS * HEAD_DIM
    tm = min(TK_ATTN, S)
    return pl.pallas_call(
        _prep_kernel,
        grid=(B, S // tm),
        in_specs=[pl.BlockSpec((1, tm, QW), lambda b, i: (b, i, 0)),
                  pl.BlockSpec((1, tm, 2 * KW), lambda b, i: (b, i, 0)),
                  pl.BlockSpec((tm, QW), lambda b, i: (i, 0)),
                  pl.BlockSpec((tm, QW), lambda b, i: (i, 0)),
                  pl.BlockSpec((1, QW), lambda b, i: (0, 0)),
                  pl.BlockSpec((1, KW), lambda b, i: (0, 0)),
                  pl.BlockSpec((QW, QW), lambda b, i: (0, 0)),
                  pl.BlockSpec((QW, QW), lambda b, i: (0, 0))],
        out_specs=[pl.BlockSpec((1, ATTN_HEADS, KW, tm), lambda b, i: (b, 0, 0, i)),
                   pl.BlockSpec((1, 1, tm, KW), lambda b, i: (b, i, 0, 0)),
                   pl.BlockSpec((1, 1, KV_HEADS * V_AUG_ROWS, tm), lambda b, i: (b, i, 0, 0))],
        out_shape=[jax.ShapeDtypeStruct((B, ATTN_HEADS, KW, S), BF16),
                   jax.ShapeDtypeStruct((B, S // tm, tm, KW), BF16),
                   jax.ShapeDtypeStruct((B, S // tm, KV_HEADS * V_AUG_ROWS, tm), BF16)],
        compiler_params=_cparams(("parallel", "parallel")),
        name="attn_prep",
    )(pq, pkv, cos_t, sin_t, gq, gk, bind, psw)


def _attn_kernel(qt_ref, k_ref, vt_ref, o_ref, st_scr, m_scr, acc_scr, *, nk):
    def scores_to(slot, c):
        kc = k_ref[0, c]
        for h in range(GQA_GROUP):
            st_scr[slot, h] = _dot(kc, qt_ref[0, h])

    def consume(slot, c):
        vc = vt_ref[0, c]
        for h in range(GQA_GROUP):
            st = st_scr[slot, h]
            m_old = m_scr[h]
            m_new = jnp.maximum(m_old, jnp.max(st, axis=0, keepdims=True))
            p = jnp.exp2(st - m_new).astype(BF16)
            acc_scr[h] = jnp.exp2(m_old - m_new) * acc_scr[h] + _dot(vc, p)
            m_scr[h] = m_new

    m_scr[...] = jnp.full(m_scr.shape, -jnp.inf, F32)
    acc_scr[...] = jnp.zeros(acc_scr.shape, F32)
    scores_to(0, 0)

    def step(nslot, cn, cslot, cc):
        kc = k_ref[0, cn]
        vc = vt_ref[0, cc]
        for h in range(GQA_GROUP):
            st_scr[nslot, h] = _dot(kc, qt_ref[0, h])
            st = st_scr[cslot, h]
            m_old = m_scr[h]
            m_new = jnp.maximum(m_old, jnp.max(st, axis=0, keepdims=True))
            p = jnp.exp2(st - m_new).astype(BF16)
            acc_scr[h] = jnp.exp2(m_old - m_new) * acc_scr[h] + _dot(vc, p)
            m_scr[h] = m_new

    def body(j, carry):
        c0 = 2 * j
        step(1, c0 + 1, 0, c0)
        step(0, c0 + 2, 1, c0 + 1)
        return carry
    lax.fori_loop(0, nk // 2 - 1, body, 0)
    step(1, nk - 1, 0, nk - 2)
    consume(1, nk - 1)

    outs = []
    for h in range(GQA_GROUP):
        acc = acc_scr[h]
        outs.append((acc[:HEAD_DIM] / acc[HEAD_DIM:HEAD_DIM + 1]).T)
    o_ref[0] = jnp.concatenate(outs, axis=-1).astype(o_ref.dtype)


def _attention(qt, k, vt):
    B, nk, tk, KW = k.shape
    S = nk * tk
    GW = GQA_GROUP * HEAD_DIM
    tq = min(TQ_ATTN, S)
    assert nk % 2 == 0
    scratch = [pltpu.VMEM((2, GQA_GROUP, tk, tq), F32), pltpu.VMEM((GQA_GROUP, 1, tq), F32),
               pltpu.VMEM((GQA_GROUP, V_AUG_ROWS, tq), F32)]
    return pl.pallas_call(
        functools.partial(_attn_kernel, nk=nk),
        scratch_shapes=scratch,
        grid=(B, KV_HEADS, S // tq),
        in_specs=[pl.BlockSpec((1, GQA_GROUP, KW, tq), lambda b, g, i: (b, g, 0, i)),
                  pl.BlockSpec((1, nk, tk, KW), lambda b, g, i: (b, 0, 0, 0)),
                  pl.BlockSpec((1, nk, V_AUG_ROWS, tk), lambda b, g, i: (b, 0, g, 0))],
        out_specs=pl.BlockSpec((1, tq, GW), lambda b, g, i: (b, i, g)),
        out_shape=jax.ShapeDtypeStruct((B, S, ATTN_HEADS * HEAD_DIM), BF16),
        compiler_params=_cparams(("parallel", "parallel", "parallel")),
        name="attention",
    )(qt, k, vt)


def _log1p(y):
    u = 1.0 + y
    return jnp.where(u == 1.0, y, jnp.log(u) * (y / (u - 1.0)))


def _softplus(z):
    return jnp.maximum(z, 0.0) + _log1p(jnp.exp(-jnp.abs(z)))


def _neg_expm1_2x(z, ez):
    return -jnp.tanh(z) * (ez * ez + 1.0)


def _gelu_tanh(x):
    return 0.5 * x * (1.0 + jnp.tanh(math.sqrt(2.0 / math.pi) * (x + 0.044715 * (x * x * x))))


def _lru_kernel(*refs, reverse, T):
    if reverse:
        (x_ref, xp_ref, xn_ref, cw_ref, cb_ref, wbd_ref, gb_ref, lam_ref, hf_ref, lg_ref,
         o_ref, carry_ref) = refs
    else:
        (x_ref, xp_ref, xn_ref, cw_ref, cb_ref, wbd_ref, gb_ref, lam_ref,
         o_ref, carry_ref) = refs
    i = pl.program_id(1)
    nt = pl.num_programs(1)
    tile = nt - 1 - i if reverse else i
    W = x_ref.shape[-1]

    @pl.when(i == 0)
    def _():
        carry_ref[...] = jnp.zeros_like(carry_ref)

    x = x_ref[0].astype(F32)
    prev = jnp.where(tile == 0, 0.0, xp_ref[0].astype(F32))
    nxt = jnp.where(tile == nt - 1, 0.0, xn_ref[0].astype(F32))
    row8 = lax.broadcasted_iota(jnp.int32, (V7X_SUBLANES, W), 0)

    def shifted(k):
        if k == 0:
            return x
        r = pltpu.roll(x, (-k) % T, axis=0)
        if k < 0:
            fill = pltpu.roll(prev, (-k) % V7X_SUBLANES, axis=0)
            head = jnp.where(row8 < -k, fill, r[:V7X_SUBLANES])
            return jnp.concatenate([head, r[V7X_SUBLANES:]], axis=0)
        fill = pltpu.roll(nxt, (-k) % V7X_SUBLANES, axis=0)
        tail = jnp.where(row8 >= V7X_SUBLANES - k, fill, r[T - V7X_SUBLANES:])
        return jnp.concatenate([r[:T - V7X_SUBLANES], tail], axis=0)

    cw = cw_ref[...]
    xc = cb_ref[...] + sum(shifted(j - CONV_LEFT) * cw[j:j + 1] for j in range(CONV_W))

    gates = _dot(xc.astype(BF16), wbd_ref[0]) + gb_ref[0]
    r = _sigmoid(gates[:, :W])
    ig = _sigmoid(gates[:, W:])
    log_a = (-LRU_C) * r * _softplus(-lam_ref[0])
    a = jnp.exp(log_a)
    u = jnp.sqrt(_neg_expm1_2x(log_a, a)) * (ig * xc)

    row = lax.broadcasted_iota(jnp.int32, (T, W), 0)

    def neighbour(z, d, fill):
        if d % V7X_SUBLANES == 0:
            pad = jnp.full((d, W), fill, F32)
            return jnp.concatenate([z[d:], pad] if reverse else [pad, z[:T - d]], axis=0)
        if reverse:
            return jnp.where(row < T - d, pltpu.roll(z, T - d, axis=0), fill)
        return jnp.where(row >= d, pltpu.roll(z, d, axis=0), fill)

    A, U = a, u
    d = 1
    while d < T:
        U = A * neighbour(U, d, 0.0) + U
        A = A * neighbour(A, d, 1.0)
        d *= 2
    h = A * carry_ref[...] + U
    if reverse:
        carry_ref[...] = h[0:1]
        o_ref[0] = ((hf_ref[0] + h) * lg_ref[0].astype(F32)).astype(o_ref.dtype)
    else:
        carry_ref[...] = h[T - 1:T]
        o_ref[0] = h


def _lru_pass(plru, conv_w, conv_b, wbd, gb, lam, hf, reverse):
    B, S, W2 = plru.shape
    W = W2 // 2
    T = min(T_LRU, S)
    nt = S // T
    r8 = T // V7X_SUBLANES
    nb8 = S // V7X_SUBLANES
    d = 1 if reverse else 0

    def tmap(i):
        return nt - 1 - i if reverse else i

    in_specs = [
        pl.BlockSpec((1, T, W), lambda b, i: (b, tmap(i), 0)),
        pl.BlockSpec((1, V7X_SUBLANES, W), lambda b, i: (b, jnp.maximum(tmap(i) * r8 - 1, 0), 0)),
        pl.BlockSpec((1, V7X_SUBLANES, W), lambda b, i: (b, jnp.minimum((tmap(i) + 1) * r8, nb8 - 1), 0)),
        pl.BlockSpec((CONV_W, W), lambda b, i: (0, 0)),
        pl.BlockSpec((1, W), lambda b, i: (0, 0)),
        pl.BlockSpec((1, W, 2 * W), lambda b, i: (d, 0, 0)),
        pl.BlockSpec((1, 1, 2 * W), lambda b, i: (d, 0, 0)),
        pl.BlockSpec((1, 1, W), lambda b, i: (d, 0, 0)),
    ]
    args = [plru, plru, plru, conv_w, conv_b, wbd, gb, lam]
    if reverse:
        in_specs += [pl.BlockSpec((1, T, W), lambda b, i: (b, tmap(i), 0)),
                     pl.BlockSpec((1, T, W), lambda b, i: (b, tmap(i), 1))]
        args += [hf, plru]
    return pl.pallas_call(
        functools.partial(_lru_kernel, reverse=reverse, T=T),
        grid=(B, nt),
        in_specs=in_specs,
        out_specs=pl.BlockSpec((1, T, W), lambda b, i: (b, tmap(i), 0)),
        out_shape=jax.ShapeDtypeStruct((B, S, W), BF16 if reverse else F32),
        scratch_shapes=[pltpu.VMEM((1, W), F32)],
        compiler_params=_cparams(("parallel", "arbitrary")),
        name="lru_bwd" if reverse else "lru_fwd",
    )(*args)


def _hgrn_consts(C, reverse):
    nl = int(round(math.log2(C)))
    t = np.arange(C)
    halves = [C >> (lvl + 1) for lvl in range(nl)]
    a_rows, masks, upper = [], [], []
    for m in halves:
        blk = t // (2 * m)
        up = (t % (2 * m)) >= m
        mid = blk * 2 * m + m
        r = t[None, :]
        masks.append((blk[:, None] == blk[None, :]) & up[:, None] & (~up)[None, :])
        if m < V7X_SUBLANES:
            a_up = (r >= mid[:, None]) & (r <= t[:, None])
            a_lo = (r > t[:, None]) & (r <= mid[:, None] - 1)
            a_rows.append(np.where(up[:, None], a_up, a_lo))
            upper.append(up)
    a_rows.append(t[None, :] <= t[:, None])
    a_rows.append(np.ones((V7X_SUBLANES, C), bool))
    masks.append(np.eye(C, dtype=bool))
    if reverse:
        a_rows = [a[::-1, ::-1] if a.shape[0] == C else a for a in a_rows]
        masks = [mm[::-1, ::-1] for mm in masks]
        upper = [u[::-1] for u in upper]
    amat = np.concatenate(a_rows, 0).astype(np.float32)
    msk = np.stack(masks, 0).astype(np.float32)
    upv = np.stack(upper, 0).astype(np.float32)[:, :, None]
    return halves, amat, msk, upv


def _hgrn_kernel(*refs, layer, reverse, C, CB, halves):
    if reverse:
        (hq_ref, hf_ref, hi_ref, lbl_ref, amat_ref, msk_ref, up_ref, of_ref, hg_ref, gain_ref,
         o_ref, st_ref) = refs
    else:
        (hq_ref, hf_ref, hi_ref, lbl_ref, amat_ref, msk_ref, up_ref, o_ref, st_ref) = refs
    H = HGRN_HEADS
    W = hq_ref.shape[-1]
    dk = W // H
    NL = len(halves)
    n_small = sum(1 for m in halves if m < V7X_SUBLANES)

    @pl.when(pl.program_id(1) == 0)
    def _():
        st_ref[...] = jnp.zeros_like(st_ref)

    lgt = lbl_ref[...]
    e = jnp.exp(lgt - jnp.max(lgt, axis=0, keepdims=True))
    p = e / jnp.sum(e, axis=0, keepdims=True)
    lb = jnp.zeros((1, W), F32)
    for j in range(1, layer + 1):
        lb = lb + p[j:j + 1]

    def gates_and_factors(j):
        rows = slice(j * C, (j + 1) * C)
        q = hq_ref[0, rows].astype(F32)
        f = lb + (1.0 - lb) * _sigmoid(hf_ref[0, rows].astype(F32))
        k = 1.0 - f
        g = jnp.log2(f)
        v = hi_ref[0, rows].astype(F32)

        sums = _dot_sel_x(amat_ref[...], g)
        b = sums[n_small * C:(n_small + 1) * C]
        tot = sums[(n_small + 1) * C:(n_small + 1) * C + 1]
        xs = []
        small = 0
        for m in halves:
            if m >= V7X_SUBLANES:
                shp = (C // (2 * m), 2 * m, W)
                b3 = b.reshape(shp)
                r0 = m if reverse else m - 1
                ref = b3[:, r0:r0 + 1, :]
                rowi = lax.broadcasted_iota(jnp.int32, shp, 1)
                qside = (rowi < m) if reverse else (rowi >= m)
                x3 = jnp.where(qside, q.reshape(shp), k.reshape(shp)) * jnp.exp2(
                    jnp.where(qside, b3 - ref, ref - b3))
                xs.append(x3.reshape(C, W).astype(BF16))
            else:
                xs.append((jnp.where(up_ref[small] > 0.5, q, k)
                           * jnp.exp2(sums[small * C:(small + 1) * C])).astype(BF16))
                small += 1
        return dict(rows=rows, v=v.astype(BF16), vt=v.T.astype(BF16), xs=xs, qk=q * k,
                    qb=(q * jnp.exp2(b)).astype(BF16), kr=(k * jnp.exp2(tot - b)).astype(BF16),
                    ex_tot=jnp.exp2(tot))

    def scores_and_state(a):
        scs, inters = [], []
        for h in range(H):
            sl = slice(h * dk, (h + 1) * dk)
            parts = [_dot_nt(a['xs'][l][:, sl], a['xs'][l][:, sl]) for l in range(NL)]
            st = st_ref[h]
            inters.append(_dot_nt(a['qb'][:, sl], st.astype(BF16)))
            st_ref[h] = st * a['ex_tot'][:, sl] + _dot(a['vt'][sl, :], a['kr'][:, sl])
            sc = msk_ref[NL] * jnp.sum(a['qk'][:, sl], axis=-1, keepdims=True)
            for l in range(NL):
                sc = sc + msk_ref[l] * parts[l]
            scs.append(sc.astype(BF16))
        return scs, inters

    def outputs(a, scs, inters):
        rows = a['rows']
        if reverse:
            of = of_ref[0, rows]
            hg = hg_ref[0, rows].astype(F32)
            gain = gain_ref[...]
        for h in range(H):
            sl = slice(h * dk, (h + 1) * dk)
            o_h = inters[h] + _dot(scs[h], a['v'][:, sl])
            if reverse:
                tot_o = of[:, sl] + o_h
                ms = jnp.mean(tot_o * tot_o, axis=-1, keepdims=True)
                o_ref[0, rows, sl] = (tot_o * lax.rsqrt(ms + RMS_EPS) * gain
                                      * hg[:, sl]).astype(o_ref.dtype)
            else:
                o_ref[0, rows, sl] = o_h

    order = list(range(CB - 1, -1, -1) if reverse else range(CB))
    cur = gates_and_factors(order[0])
    for n, j in enumerate(order):
        scs, inters = scores_and_state(cur)
        nxt = gates_and_factors(order[n + 1]) if n + 1 < CB else None
        outputs(cur, scs, inters)
        cur = nxt


def _hgrn_pass(phg, lb_logits, norm_gain, of, layer, reverse):
    B, S, W5 = phg.shape
    W = W5 // 5
    C = min(C_HGRN, S)
    CB = min(CB_HGRN, S // C)
    TB = CB * C
    nc = S // TB
    halves, amat, msk, upv = _hgrn_consts(C, reverse)
    L = lb_logits.shape[0]
    dk = W // HGRN_HEADS

    def cmap(i):
        return nc - 1 - i if reverse else i

    fcol = 2 if reverse else 1
    in_specs = [
        pl.BlockSpec((1, TB, W), lambda b, i: (b, cmap(i), 0)),
        pl.BlockSpec((1, TB, W), lambda b, i: (b, cmap(i), fcol)),
        pl.BlockSpec((1, TB, W), lambda b, i: (b, cmap(i), 3)),
        pl.BlockSpec((L, W), lambda b, i: (0, 0)),
        pl.BlockSpec(amat.shape, lambda b, i: (0, 0)),
        pl.BlockSpec(msk.shape, lambda b, i: (0, 0, 0)),
        pl.BlockSpec(upv.shape, lambda b, i: (0, 0, 0)),
    ]
    args = [phg, phg, phg, lb_logits, jnp.asarray(amat, BF16), jnp.asarray(msk, F32), jnp.asarray(upv, F32)]
    if reverse:
        in_specs += [pl.BlockSpec((1, TB, W), lambda b, i: (b, cmap(i), 0)),
                     pl.BlockSpec((1, TB, W), lambda b, i: (b, cmap(i), 4)),
                     pl.BlockSpec((1, dk), lambda b, i: (0, 0))]
        args += [of, phg, norm_gain.reshape(1, dk)]
    return pl.pallas_call(
        functools.partial(_hgrn_kernel, layer=layer, reverse=reverse, C=C, CB=CB, halves=tuple(halves)),
        grid=(B, nc),
        in_specs=in_specs,
        out_specs=pl.BlockSpec((1, TB, W), lambda b, i: (b, cmap(i), 0)),
        out_shape=jax.ShapeDtypeStruct((B, S, W), BF16 if reverse else F32),
        scratch_shapes=[pltpu.VMEM((HGRN_HEADS, dk, dk), F32)],
        compiler_params=_cparams(("parallel", "arbitrary")),
        name="hgrn_bwd" if reverse else "hgrn_fwd",
    )(*args)


def _merge_kernel(ya_ref, yl_ref, yh_ref, g0_ref, g1_ref, g2_ref, x_ref, mod_ref,
                  wa_ref, wl_ref, wh_ref, wo_ref, lng_ref, lnb_ref, wr_ref,
                  x1_ref, h2_ref, route_ref, *, alpha, n_groups, e_per):
    m = mod_ref[0]
    half = x_ref.shape[1] // 2
    halves = [slice(r0, r0 + half) for r0 in (0, half)]

    branch = [(_dot(ya_ref[0, rs], wa_ref[...]), _dot(yl_ref[0, rs], wl_ref[...]),
               _dot(yh_ref[0, rs], wh_ref[...])) for rs in halves]
    mixes = []
    for rs, (pa, pl_, ph) in zip(halves, branch):
        merged = (_sigmoid(g0_ref[0, rs].astype(F32)) * pa + _sigmoid(g1_ref[0, rs].astype(F32)) * pl_
                  + _sigmoid(g2_ref[0, rs].astype(F32)) * ph)
        mixes.append(_dot(merged.astype(BF16), wo_ref[...]))
    logit_halves = []
    for rs, mix in zip(halves, mixes):
        x1 = _layernorm(alpha * x_ref[0, rs] + (1.0 + m[2:3]) * mix, lng_ref[...], lnb_ref[...])
        x1_ref[0, rs] = x1
        h2 = x1 * (1.0 + m[4:5]) + m[3:4]
        h2_ref[0, rs] = h2
        xh, xl = _split2(h2)
        r = _dot(xh, wr_ref[...])
        logit_halves.append(r[:, :V7X_LANES] + r[:, V7X_LANES:] + _dot(xl, wr_ref[:, :V7X_LANES]))
    for rs, logits in zip(halves, logit_halves):
        route_ref[0, rs] = _route(logits, n_groups, e_per)


def _route(logits, n_groups, e_per):
    lane = lax.broadcasted_iota(jnp.int32, logits.shape, 1)
    big = jnp.int32(V7X_LANES)
    neg = jnp.float32(-jnp.inf)
    gl = jnp.where(lane < n_groups, logits, neg)
    gmax = jnp.max(gl, axis=-1, keepdims=True)
    gsel = jnp.min(jnp.where(gl == gmax, lane, big), axis=-1, keepdims=True)
    gp = 1.0 / jnp.sum(jnp.exp(gl - gmax), axis=-1, keepdims=True)
    lo = n_groups + gsel * e_per
    el = jnp.where((lane >= lo) & (lane < lo + e_per), logits, neg)
    m1 = jnp.max(el, axis=-1, keepdims=True)
    i1 = jnp.min(jnp.where(el == m1, lane, big), axis=-1, keepdims=True)
    el2 = jnp.where(lane == i1, neg, el)
    m2 = jnp.max(el2, axis=-1, keepdims=True)
    i2 = jnp.min(jnp.where(el2 == m2, lane, big), axis=-1, keepdims=True)
    z = jnp.sum(jnp.exp(el - m1), axis=-1, keepdims=True)
    p1 = 1.0 / z
    p2 = jnp.exp(m2 - m1) / z
    w1 = gp * (p1 / (p1 + p2))
    w2 = gp * (p2 / (p1 + p2))
    e1 = (i1 - n_groups).astype(F32)
    e2 = (i2 - n_groups).astype(F32)
    return jnp.where(lane == 0, w1, jnp.where(lane == 1, w2,
                     jnp.where(lane == 2, e1, jnp.where(lane == 3, e2, 0.0))))


def _merge(ya, yl, yh, pmg, x, mod_l, wa, wl, wh, wo, lng, lnb, wr, n_groups, e_per, alpha):
    B, S, D = x.shape
    tm = min(TM_MERGE, S)
    bw = ya.shape[-1]
    tok = lambda b, i: (b, i, 0)
    const = lambda b, i: (0, 0)
    return pl.pallas_call(
        functools.partial(_merge_kernel, alpha=alpha, n_groups=n_groups, e_per=e_per),
        grid=(B, S // tm),
        in_specs=[pl.BlockSpec((1, tm, bw), tok), pl.BlockSpec((1, tm, bw), tok), pl.BlockSpec((1, tm, bw), tok),
                  pl.BlockSpec((1, tm, D), lambda b, i: (b, i, 0)),
                  pl.BlockSpec((1, tm, D), lambda b, i: (b, i, 1)),
                  pl.BlockSpec((1, tm, D), lambda b, i: (b, i, 2)),
                  pl.BlockSpec((1, tm, D), tok),
                  pl.BlockSpec((1, 6, D), lambda b, i: (b, 0, 0)),
                  pl.BlockSpec((bw, D), const), pl.BlockSpec((bw, D), const), pl.BlockSpec((bw, D), const),
                  pl.BlockSpec((D, D), const),
                  pl.BlockSpec((1, D), const), pl.BlockSpec((1, D), const),
                  pl.BlockSpec((D, 2 * V7X_LANES), const)],
        out_specs=[pl.BlockSpec((1, tm, D), tok), pl.BlockSpec((1, tm, D), tok),
                   pl.BlockSpec((1, tm, V7X_LANES), tok)],
        out_shape=[jax.ShapeDtypeStruct((B, S, D), F32), jax.ShapeDtypeStruct((B, S, D), F32),
                   jax.ShapeDtypeStruct((B, S, V7X_LANES), F32)],
        compiler_params=_cparams(("parallel", "parallel")),
        name="merge",
    )(ya, yl, yh, pmg, pmg, pmg, x, mod_l, wa, wl, wh, wo, lng, lnb, wr)


def _dispatch_kernel(dest_ref, h_ref, xs_in, xs_hbm, sem, *, TM):
    del xs_in

    for r in range(TM):
        for a in range(TOP_K):
            pltpu.make_async_copy(h_ref.at[pl.ds(r, 1)],
                                  xs_hbm.at[pl.ds(dest_ref[0, 0, TOP_K * r + a], 1)],
                                  sem).start(priority=a % 2)
    for a in range(TOP_K):
        pltpu.make_async_copy(h_ref, xs_hbm.at[pl.ds(0, TM)], sem).wait()


def _dispatch(h2, dest, xs_prev, S):
    N, D = h2.shape
    TM = min(TM_DISPATCH, S)
    nt = N // TM
    return pl.pallas_call(
        functools.partial(_dispatch_kernel, TM=TM),
        grid=(nt,),
        in_specs=[pl.BlockSpec((1, 1, TOP_K * TM), lambda i: (i, 0, 0), memory_space=pltpu.SMEM),
                  pl.BlockSpec((TM, D), lambda i: (i, 0)),
                  pl.BlockSpec(memory_space=pl.ANY)],
        out_specs=pl.BlockSpec(memory_space=pl.ANY),
        out_shape=jax.ShapeDtypeStruct(xs_prev.shape, xs_prev.dtype),
        scratch_shapes=[pltpu.SemaphoreType.DMA(())],
        input_output_aliases={2: 0},
        compiler_params=_cparams(("arbitrary",)),
        name="moe_dispatch",
    )(dest.reshape(nt, 1, TOP_K * TM), h2, xs_prev)


def _expert_kernel(blk_e_ref, nused_ref, xs_ref, w13_ref, w2_ref, ys_ref, *, F):
    i = pl.program_id(0)

    @pl.when(i < nused_ref[0])
    def _():
        half = xs_ref.shape[0] // 2
        abs_ = [_dot(xs_ref[r0:r0 + half, :].astype(BF16), w13_ref[0]) for r0 in (0, half)]
        for n, r0 in enumerate((0, half)):
            hmid = (_silu(abs_[n][:, :F]) * abs_[n][:, F:]).astype(BF16)
            ys_ref[r0:r0 + half, :] = _dot(hmid, w2_ref[0])

    @pl.when(i >= nused_ref[0])
    def _():
        ys_ref[...] = jnp.zeros_like(ys_ref)


def _experts(xs, blk_e, nused, w13, w2):
    P, D = xs.shape
    R = R_MOE
    nb = P // R
    F = w2.shape[1]
    grid_spec = pltpu.PrefetchScalarGridSpec(
        num_scalar_prefetch=2,
        grid=(nb,),
        in_specs=[pl.BlockSpec((R, D), lambda i, be, nu: (i, 0)),
                  pl.BlockSpec((1, D, 2 * F), lambda i, be, nu: (be[i], 0, 0)),
                  pl.BlockSpec((1, F, D), lambda i, be, nu: (be[i], 0, 0))],
        out_specs=pl.BlockSpec((R, D), lambda i, be, nu: (i, 0)),
    )
    return pl.pallas_call(
        functools.partial(_expert_kernel, F=F),
        grid_spec=grid_spec,
        out_shape=jax.ShapeDtypeStruct((P, D), F32),
        compiler_params=_cparams(("arbitrary",)),
        name="moe_experts",
    )(blk_e, nused, xs, w13, w2)


def _combine_kernel(pos_ref, posn_ref, ys_hbm, route_ref, x_ref, mod_ref, lng_ref, lnb_ref, o_ref,
                    ybuf, sems, *, TC, alpha):
    i = pl.program_id(0)
    n = pl.num_programs(0)
    slot = i % 2

    def issue(p_ref, s):
        for r in range(TC):
            for a in range(TOP_K):
                pltpu.make_async_copy(ys_hbm.at[pl.ds(p_ref[0, 0, TOP_K * r + a], 1)],
                                      ybuf.at[s, a, pl.ds(r, 1)], sems.at[s]).start(priority=a % 2)

    @pl.when(i == 0)
    def _():
        issue(pos_ref, 0)

    for s in range(2):
        @pl.when((i + 1 < n) & (slot == 1 - s))
        def _(s=s):
            issue(posn_ref, s)

    for a in range(TOP_K):
        pltpu.make_async_copy(ys_hbm.at[pl.ds(0, TC)], ybuf.at[slot, a], sems.at[slot]).wait()

    m = mod_ref[0]
    w = route_ref[...]
    ffn = w[:, 0:1] * ybuf[slot, 0]
    for a in range(1, TOP_K):
        ffn = ffn + w[:, a:a + 1] * ybuf[slot, a]
    o_ref[...] = _layernorm(alpha * x_ref[...] + (1.0 + m[5:6]) * ffn, lng_ref[...], lnb_ref[...])


def _combine(ys, dest, route, x1, mod_l, lng, lnb, S, alpha):
    N, D = x1.shape
    TC = min(TC_MOE, S)
    nt = N // TC
    per_b = S // TC
    pos = dest.reshape(nt, 1, TOP_K * TC)
    return pl.pallas_call(
        functools.partial(_combine_kernel, TC=TC, alpha=alpha),
        grid=(nt,),
        in_specs=[pl.BlockSpec((1, 1, TOP_K * TC), lambda i: (i, 0, 0), memory_space=pltpu.SMEM),
                  pl.BlockSpec((1, 1, TOP_K * TC), lambda i: (jnp.minimum(i + 1, nt - 1), 0, 0),
                               memory_space=pltpu.SMEM),
                  pl.BlockSpec(memory_space=pl.ANY),
                  pl.BlockSpec((TC, V7X_LANES), lambda i: (i, 0)),
                  pl.BlockSpec((TC, D), lambda i: (i, 0)),
                  pl.BlockSpec((1, 6, D), lambda i: (i // per_b, 0, 0)),
                  pl.BlockSpec((1, D), lambda i: (0, 0)),
                  pl.BlockSpec((1, D), lambda i: (0, 0))],
        out_specs=pl.BlockSpec((TC, D), lambda i: (i, 0)),
        out_shape=jax.ShapeDtypeStruct((N, D), F32),
        scratch_shapes=[pltpu.VMEM((2, TOP_K, TC, D), F32), pltpu.SemaphoreType.DMA((2,))],
        compiler_params=_cparams(("arbitrary",)),
        name="moe_combine",
    )(pos, pos, ys, route, x1, mod_l, lng, lnb)


def _dispatch_plan(route, n_experts, R):
    eid = route[:, TOP_K:2 * TOP_K].astype(jnp.int32)
    hot = (eid[:, :, None] == jnp.arange(n_experts, dtype=jnp.int32)).astype(jnp.int32).sum(1)
    csum = jnp.cumsum(hot, axis=0)
    counts = csum[-1]
    rank = jnp.take_along_axis(csum - hot, eid, axis=1)
    pcounts = (counts + R - 1) // R * R
    pends = jnp.cumsum(pcounts)
    pstarts = pends - pcounts
    dest = (pstarts[eid] + rank).astype(jnp.int32)
    nb = (eid.shape[0] * TOP_K + n_experts * R) // R
    starts = jnp.arange(nb, dtype=jnp.int32) * R
    blk_e = jnp.minimum((pends[None, :] <= starts[:, None]).astype(jnp.int32).sum(1), n_experts - 1)
    nused = (pends[-1:] // R).astype(jnp.int32)
    return blk_e, nused, dest


def _prep_weights(p):
    L = p['w_in'].shape[0]
    W = p['lru_conv_w'].shape[-1]
    nblk, bs = p['lru_wa'].shape[2], p['lru_wa'].shape[3]

    def blockdiag(w):
        eye = jnp.eye(nblk, dtype=w.dtype)
        return jnp.einsum('ldnij,nm->ldnimj', w, eye).reshape(L, 2, W, W)

    G, E = p['router_expert_w'].shape[1], p['router_expert_w'].shape[3]
    D = p['w_in'].shape[1]
    wr = jnp.concatenate([p['router_group_w'],
                          p['router_expert_w'].transpose(0, 2, 1, 3).reshape(L, D, G * E)], -1)
    wr = jnp.pad(wr, ((0, 0), (0, 0), (0, V7X_LANES - wr.shape[-1])))
    wr_hi = wr.astype(BF16)
    wr = jnp.concatenate([wr_hi, (wr - wr_hi.astype(F32)).astype(BF16)], -1)
    return dict(
        w_in=p['w_in'].astype(BF16),
        gq=jnp.tile(p['attn_q_gain'], (1, ATTN_HEADS))[:, None, :],
        gk=jnp.tile(p['attn_k_gain'], (1, KV_HEADS))[:, None, :],
        w_attn_o=p['w_attn_o'].astype(BF16),
        conv_w=p['lru_conv_w'], conv_b=p['lru_conv_b'][:, None, :],
        wbd=jnp.concatenate([blockdiag(p['lru_wa']), blockdiag(p['lru_wi'])], -1).astype(BF16),
        gb=jnp.concatenate([p['lru_ba'], p['lru_bi']], -1)[:, :, None, :],
        lam=p['lru_lambda'][:, :, None, :],
        w_lru_o=p['w_lru_o'].astype(BF16),
        w_hgrn_o=p['w_hgrn_o'].astype(BF16),
        w_out=p['w_out'].astype(BF16),
        ln1_g=p['ln1_g'][:, None, :], ln1_b=p['ln1_b'][:, None, :],
        ln2_g=p['ln2_g'][:, None, :], ln2_b=p['ln2_b'][:, None, :],
        wr=wr, n_groups=G, e_per=E,
        w13=p['expert_w13'].astype(BF16), w2=p['expert_w2'].astype(BF16),
    )


def _trunk(x, c, p, wp):
    B, S, D = x.shape
    L = p['w_in'].shape[0]
    alpha = (2 * L) ** 0.25
    QW = ATTN_HEADS * HEAD_DIM
    KW = KV_HEADS * HEAD_DIM
    LW = p['lru_conv_w'].shape[-1]
    HW = p['hgrn_lb_logits'].shape[-1]
    widths = (QW, 2 * KW, 2 * LW, 5 * HW, N_BRANCH * D)
    assert sum(widths) == p['w_in'].shape[-1]
    acts = (((0, None),), ((0, None),), ((0, None), (LW, _gelu_tanh)),
            ((0, _silu), (HW, None), (4 * HW, _silu)), ((0, None),))
    n_experts = wp['n_groups'] * wp['e_per']

    mod = _ada_mod(c, p['ada_w'], p['ada_b'])
    cos_t, sin_t = _rope_tables(S)
    bind, psw = _head_consts()
    cos_t, sin_t = jnp.asarray(cos_t), jnp.asarray(sin_t)
    bind, psw = jnp.asarray(bind, BF16), jnp.asarray(psw, BF16)

    xs = None
    for l in range(L):
        pq, pkv, plru, phg, pmg = _inproj(x, mod[l], wp['w_in'][l], widths, acts)
        qt, k, vt = _attn_prep(pq, pkv, cos_t, sin_t, wp['gq'][l], wp['gk'][l], bind, psw)
        ya = _attention(qt, k, vt)
        lru_args = (plru, wp['conv_w'][l], wp['conv_b'][l], wp['wbd'][l], wp['gb'][l], wp['lam'][l])
        hf = _lru_pass(*lru_args, None, reverse=False)
        yl = _lru_pass(*lru_args, hf, reverse=True)
        of = _hgrn_pass(phg, p['hgrn_lb_logits'], p['hgrn_norm_gain'][l], None, l, reverse=False)
        yh = _hgrn_pass(phg, p['hgrn_lb_logits'], p['hgrn_norm_gain'][l], of, l, reverse=True)
        x1, h2, route = _merge(ya, yl, yh, pmg, x, mod[l], wp['w_attn_o'][l], wp['w_lru_o'][l],
                               wp['w_hgrn_o'][l], wp['w_out'][l], wp['ln1_g'][l], wp['ln1_b'][l],
                               wp['wr'][l], wp['n_groups'], wp['e_per'], alpha)
        N = B * S
        route = route.reshape(N, V7X_LANES)
        blk_e, nused, dest = _dispatch_plan(route, n_experts, R_MOE)
        if xs is None:
            xs = jnp.zeros((N * TOP_K + n_experts * R_MOE, D), F32)
        xs = _dispatch(h2.reshape(N, D), dest, xs, S)
        ys = _experts(xs, blk_e, nused, wp['w13'][l], wp['w2'][l])
        x = _combine(ys, dest, route, x1.reshape(N, D), mod[l], wp['ln2_g'][l], wp['ln2_b'][l], S,
                     alpha).reshape(B, S, D)
    return x


def kernel(x_prompt, x_sample, c_prompt, c_sample, ada_w, ada_b, w_in, attn_q_gain, attn_k_gain, w_attn_o, lru_conv_w, lru_conv_b, lru_wa, lru_ba, lru_wi, lru_bi, lru_lambda, w_lru_o, hgrn_lb_logits, hgrn_norm_gain, w_hgrn_o, w_out, ln1_g, ln1_b, router_group_w, router_expert_w, expert_w13, expert_w2, ln2_g, ln2_b):
    p = dict(ada_w=ada_w, ada_b=ada_b, w_in=w_in, attn_q_gain=attn_q_gain, attn_k_gain=attn_k_gain,
             w_attn_o=w_attn_o, lru_conv_w=lru_conv_w, lru_conv_b=lru_conv_b, lru_wa=lru_wa, lru_ba=lru_ba,
             lru_wi=lru_wi, lru_bi=lru_bi, lru_lambda=lru_lambda, w_lru_o=w_lru_o,
             hgrn_lb_logits=hgrn_lb_logits, hgrn_norm_gain=hgrn_norm_gain, w_hgrn_o=w_hgrn_o, w_out=w_out,
             ln1_g=ln1_g, ln1_b=ln1_b, router_group_w=router_group_w, router_expert_w=router_expert_w,
             expert_w13=expert_w13, expert_w2=expert_w2, ln2_g=ln2_g, ln2_b=ln2_b)
    wp = _prep_weights(p)
    return (_trunk(x_prompt, c_prompt, p, wp), _trunk(x_sample, c_sample, p, wp))
```

```python
import functools
import math

import numpy as np
import jax
import jax.numpy as jnp
from jax import lax
from jax.experimental import pallas as pl
from jax.experimental.pallas import tpu as pltpu

F32 = jnp.float32
BF16 = jnp.bfloat16

GRID_W = 64
CONV_W = 4
CONV_LEFT = 2
LRU_C = 8.0
ATTN_HEADS = 8
KV_HEADS = 2
HEAD_DIM = 64
GQA_GROUP = ATTN_HEADS // KV_HEADS
AXIAL_DIM = HEAD_DIM // 2
AXIAL_FREQS = AXIAL_DIM // 2
ROPE_THETA = 10000.0
HGRN_HEADS = 4
N_BRANCH = 3
TOP_K = 2
LN_EPS = 1e-5
RMS_EPS = 1e-6
LOG2_E = 1.4426950408889634

V7X_LANES = 128
V7X_SUBLANES = 8
V7X_VMEM_LIMIT_BYTES = 56 * 1024 * 1024

TM_INPROJ = 512
V_AUG_ROWS = HEAD_DIM + 16
TQ_ATTN = 512
T_LRU = 256
C_HGRN = 128
CB_HGRN = 4
TK_ATTN = 512
TM_MERGE = 512
R_MOE = 512
TM_DISPATCH = 256
TC_MOE = 256


def _cparams(sem):
    return pltpu.CompilerParams(dimension_semantics=sem, vmem_limit_bytes=V7X_VMEM_LIMIT_BYTES)


def _dot(a, b):
    return jnp.dot(a, b, preferred_element_type=F32)


def _dot_nt(a, b):
    return lax.dot_general(a, b, (((1,), (1,)), ((), ())), preferred_element_type=F32)


def _dot_hi(a, b):
    return jnp.dot(a, b, preferred_element_type=F32, precision=lax.Precision.HIGHEST)


def _split2(x):
    hi = x.astype(BF16)
    lo = (x - hi.astype(F32)).astype(BF16)
    return hi, lo


def _dot_x_sel(x, m):
    hi, lo = _split2(x)
    return _dot(hi, m) + _dot(lo, m)


def _dot_sel_x(m, x):
    hi, lo = _split2(x)
    return _dot(m, hi) + _dot(m, lo)


def _sigmoid(x):
    return 1.0 / (1.0 + jnp.exp(-x))


def _silu(x):
    return x * _sigmoid(x)


def _layernorm(y, g, b):
    mu = jnp.mean(y, axis=-1, keepdims=True)
    d = y - mu
    var = jnp.mean(d * d, axis=-1, keepdims=True)
    return d * lax.rsqrt(var + LN_EPS) * g + b


def _ada_kernel(c_ref, w_ref, b_ref, o_ref):
    c = c_ref[...]
    o_ref[0] = _dot_hi(_silu(c), w_ref[0]) + b_ref[0]


def _ada_mod(c, ada_w, ada_b):
    L, D, N6 = ada_w.shape
    B = c.shape[0]
    tn = N6 // 6
    out = pl.pallas_call(
        _ada_kernel,
        grid=(L, N6 // tn),
        in_specs=[pl.BlockSpec((B, D), lambda l, j: (0, 0)),
                  pl.BlockSpec((1, D, tn), lambda l, j: (l, 0, j)),
                  pl.BlockSpec((1, 1, tn), lambda l, j: (l, 0, j))],
        out_specs=pl.BlockSpec((1, B, tn), lambda l, j: (l, 0, j)),
        out_shape=jax.ShapeDtypeStruct((L, B, N6), F32),
        compiler_params=_cparams(("arbitrary", "arbitrary")),
        name="ada_mod",
    )(c, ada_w, ada_b.reshape(L, 1, N6))
    return out.reshape(L, B, 6, N6 // 6)


def _inproj_kernel(x_ref, mod_ref, w_ref, *o_refs, chunk, acts):
    m = mod_ref[0]
    h = (x_ref[0] * (1.0 + m[1:2]) + m[0:1]).astype(BF16)
    off = 0
    for o_ref, segs in zip(o_refs, acts):
        n = o_ref.shape[-1]
        for c0 in range(0, n, chunk):
            c1 = min(c0 + chunk, n)
            act = [f for start, f in segs if start <= c0][-1]
            y = _dot(h, w_ref[:, off + c0:off + c1])
            o_ref[0, :, c0:c1] = (y if act is None else act(y)).astype(o_ref.dtype)
        off += n


def _inproj(x, mod_l, w_in_bf, widths, acts):
    B, S, D = x.shape
    N = w_in_bf.shape[1]
    tm = min(TM_INPROJ, S)
    out_shape = [jax.ShapeDtypeStruct((B, S, n), BF16) for n in widths]
    out_specs = [pl.BlockSpec((1, tm, n), lambda b, i: (b, i, 0)) for n in widths]
    return pl.pallas_call(
        functools.partial(_inproj_kernel, chunk=512, acts=acts),
        grid=(B, S // tm),
        in_specs=[pl.BlockSpec((1, tm, D), lambda b, i: (b, i, 0)),
                  pl.BlockSpec((1, 6, D), lambda b, i: (b, 0, 0)),
                  pl.BlockSpec((D, N), lambda b, i: (0, 0), pipeline_mode=pl.Buffered(1))],
        out_specs=out_specs,
        out_shape=out_shape,
        compiler_params=_cparams(("parallel", "parallel")),
        name="in_proj",
    )(x, mod_l, w_in_bf)


def _rope_tables(S):
    t = np.arange(S)
    row = (t // GRID_W).astype(np.float64)
    col = (t % GRID_W).astype(np.float64)
    inv = ROPE_THETA ** (-np.arange(AXIAL_FREQS, dtype=np.float64) / AXIAL_FREQS)
    ang_r = (row[:, None].astype(np.float32) * inv.astype(np.float32)[None, :]).astype(np.float32)
    ang_c = (col[:, None].astype(np.float32) * inv.astype(np.float32)[None, :]).astype(np.float32)
    cos_h = np.concatenate([np.cos(ang_r), np.cos(ang_r), np.cos(ang_c), np.cos(ang_c)], -1)
    sin_h = np.concatenate([np.sin(ang_r), np.sin(ang_r), np.sin(ang_c), np.sin(ang_c)], -1)
    cos_t = np.tile(cos_h, (1, ATTN_HEADS)).astype(np.float32)
    sin_t = np.tile(sin_h, (1, ATTN_HEADS)).astype(np.float32)
    return cos_t, sin_t


def _head_consts():
    W = ATTN_HEADS * HEAD_DIM
    d = np.arange(W)
    bind = (d[:, None] // HEAD_DIM == d[None, :] // HEAD_DIM).astype(np.float32) / HEAD_DIM
    psw = np.zeros((W, W), np.float32)
    lowhalf = (d % AXIAL_DIM) < AXIAL_FREQS
    for j in d:
        if lowhalf[j]:
            psw[j + AXIAL_FREQS, j] = -1.0
        else:
            psw[j - AXIAL_FREQS, j] = 1.0
    return bind, psw


def _prep_kernel(pq_ref, pkv_ref, cos_ref, sin_ref, gq_ref, gk_ref, bind_ref, psw_ref,
                 qt_ref, k_ref, vt_ref):
    KW = KV_HEADS * HEAD_DIM
    cos = cos_ref[...]
    sin = sin_ref[...]

    def norm_rope(u, gain, w):
        ms = _dot_x_sel(u * u, bind_ref[:w, :w])
        un = u * lax.rsqrt(ms + RMS_EPS) * gain
        return un * cos[:, :w] + _dot_x_sel(un, psw_ref[:w, :w]) * sin[:, :w]

    q = norm_rope(pq_ref[0].astype(F32), gq_ref[...], ATTN_HEADS * HEAD_DIM)
    qt = (q * (HEAD_DIM ** -0.5 * LOG2_E)).T
    zeros = jnp.zeros((HEAD_DIM, qt.shape[1]), F32)
    for h in range(ATTN_HEADS):
        rows = [zeros] * KV_HEADS
        rows[h // GQA_GROUP] = qt[h * HEAD_DIM:(h + 1) * HEAD_DIM]
        qt_ref[0, h] = jnp.concatenate(rows, axis=0).astype(BF16)
    kv = pkv_ref[0]
    k_ref[0, 0] = norm_rope(kv[:, :KW].astype(F32), gk_ref[...], KW).astype(BF16)
    vt = kv[:, KW:].astype(F32).T
    tm = vt.shape[1]
    extra = (lax.broadcasted_iota(jnp.int32, (V_AUG_ROWS - HEAD_DIM, tm), 0) == 0).astype(F32)
    vt_ref[0, 0] = jnp.concatenate(
        [blk for g in range(KV_HEADS) for blk in (vt[g * HEAD_DIM:(g + 1) * HEAD_DIM], extra)],
        axis=0).astype(BF16)


def _attn_prep(pq, pkv, cos_t, sin_t, gq, gk, bind, psw):
    B, S, QW = pq.shape
    KW = KV_HEADS * HEAD_DIM
    tm = min(TK_ATTN, S)
    return pl.pallas_call(
        _prep_kernel,
        grid=(B, S // tm),
        in_specs=[pl.BlockSpec((1, tm, QW), lambda b, i: (b, i, 0)),
                  pl.BlockSpec((1, tm, 2 * KW), lambda b, i: (b, i, 0)),
                  pl.BlockSpec((tm, QW), lambda b, i: (i, 0)),
                  pl.BlockSpec((tm, QW), lambda b, i: (i, 0)),
                  pl.BlockSpec((1, QW), lambda b, i: (0, 0)),
                  pl.BlockSpec((1, KW), lambda b, i: (0, 0)),
                  pl.BlockSpec((QW, QW), lambda b, i: (0, 0)),
                  pl.BlockSpec((QW, QW), lambda b, i: (0, 0))],
        out_specs=[pl.BlockSpec((1, ATTN_HEADS, KW, tm), lambda b, i: (b, 0, 0, i)),
                   pl.BlockSpec((1, 1, tm, KW), lambda b, i: (b, i, 0, 0)),
                   pl.BlockSpec((1, 1, KV_HEADS * V_AUG_ROWS, tm), lambda b, i: (b, i, 0, 0))],
        out_shape=[jax.ShapeDtypeStruct((B, ATTN_HEADS, KW, S), BF16),
                   jax.ShapeDtypeStruct((B, S // tm, tm, KW), BF16),
                   jax.ShapeDtypeStruct((B, S // tm, KV_HEADS * V_AUG_ROWS, tm), BF16)],
        compiler_params=_cparams(("parallel", "parallel")),
        name="attn_prep",
    )(pq, pkv, cos_t, sin_t, gq, gk, bind, psw)


def _attn_kernel(qt_ref, k_ref, vt_ref, o_ref, st_scr, m_scr, acc_scr, *, nk):
    def scores_to(slot, c):
        kc = k_ref[0, c]
        for h in range(GQA_GROUP):
            st_scr[slot, h] = _dot(kc, qt_ref[0, h])

    def consume(slot, c):
        vc = vt_ref[0, c]
        for h in range(GQA_GROUP):
            st = st_scr[slot, h]
            m_old = m_scr[h]
            m_new = jnp.maximum(m_old, jnp.max(st, axis=0, keepdims=True))
            p = jnp.exp2(st - m_new).astype(BF16)
            acc_scr[h] = jnp.exp2(m_old - m_new) * acc_scr[h] + _dot(vc, p)
            m_scr[h] = m_new

    m_scr[...] = jnp.full(m_scr.shape, -jnp.inf, F32)
    acc_scr[...] = jnp.zeros(acc_scr.shape, F32)
    scores_to(0, 0)

    def step(nslot, cn, cslot, cc):
        kc = k_ref[0, cn]
        vc = vt_ref[0, cc]
        for h in range(GQA_GROUP):
            st_scr[nslot, h] = _dot(kc, qt_ref[0, h])
            st = st_scr[cslot, h]
            m_old = m_scr[h]
            m_new = jnp.maximum(m_old, jnp.max(st, axis=0, keepdims=True))
            p = jnp.exp2(st - m_new).astype(BF16)
            acc_scr[h] = jnp.exp2(m_old - m_new) * acc_scr[h] + _dot(vc, p)
            m_scr[h] = m_new

    def body(j, carry):
        c0 = 2 * j
        step(1, c0 + 1, 0, c0)
        step(0, c0 + 2, 1, c0 + 1)
        return carry
    lax.fori_loop(0, nk // 2 - 1, body, 0)
    step(1, nk - 1, 0, nk - 2)
    consume(1, nk - 1)

    outs = []
    for h in range(GQA_GROUP):
        acc = acc_scr[h]
        outs.append((acc[:HEAD_DIM] / acc[HEAD_DIM:HEAD_DIM + 1]).T)
    o_ref[0] = jnp.concatenate(outs, axis=-1).astype(o_ref.dtype)


def _attention(qt, k, vt):
    B, nk, tk, KW = k.shape
    S = nk * tk
    GW = GQA_GROUP * HEAD_DIM
    tq = min(TQ_ATTN, S)
    assert nk % 2 == 0
    scratch = [pltpu.VMEM((2, GQA_GROUP, tk, tq), F32), pltpu.VMEM((GQA_GROUP, 1, tq), F32),
               pltpu.VMEM((GQA_GROUP, V_AUG_ROWS, tq), F32)]
    return pl.pallas_call(
        functools.partial(_attn_kernel, nk=nk),
        scratch_shapes=scratch,
        grid=(B, KV_HEADS, S // tq),
        in_specs=[pl.BlockSpec((1, GQA_GROUP, KW, tq), lambda b, g, i: (b, g, 0, i)),
                  pl.BlockSpec((1, nk, tk, KW), lambda b, g, i: (b, 0, 0, 0)),
                  pl.BlockSpec((1, nk, V_AUG_ROWS, tk), lambda b, g, i: (b, 0, g, 0))],
        out_specs=pl.BlockSpec((1, tq, GW), lambda b, g, i: (b, i, g)),
        out_shape=jax.ShapeDtypeStruct((B, S, ATTN_HEADS * HEAD_DIM), BF16),
        compiler_params=_cparams(("parallel", "parallel", "parallel")),
        name="attention",
    )(qt, k, vt)


def _log1p(y):
    u = 1.0 + y
    return jnp.where(u == 1.0, y, jnp.log(u) * (y / (u - 1.0)))


def _softplus(z):
    return jnp.maximum(z, 0.0) + _log1p(jnp.exp(-jnp.abs(z)))


def _neg_expm1_2x(z, ez):
    return -jnp.tanh(z) * (ez * ez + 1.0)


def _gelu_tanh(x):
    return 0.5 * x * (1.0 + jnp.tanh(math.sqrt(2.0 / math.pi) * (x + 0.044715 * (x * x * x))))


def _lru_kernel(*refs, reverse, T):
    if reverse:
        (x_ref, xp_ref, xn_ref, cw_ref, cb_ref, wbd_ref, gb_ref, lam_ref, hf_ref, lg_ref,
         o_ref, carry_ref) = refs
    else:
        (x_ref, xp_ref, xn_ref, cw_ref, cb_ref, wbd_ref, gb_ref, lam_ref,
         o_ref, carry_ref) = refs
    i = pl.program_id(1)
    nt = pl.num_programs(1)
    tile = nt - 1 - i if reverse else i
    W = x_ref.shape[-1]

    @pl.when(i == 0)
    def _():
        carry_ref[...] = jnp.zeros_like(carry_ref)

    x = x_ref[0].astype(F32)
    prev = jnp.where(tile == 0, 0.0, xp_ref[0].astype(F32))
    nxt = jnp.where(tile == nt - 1, 0.0, xn_ref[0].astype(F32))
    row8 = lax.broadcasted_iota(jnp.int32, (V7X_SUBLANES, W), 0)

    def shifted(k):
        if k == 0:
            return x
        r = pltpu.roll(x, (-k) % T, axis=0)
        if k < 0:
            fill = pltpu.roll(prev, (-k) % V7X_SUBLANES, axis=0)
            head = jnp.where(row8 < -k, fill, r[:V7X_SUBLANES])
            return jnp.concatenate([head, r[V7X_SUBLANES:]], axis=0)
        fill = pltpu.roll(nxt, (-k) % V7X_SUBLANES, axis=0)
        tail = jnp.where(row8 >= V7X_SUBLANES - k, fill, r[T - V7X_SUBLANES:])
        return jnp.concatenate([r[:T - V7X_SUBLANES], tail], axis=0)

    cw = cw_ref[...]
    xc = cb_ref[...] + sum(shifted(j - CONV_LEFT) * cw[j:j + 1] for j in range(CONV_W))

    gates = _dot(xc.astype(BF16), wbd_ref[0]) + gb_ref[0]
    r = _sigmoid(gates[:, :W])
    ig = _sigmoid(gates[:, W:])
    log_a = (-LRU_C) * r * _softplus(-lam_ref[0])
    a = jnp.exp(log_a)
    u = jnp.sqrt(_neg_expm1_2x(log_a, a)) * (ig * xc)

    row = lax.broadcasted_iota(jnp.int32, (T, W), 0)

    def neighbour(z, d, fill):
        if d % V7X_SUBLANES == 0:
            pad = jnp.full((d, W), fill, F32)
            return jnp.concatenate([z[d:], pad] if reverse else [pad, z[:T - d]], axis=0)
        if reverse:
            return jnp.where(row < T - d, pltpu.roll(z, T - d, axis=0), fill)
        return jnp.where(row >= d, pltpu.roll(z, d, axis=0), fill)

    A, U = a, u
    d = 1
    while d < T:
        U = A * neighbour(U, d, 0.0) + U
        A = A * neighbour(A, d, 1.0)
        d *= 2
    h = A * carry_ref[...] + U
    if reverse:
        carry_ref[...] = h[0:1]
        o_ref[0] = ((hf_ref[0] + h) * lg_ref[0].astype(F32)).astype(o_ref.dtype)
    else:
        carry_ref[...] = h[T - 1:T]
        o_ref[0] = h


def _lru_pass(plru, conv_w, conv_b, wbd, gb, lam, hf, reverse):
    B, S, W2 = plru.shape
    W = W2 // 2
    T = min(T_LRU, S)
    nt = S // T
    r8 = T // V7X_SUBLANES
    nb8 = S // V7X_SUBLANES
    d = 1 if reverse else 0

    def tmap(i):
        return nt - 1 - i if reverse else i

    in_specs = [
        pl.BlockSpec((1, T, W), lambda b, i: (b, tmap(i), 0)),
        pl.BlockSpec((1, V7X_SUBLANES, W), lambda b, i: (b, jnp.maximum(tmap(i) * r8 - 1, 0), 0)),
        pl.BlockSpec((1, V7X_SUBLANES, W), lambda b, i: (b, jnp.minimum((tmap(i) + 1) * r8, nb8 - 1), 0)),
        pl.BlockSpec((CONV_W, W), lambda b, i: (0, 0)),
        pl.BlockSpec((1, W), lambda b, i: (0, 0)),
        pl.BlockSpec((1, W, 2 * W), lambda b, i: (d, 0, 0)),
        pl.BlockSpec((1, 1, 2 * W), lambda b, i: (d, 0, 0)),
        pl.BlockSpec((1, 1, W), lambda b, i: (d, 0, 0)),
    ]
    args = [plru, plru, plru, conv_w, conv_b, wbd, gb, lam]
    if reverse:
        in_specs += [pl.BlockSpec((1, T, W), lambda b, i: (b, tmap(i), 0)),
                     pl.BlockSpec((1, T, W), lambda b, i: (b, tmap(i), 1))]
        args += [hf, plru]
    return pl.pallas_call(
        functools.partial(_lru_kernel, reverse=reverse, T=T),
        grid=(B, nt),
        in_specs=in_specs,
        out_specs=pl.BlockSpec((1, T, W), lambda b, i: (b, tmap(i), 0)),
        out_shape=jax.ShapeDtypeStruct((B, S, W), BF16 if reverse else F32),
        scratch_shapes=[pltpu.VMEM((1, W), F32)],
        compiler_params=_cparams(("parallel", "arbitrary")),
        name="lru_bwd" if reverse else "lru_fwd",
    )(*args)


def _hgrn_consts(C, reverse):
    nl = int(round(math.log2(C)))
    t = np.arange(C)
    halves = [C >> (lvl + 1) for lvl in range(nl)]
    a_rows, masks, upper = [], [], []
    for m in halves:
        blk = t // (2 * m)
        up = (t % (2 * m)) >= m
        mid = blk * 2 * m + m
        r = t[None, :]
        masks.append((blk[:, None] == blk[None, :]) & up[:, None] & (~up)[None, :])
        if m < V7X_SUBLANES:
            a_up = (r >= mid[:, None]) & (r <= t[:, None])
            a_lo = (r > t[:, None]) & (r <= mid[:, None] - 1)
            a_rows.append(np.where(up[:, None], a_up, a_lo))
            upper.append(up)
    a_rows.append(t[None, :] <= t[:, None])
    a_rows.append(np.ones((V7X_SUBLANES, C), bool))
    masks.append(np.eye(C, dtype=bool))
    if reverse:
        a_rows = [a[::-1, ::-1] if a.shape[0] == C else a for a in a_rows]
        masks = [mm[::-1, ::-1] for mm in masks]
        upper = [u[::-1] for u in upper]
    amat = np.concatenate(a_rows, 0).astype(np.float32)
    msk = np.stack(masks, 0).astype(np.float32)
    upv = np.stack(upper, 0).astype(np.float32)[:, :, None]
    return halves, amat, msk, upv


def _hgrn_kernel(*refs, layer, reverse, C, CB, halves):
    if reverse:
        (hq_ref, hf_ref, hi_ref, lbl_ref, amat_ref, msk_ref, up_ref, of_ref, hg_ref, gain_ref,
         o_ref, st_ref) = refs
    else:
        (hq_ref, hf_ref, hi_ref, lbl_ref, amat_ref, msk_ref, up_ref, o_ref, st_ref) = refs
    H = HGRN_HEADS
    W = hq_ref.shape[-1]
    dk = W // H
    NL = len(halves)
    n_small = sum(1 for m in halves if m < V7X_SUBLANES)

    @pl.when(pl.program_id(1) == 0)
    def _():
        st_ref[...] = jnp.zeros_like(st_ref)

    lgt = lbl_ref[...]
    e = jnp.exp(lgt - jnp.max(lgt, axis=0, keepdims=True))
    p = e / jnp.sum(e, axis=0, keepdims=True)
    lb = jnp.zeros((1, W), F32)
    for j in range(1, layer + 1):
        lb = lb + p[j:j + 1]

    def gates_and_factors(j):
        rows = slice(j * C, (j + 1) * C)
        q = hq_ref[0, rows].astype(F32)
        f = lb + (1.0 - lb) * _sigmoid(hf_ref[0, rows].astype(F32))
        k = 1.0 - f
        g = jnp.log2(f)
        v = hi_ref[0, rows].astype(F32)

        sums = _dot_sel_x(amat_ref[...], g)
        b = sums[n_small * C:(n_small + 1) * C]
        tot = sums[(n_small + 1) * C:(n_small + 1) * C + 1]
        xs = []
        small = 0
        for m in halves:
            if m >= V7X_SUBLANES:
                shp = (C // (2 * m), 2 * m, W)
                b3 = b.reshape(shp)
                r0 = m if reverse else m - 1
                ref = b3[:, r0:r0 + 1, :]
                rowi = lax.broadcasted_iota(jnp.int32, shp, 1)
                qside = (rowi < m) if reverse else (rowi >= m)
                x3 = jnp.where(qside, q.reshape(shp), k.reshape(shp)) * jnp.exp2(
                    jnp.where(qside, b3 - ref, ref - b3))
                xs.append(x3.reshape(C, W).astype(BF16))
            else:
                xs.append((jnp.where(up_ref[small] > 0.5, q, k)
                           * jnp.exp2(sums[small * C:(small + 1) * C])).astype(BF16))
                small += 1
        return dict(rows=rows, v=v.astype(BF16), vt=v.T.astype(BF16), xs=xs, qk=q * k,
                    qb=(q * jnp.exp2(b)).astype(BF16), kr=(k * jnp.exp2(tot - b)).astype(BF16),
                    ex_tot=jnp.exp2(tot))

    def scores_and_state(a):
        scs, inters = [], []
        for h in range(H):
            sl = slice(h * dk, (h + 1) * dk)
            parts = [_dot_nt(a['xs'][l][:, sl], a['xs'][l][:, sl]) for l in range(NL)]
            st = st_ref[h]
            inters.append(_dot_nt(a['qb'][:, sl], st.astype(BF16)))
            st_ref[h] = st * a['ex_tot'][:, sl] + _dot(a['vt'][sl, :], a['kr'][:, sl])
            sc = msk_ref[NL] * jnp.sum(a['qk'][:, sl], axis=-1, keepdims=True)
            for l in range(NL):
                sc = sc + msk_ref[l] * parts[l]
            scs.append(sc.astype(BF16))
        return scs, inters

    def outputs(a, scs, inters):
        rows = a['rows']
        if reverse:
            of = of_ref[0, rows]
            hg = hg_ref[0, rows].astype(F32)
            gain = gain_ref[...]
        for h in range(H):
            sl = slice(h * dk, (h + 1) * dk)
            o_h = inters[h] + _dot(scs[h], a['v'][:, sl])
            if reverse:
                tot_o = of[:, sl] + o_h
                ms = jnp.mean(tot_o * tot_o, axis=-1, keepdims=True)
                o_ref[0, rows, sl] = (tot_o * lax.rsqrt(ms + RMS_EPS) * gain
                                      * hg[:, sl]).astype(o_ref.dtype)
            else:
                o_ref[0, rows, sl] = o_h

    order = list(range(CB - 1, -1, -1) if reverse else range(CB))
    cur = gates_and_factors(order[0])
    for n, j in enumerate(order):
        scs, inters = scores_and_state(cur)
        nxt = gates_and_factors(order[n + 1]) if n + 1 < CB else None
        outputs(cur, scs, inters)
        cur = nxt


def _hgrn_pass(phg, lb_logits, norm_gain, of, layer, reverse):
    B, S, W5 = phg.shape
    W = W5 // 5
    C = min(C_HGRN, S)
    CB = min(CB_HGRN, S // C)
    TB = CB * C
    nc = S // TB
    halves, amat, msk, upv = _hgrn_consts(C, reverse)
    L = lb_logits.shape[0]
    dk = W // HGRN_HEADS

    def cmap(i):
        return nc - 1 - i if reverse else i

    fcol = 2 if reverse else 1
    in_specs = [
        pl.BlockSpec((1, TB, W), lambda b, i: (b, cmap(i), 0)),
        pl.BlockSpec((1, TB, W), lambda b, i: (b, cmap(i), fcol)),
        pl.BlockSpec((1, TB, W), lambda b, i: (b, cmap(i), 3)),
        pl.BlockSpec((L, W), lambda b, i: (0, 0)),
        pl.BlockSpec(amat.shape, lambda b, i: (0, 0)),
        pl.BlockSpec(msk.shape, lambda b, i: (0, 0, 0)),
        pl.BlockSpec(upv.shape, lambda b, i: (0, 0, 0)),
    ]
    args = [phg, phg, phg, lb_logits, jnp.asarray(amat, BF16), jnp.asarray(msk, F32), jnp.asarray(upv, F32)]
    if reverse:
        in_specs += [pl.BlockSpec((1, TB, W), lambda b, i: (b, cmap(i), 0)),
                     pl.BlockSpec((1, TB, W), lambda b, i: (b, cmap(i), 4)),
                     pl.BlockSpec((1, dk), lambda b, i: (0, 0))]
        args += [of, phg, norm_gain.reshape(1, dk)]
    return pl.pallas_call(
        functools.partial(_hgrn_kernel, layer=layer, reverse=reverse, C=C, CB=CB, halves=tuple(halves)),
        grid=(B, nc),
        in_specs=in_specs,
        out_specs=pl.BlockSpec((1, TB, W), lambda b, i: (b, cmap(i), 0)),
        out_shape=jax.ShapeDtypeStruct((B, S, W), BF16 if reverse else F32),
        scratch_shapes=[pltpu.VMEM((HGRN_HEADS, dk, dk), F32)],
        compiler_params=_cparams(("parallel", "arbitrary")),
        name="hgrn_bwd" if reverse else "hgrn_fwd",
    )(*args)


def _merge_kernel(ya_ref, yl_ref, yh_ref, g0_ref, g1_ref, g2_ref, x_ref, mod_ref,
                  wa_ref, wl_ref, wh_ref, wo_ref, lng_ref, lnb_ref, wr_ref,
                  x1_ref, h2_ref, route_ref, *, alpha, n_groups, e_per):
    m = mod_ref[0]
    half = x_ref.shape[1] // 2
    halves = [slice(r0, r0 + half) for r0 in (0, half)]

    branch = [(_dot(ya_ref[0, rs], wa_ref[...]), _dot(yl_ref[0, rs], wl_ref[...]),
               _dot(yh_ref[0, rs], wh_ref[...])) for rs in halves]
    mixes = []
    for rs, (pa, pl_, ph) in zip(halves, branch):
        merged = (_sigmoid(g0_ref[0, rs].astype(F32)) * pa + _sigmoid(g1_ref[0, rs].astype(F32)) * pl_
                  + _sigmoid(g2_ref[0, rs].astype(F32)) * ph)
        mixes.append(_dot(merged.astype(BF16), wo_ref[...]))
    logit_halves = []
    for rs, mix in zip(halves, mixes):
        x1 = _layernorm(alpha * x_ref[0, rs] + (1.0 + m[2:3]) * mix, lng_ref[...], lnb_ref[...])
        x1_ref[0, rs] = x1
        h2 = x1 * (1.0 + m[4:5]) + m[3:4]
        h2_ref[0, rs] = h2
        xh, xl = _split2(h2)
        r = _dot(xh, wr_ref[...])
        logit_halves.append(r[:, :V7X_LANES] + r[:, V7X_LANES:] + _dot(xl, wr_ref[:, :V7X_LANES]))
    for rs, logits in zip(halves, logit_halves):
        route_ref[0, rs] = _route(logits, n_groups, e_per)


def _route(logits, n_groups, e_per):
    lane = lax.broadcasted_iota(jnp.int32, logits.shape, 1)
    big = jnp.int32(V7X_LANES)
    neg = jnp.float32(-jnp.inf)
    gl = jnp.where(lane < n_groups, logits, neg)
    gmax = jnp.max(gl, axis=-1, keepdims=True)
    gsel = jnp.min(jnp.where(gl == gmax, lane, big), axis=-1, keepdims=True)
    gp = 1.0 / jnp.sum(jnp.exp(gl - gmax), axis=-1, keepdims=True)
    lo = n_groups + gsel * e_per
    el = jnp.where((lane >= lo) & (lane < lo + e_per), logits, neg)
    m1 = jnp.max(el, axis=-1, keepdims=True)
    i1 = jnp.min(jnp.where(el == m1, lane, big), axis=-1, keepdims=True)
    el2 = jnp.where(lane == i1, neg, el)
    m2 = jnp.max(el2, axis=-1, keepdims=True)
    i2 = jnp.min(jnp.where(el2 == m2, lane, big), axis=-1, keepdims=True)
    z = jnp.sum(jnp.exp(el - m1), axis=-1, keepdims=True)
    p1 = 1.0 / z
    p2 = jnp.exp(m2 - m1) / z
    w1 = gp * (p1 / (p1 + p2))
    w2 = gp * (p2 / (p1 + p2))
    e1 = (i1 - n_groups).astype(F32)
    e2 = (i2 - n_groups).astype(F32)
    return jnp.where(lane == 0, w1, jnp.where(lane == 1, w2,
                     jnp.where(lane == 2, e1, jnp.where(lane == 3, e2, 0.0))))


def _merge(ya, yl, yh, pmg, x, mod_l, wa, wl, wh, wo, lng, lnb, wr, n_groups, e_per, alpha):
    B, S, D = x.shape
    tm = min(TM_MERGE, S)
    bw = ya.shape[-1]
    tok = lambda b, i: (b, i, 0)
    const = lambda b, i: (0, 0)
    return pl.pallas_call(
        functools.partial(_merge_kernel, alpha=alpha, n_groups=n_groups, e_per=e_per),
        grid=(B, S // tm),
        in_specs=[pl.BlockSpec((1, tm, bw), tok), pl.BlockSpec((1, tm, bw), tok), pl.BlockSpec((1, tm, bw), tok),
                  pl.BlockSpec((1, tm, D), lambda b, i: (b, i, 0)),
                  pl.BlockSpec((1, tm, D), lambda b, i: (b, i, 1)),
                  pl.BlockSpec((1, tm, D), lambda b, i: (b, i, 2)),
                  pl.BlockSpec((1, tm, D), tok),
                  pl.BlockSpec((1, 6, D), lambda b, i: (b, 0, 0)),
                  pl.BlockSpec((bw, D), const), pl.BlockSpec((bw, D), const), pl.BlockSpec((bw, D), const),
                  pl.BlockSpec((D, D), const),
                  pl.BlockSpec((1, D), const), pl.BlockSpec((1, D), const),
                  pl.BlockSpec((D, 2 * V7X_LANES), const)],
        out_specs=[pl.BlockSpec((1, tm, D), tok), pl.BlockSpec((1, tm, D), tok),
                   pl.BlockSpec((1, tm, V7X_LANES), tok)],
        out_shape=[jax.ShapeDtypeStruct((B, S, D), F32), jax.ShapeDtypeStruct((B, S, D), F32),
                   jax.ShapeDtypeStruct((B, S, V7X_LANES), F32)],
        compiler_params=_cparams(("parallel", "parallel")),
        name="merge",
    )(ya, yl, yh, pmg, pmg, pmg, x, mod_l, wa, wl, wh, wo, lng, lnb, wr)


def _dispatch_kernel(dest_ref, h_ref, xs_in, xs_hbm, sem, *, TM):
    del xs_in

    for r in range(TM):
        for a in range(TOP_K):
            pltpu.make_async_copy(h_ref.at[pl.ds(r, 1)],
                                  xs_hbm.at[pl.ds(dest_ref[a, 0, 0, r], 1)],
                                  sem).start(priority=a % 2)
    for a in range(TOP_K):
        pltpu.make_async_copy(h_ref, xs_hbm.at[pl.ds(0, TM)], sem).wait()


def _dispatch(h2, dest, xs_prev, S):
    N, D = h2.shape
    TM = min(TM_DISPATCH, S)
    nt = N // TM
    return pl.pallas_call(
        functools.partial(_dispatch_kernel, TM=TM),
        grid=(nt,),
        in_specs=[pl.BlockSpec((TOP_K, 1, 1, TM), lambda i: (0, i, 0, 0), memory_space=pltpu.SMEM),
                  pl.BlockSpec((TM, D), lambda i: (i, 0)),
                  pl.BlockSpec(memory_space=pl.ANY)],
        out_specs=pl.BlockSpec(memory_space=pl.ANY),
        out_shape=jax.ShapeDtypeStruct(xs_prev.shape, xs_prev.dtype),
        scratch_shapes=[pltpu.SemaphoreType.DMA(())],
        input_output_aliases={2: 0},
        compiler_params=_cparams(("arbitrary",)),
        name="moe_dispatch",
    )(dest.reshape(TOP_K, nt, 1, TM), h2, xs_prev)


def _expert_kernel(blk_e_ref, nused_ref, xs_ref, w13_ref, w2_ref, ys_ref, *, F):
    i = pl.program_id(0)

    @pl.when(i < nused_ref[0])
    def _():
        half = xs_ref.shape[0] // 2
        abs_ = [_dot(xs_ref[r0:r0 + half, :].astype(BF16), w13_ref[0]) for r0 in (0, half)]
        for n, r0 in enumerate((0, half)):
            hmid = (_silu(abs_[n][:, :F]) * abs_[n][:, F:]).astype(BF16)
            ys_ref[r0:r0 + half, :] = _dot(hmid, w2_ref[0])

    @pl.when(i >= nused_ref[0])
    def _():
        ys_ref[...] = jnp.zeros_like(ys_ref)


def _experts(xs, blk_e, nused, w13, w2):
    P, D = xs.shape
    R = R_MOE
    nb = P // R
    F = w2.shape[1]
    grid_spec = pltpu.PrefetchScalarGridSpec(
        num_scalar_prefetch=2,
        grid=(nb,),
        in_specs=[pl.BlockSpec((R, D), lambda i, be, nu: (i, 0)),
                  pl.BlockSpec((1, D, 2 * F), lambda i, be, nu: (be[i], 0, 0)),
                  pl.BlockSpec((1, F, D), lambda i, be, nu: (be[i], 0, 0))],
        out_specs=pl.BlockSpec((R, D), lambda i, be, nu: (i, 0)),
    )
    return pl.pallas_call(
        functools.partial(_expert_kernel, F=F),
        grid_spec=grid_spec,
        out_shape=jax.ShapeDtypeStruct((P, D), F32),
        compiler_params=_cparams(("arbitrary",)),
        name="moe_experts",
    )(blk_e, nused, xs, w13, w2)


def _combine_kernel(pos_ref, posn_ref, ys_hbm, route_ref, x_ref, mod_ref, lng_ref, lnb_ref, o_ref,
                    ybuf, sems, *, TC, alpha):
    i = pl.program_id(0)
    n = pl.num_programs(0)
    slot = i % 2

    def issue(p_ref, s):
        for r in range(TC):
            for a in range(TOP_K):
                pltpu.make_async_copy(ys_hbm.at[pl.ds(p_ref[a, 0, 0, r], 1)],
                                      ybuf.at[s, a, pl.ds(r, 1)], sems.at[s]).start(priority=a % 2)

    @pl.when(i == 0)
    def _():
        issue(pos_ref, 0)

    for s in range(2):
        @pl.when((i + 1 < n) & (slot == 1 - s))
        def _(s=s):
            issue(posn_ref, s)

    for a in range(TOP_K):
        pltpu.make_async_copy(ys_hbm.at[pl.ds(0, TC)], ybuf.at[slot, a], sems.at[slot]).wait()

    m = mod_ref[0]
    w = route_ref[...]
    ffn = w[:, 0:1] * ybuf[slot, 0]
    for a in range(1, TOP_K):
        ffn = ffn + w[:, a:a + 1] * ybuf[slot, a]
    o_ref[...] = _layernorm(alpha * x_ref[...] + (1.0 + m[5:6]) * ffn, lng_ref[...], lnb_ref[...])


def _combine(ys, dest, route, x1, mod_l, lng, lnb, S, alpha):
    N, D = x1.shape
    TC = min(TC_MOE, S)
    nt = N // TC
    per_b = S // TC
    pos = dest.reshape(TOP_K, nt, 1, TC)
    return pl.pallas_call(
        functools.partial(_combine_kernel, TC=TC, alpha=alpha),
        grid=(nt,),
        in_specs=[pl.BlockSpec((TOP_K, 1, 1, TC), lambda i: (0, i, 0, 0), memory_space=pltpu.SMEM),
                  pl.BlockSpec((TOP_K, 1, 1, TC), lambda i: (0, jnp.minimum(i + 1, nt - 1), 0, 0),
                               memory_space=pltpu.SMEM),
                  pl.BlockSpec(memory_space=pl.ANY),
                  pl.BlockSpec((TC, V7X_LANES), lambda i: (i, 0)),
                  pl.BlockSpec((TC, D), lambda i: (i, 0)),
                  pl.BlockSpec((1, 6, D), lambda i: (i // per_b, 0, 0)),
                  pl.BlockSpec((1, D), lambda i: (0, 0)),
                  pl.BlockSpec((1, D), lambda i: (0, 0))],
        out_specs=pl.BlockSpec((TC, D), lambda i: (i, 0)),
        out_shape=jax.ShapeDtypeStruct((N, D), F32),
        scratch_shapes=[pltpu.VMEM((2, TOP_K, TC, D), F32), pltpu.SemaphoreType.DMA((2,))],
        compiler_params=_cparams(("arbitrary",)),
        name="moe_combine",
    )(pos, pos, ys, route, x1, mod_l, lng, lnb)


def _dispatch_plan(route, n_experts, R):
    lanes = jnp.arange(n_experts, dtype=jnp.int32)
    hots = [route[:, TOP_K + a].astype(jnp.int32)[:, None] == lanes for a in range(TOP_K)]
    hot = sum(h.astype(jnp.int32) for h in hots)
    csum = jnp.cumsum(hot, axis=0)
    counts = csum[-1]
    pcounts = (counts + R - 1) // R * R
    pends = jnp.cumsum(pcounts)
    pstarts = pends - pcounts
    base = pstarts[None, :] + csum - hot
    dest = jnp.stack([jnp.sum(jnp.where(h, base, 0), axis=1) for h in hots], 0).astype(jnp.int32)
    nb = (route.shape[0] * TOP_K + n_experts * R) // R
    starts = jnp.arange(nb, dtype=jnp.int32) * R
    blk_e = jnp.minimum((pends[None, :] <= starts[:, None]).astype(jnp.int32).sum(1), n_experts - 1)
    nused = (pends[-1:] // R).astype(jnp.int32)
    return blk_e, nused, dest


def _prep_weights(p):
    L = p['w_in'].shape[0]
    W = p['lru_conv_w'].shape[-1]
    nblk, bs = p['lru_wa'].shape[2], p['lru_wa'].shape[3]

    def blockdiag(w):
        eye = jnp.eye(nblk, dtype=w.dtype)
        return jnp.einsum('ldnij,nm->ldnimj', w, eye).reshape(L, 2, W, W)

    G, E = p['router_expert_w'].shape[1], p['router_expert_w'].shape[3]
    D = p['w_in'].shape[1]
    wr = jnp.concatenate([p['router_group_w'],
                          p['router_expert_w'].transpose(0, 2, 1, 3).reshape(L, D, G * E)], -1)
    wr = jnp.pad(wr, ((0, 0), (0, 0), (0, V7X_LANES - wr.shape[-1])))
    wr_hi = wr.astype(BF16)
    wr = jnp.concatenate([wr_hi, (wr - wr_hi.astype(F32)).astype(BF16)], -1)
    return dict(
        w_in=p['w_in'].astype(BF16),
        gq=jnp.tile(p['attn_q_gain'], (1, ATTN_HEADS))[:, None, :],
        gk=jnp.tile(p['attn_k_gain'], (1, KV_HEADS))[:, None, :],
        w_attn_o=p['w_attn_o'].astype(BF16),
        conv_w=p['lru_conv_w'], conv_b=p['lru_conv_b'][:, None, :],
        wbd=jnp.concatenate([blockdiag(p['lru_wa']), blockdiag(p['lru_wi'])], -1).astype(BF16),
        gb=jnp.concatenate([p['lru_ba'], p['lru_bi']], -1)[:, :, None, :],
        lam=p['lru_lambda'][:, :, None, :],
        w_lru_o=p['w_lru_o'].astype(BF16),
        w_hgrn_o=p['w_hgrn_o'].astype(BF16),
        w_out=p['w_out'].astype(BF16),
        ln1_g=p['ln1_g'][:, None, :], ln1_b=p['ln1_b'][:, None, :],
        ln2_g=p['ln2_g'][:, None, :], ln2_b=p['ln2_b'][:, None, :],
        wr=wr, n_groups=G, e_per=E,
        w13=p['expert_w13'].astype(BF16), w2=p['expert_w2'].astype(BF16),
    )


def _trunk(x, c, p, wp):
    B, S, D = x.shape
    L = p['w_in'].shape[0]
    alpha = (2 * L) ** 0.25
    QW = ATTN_HEADS * HEAD_DIM
    KW = KV_HEADS * HEAD_DIM
    LW = p['lru_conv_w'].shape[-1]
    HW = p['hgrn_lb_logits'].shape[-1]
    widths = (QW, 2 * KW, 2 * LW, 5 * HW, N_BRANCH * D)
    assert sum(widths) == p['w_in'].shape[-1]
    acts = (((0, None),), ((0, None),), ((0, None), (LW, _gelu_tanh)),
            ((0, _silu), (HW, None), (4 * HW, _silu)), ((0, None),))
    n_experts = wp['n_groups'] * wp['e_per']

    mod = _ada_mod(c, p['ada_w'], p['ada_b'])
    cos_t, sin_t = _rope_tables(S)
    bind, psw = _head_consts()
    cos_t, sin_t = jnp.asarray(cos_t), jnp.asarray(sin_t)
    bind, psw = jnp.asarray(bind, BF16), jnp.asarray(psw, BF16)

    xs = None
    for l in range(L):
        pq, pkv, plru, phg, pmg = _inproj(x, mod[l], wp['w_in'][l], widths, acts)
        qt, k, vt = _attn_prep(pq, pkv, cos_t, sin_t, wp['gq'][l], wp['gk'][l], bind, psw)
        ya = _attention(qt, k, vt)
        lru_args = (plru, wp['conv_w'][l], wp['conv_b'][l], wp['wbd'][l], wp['gb'][l], wp['lam'][l])
        hf = _lru_pass(*lru_args, None, reverse=False)
        yl = _lru_pass(*lru_args, hf, reverse=True)
        of = _hgrn_pass(phg, p['hgrn_lb_logits'], p['hgrn_norm_gain'][l], None, l, reverse=False)
        yh = _hgrn_pass(phg, p['hgrn_lb_logits'], p['hgrn_norm_gain'][l], of, l, reverse=True)
        x1, h2, route = _merge(ya, yl, yh, pmg, x, mod[l], wp['w_attn_o'][l], wp['w_lru_o'][l],
                               wp['w_hgrn_o'][l], wp['w_out'][l], wp['ln1_g'][l], wp['ln1_b'][l],
                               wp['wr'][l], wp['n_groups'], wp['e_per'], alpha)
        N = B * S
        route = route.reshape(N, V7X_LANES)
        blk_e, nused, dest = _dispatch_plan(route, n_experts, R_MOE)
        if xs is None:
            xs = jnp.zeros((N * TOP_K + n_experts * R_MOE, D), F32)
        xs = _dispatch(h2.reshape(N, D), dest, xs, S)
        ys = _experts(xs, blk_e, nused, wp['w13'][l], wp['w2'][l])
        x = _combine(ys, dest, route, x1.reshape(N, D), mod[l], wp['ln2_g'][l], wp['ln2_b'][l], S,
                     alpha).reshape(B, S, D)
    return x


def kernel(x_prompt, x_sample, c_prompt, c_sample, ada_w, ada_b, w_in, attn_q_gain, attn_k_gain, w_attn_o, lru_conv_w, lru_conv_b, lru_wa, lru_ba, lru_wi, lru_bi, lru_lambda, w_lru_o, hgrn_lb_logits, hgrn_norm_gain, w_hgrn_o, w_out, ln1_g, ln1_b, router_group_w, router_expert_w, expert_w13, expert_w2, ln2_g, ln2_b):
    p = dict(ada_w=ada_w, ada_b=ada_b, w_in=w_in, attn_q_gain=attn_q_gain, attn_k_gain=attn_k_gain,
             w_attn_o=w_attn_o, lru_conv_w=lru_conv_w, lru_conv_b=lru_conv_b, lru_wa=lru_wa, lru_ba=lru_ba,
             lru_wi=lru_wi, lru_bi=lru_bi, lru_lambda=lru_lambda, w_lru_o=w_lru_o,
             hgrn_lb_logits=hgrn_lb_logits, hgrn_norm_gain=hgrn_norm_gain, w_hgrn_o=w_hgrn_o, w_out=w_out,
             ln1_g=ln1_g, ln1_b=ln1_b, router_group_w=router_group_w, router_expert_w=router_expert_w,
             expert_w13=expert_w13, expert_w2=expert_w2, ln2_g=ln2_g, ln2_b=ln2_b)
    wp = _prep_weights(p)
    return (_trunk(x_prompt, c_prompt, p, wp), _trunk(x_sample, c_sample, p, wp))
```

```python
import functools
import math

import numpy as np
import jax
import jax.numpy as jnp
from jax import lax
from jax.experimental import pallas as pl
from jax.experimental.pallas import tpu as pltpu

F32 = jnp.float32
BF16 = jnp.bfloat16

GRID_W = 64
CONV_W = 4
CONV_LEFT = 2
LRU_C = 8.0
ATTN_HEADS = 8
KV_HEADS = 2
HEAD_DIM = 64
GQA_GROUP = ATTN_HEADS // KV_HEADS
AXIAL_DIM = HEAD_DIM // 2
AXIAL_FREQS = AXIAL_DIM // 2
ROPE_THETA = 10000.0
HGRN_HEADS = 4
N_BRANCH = 3
TOP_K = 2
LN_EPS = 1e-5
RMS_EPS = 1e-6
LOG2_E = 1.4426950408889634

V7X_LANES = 128
V7X_SUBLANES = 8
V7X_VMEM_LIMIT_BYTES = 56 * 1024 * 1024

TM_INPROJ = 512
V_AUG_ROWS = HEAD_DIM + 16
TQ_ATTN = 512
T_LRU = 256
C_HGRN = 128
CB_HGRN = 4
TK_ATTN = 512
TM_MERGE = 512
R_MOE = 512
TM_DISPATCH = 256
TC_MOE = 256


def _cparams(sem):
    return pltpu.CompilerParams(dimension_semantics=sem, vmem_limit_bytes=V7X_VMEM_LIMIT_BYTES)


def _dot(a, b):
    return jnp.dot(a, b, preferred_element_type=F32)


def _dot_nt(a, b):
    return lax.dot_general(a, b, (((1,), (1,)), ((), ())), preferred_element_type=F32)


def _dot_hi(a, b):
    return jnp.dot(a, b, preferred_element_type=F32, precision=lax.Precision.HIGHEST)


def _split2(x):
    hi = x.astype(BF16)
    lo = (x - hi.astype(F32)).astype(BF16)
    return hi, lo


def _dot_x_sel(x, m):
    hi, lo = _split2(x)
    return _dot(hi, m) + _dot(lo, m)


def _dot_sel_x(m, x):
    hi, lo = _split2(x)
    return _dot(m, hi) + _dot(m, lo)


def _sigmoid(x):
    return 1.0 / (1.0 + jnp.exp(-x))


def _silu(x):
    return x * _sigmoid(x)


def _layernorm(y, g, b):
    mu = jnp.mean(y, axis=-1, keepdims=True)
    d = y - mu
    var = jnp.mean(d * d, axis=-1, keepdims=True)
    return d * lax.rsqrt(var + LN_EPS) * g + b


def _ada_kernel(c_ref, w_ref, b_ref, o_ref):
    c = c_ref[...]
    o_ref[0] = _dot_hi(_silu(c), w_ref[0]) + b_ref[0]


def _ada_mod(c, ada_w, ada_b):
    L, D, N6 = ada_w.shape
    B = c.shape[0]
    tn = N6 // 6
    out = pl.pallas_call(
        _ada_kernel,
        grid=(L, N6 // tn),
        in_specs=[pl.BlockSpec((B, D), lambda l, j: (0, 0)),
                  pl.BlockSpec((1, D, tn), lambda l, j: (l, 0, j)),
                  pl.BlockSpec((1, 1, tn), lambda l, j: (l, 0, j))],
        out_specs=pl.BlockSpec((1, B, tn), lambda l, j: (l, 0, j)),
        out_shape=jax.ShapeDtypeStruct((L, B, N6), F32),
        compiler_params=_cparams(("arbitrary", "arbitrary")),
        name="ada_mod",
    )(c, ada_w, ada_b.reshape(L, 1, N6))
    return out.reshape(L, B, 6, N6 // 6)


def _inproj_kernel(x_ref, mod_ref, w_ref, *o_refs, chunk, acts):
    m = mod_ref[0]
    h = (x_ref[0] * (1.0 + m[1:2]) + m[0:1]).astype(BF16)
    off = 0
    for o_ref, segs in zip(o_refs, acts):
        n = o_ref.shape[-1]
        for c0 in range(0, n, chunk):
            c1 = min(c0 + chunk, n)
            act = [f for start, f in segs if start <= c0][-1]
            y = _dot(h, w_ref[0, :, off + c0:off + c1])
            o_ref[0, :, c0:c1] = (y if act is None else act(y)).astype(o_ref.dtype)
        off += n


def _inproj(x, mod_l, w_in_bf, layer, widths, acts):
    B, S, D = x.shape
    N = w_in_bf.shape[-1]
    tm = min(TM_INPROJ, S)
    out_shape = [jax.ShapeDtypeStruct((B, S, n), BF16) for n in widths]
    out_specs = [pl.BlockSpec((1, tm, n), lambda b, i: (b, i, 0)) for n in widths]
    return pl.pallas_call(
        functools.partial(_inproj_kernel, chunk=512, acts=acts),
        grid=(B, S // tm),
        in_specs=[pl.BlockSpec((1, tm, D), lambda b, i: (b, i, 0)),
                  pl.BlockSpec((1, 6, D), lambda b, i: (b, 0, 0)),
                  pl.BlockSpec((1, D, N), lambda b, i: (layer, 0, 0), pipeline_mode=pl.Buffered(1))],
        out_specs=out_specs,
        out_shape=out_shape,
        compiler_params=_cparams(("parallel", "parallel")),
        name="in_proj",
    )(x, mod_l, w_in_bf)


def _rope_tables(S):
    t = np.arange(S)
    row = (t // GRID_W).astype(np.float64)
    col = (t % GRID_W).astype(np.float64)
    inv = ROPE_THETA ** (-np.arange(AXIAL_FREQS, dtype=np.float64) / AXIAL_FREQS)
    ang_r = (row[:, None].astype(np.float32) * inv.astype(np.float32)[None, :]).astype(np.float32)
    ang_c = (col[:, None].astype(np.float32) * inv.astype(np.float32)[None, :]).astype(np.float32)
    cos_h = np.concatenate([np.cos(ang_r), np.cos(ang_r), np.cos(ang_c), np.cos(ang_c)], -1)
    sin_h = np.concatenate([np.sin(ang_r), np.sin(ang_r), np.sin(ang_c), np.sin(ang_c)], -1)
    cos_t = np.tile(cos_h, (1, ATTN_HEADS)).astype(np.float32)
    sin_t = np.tile(sin_h, (1, ATTN_HEADS)).astype(np.float32)
    return cos_t, sin_t


def _head_consts():
    W = ATTN_HEADS * HEAD_DIM
    d = np.arange(W)
    bind = (d[:, None] // HEAD_DIM == d[None, :] // HEAD_DIM).astype(np.float32) / HEAD_DIM
    psw = np.zeros((W, W), np.float32)
    lowhalf = (d % AXIAL_DIM) < AXIAL_FREQS
    for j in d:
        if lowhalf[j]:
            psw[j + AXIAL_FREQS, j] = -1.0
        else:
            psw[j - AXIAL_FREQS, j] = 1.0
    return bind, psw


def _prep_kernel(pq_ref, pkv_ref, cos_ref, sin_ref, gq_ref, gk_ref, bind_ref, psw_ref,
                 qt_ref, k_ref, vt_ref):
    KW = KV_HEADS * HEAD_DIM
    cos = cos_ref[...]
    sin = sin_ref[...]

    def norm_rope(u, gain, w):
        ms = _dot_x_sel(u * u, bind_ref[:w, :w])
        un = u * lax.rsqrt(ms + RMS_EPS) * gain
        return un * cos[:, :w] + _dot_x_sel(un, psw_ref[:w, :w]) * sin[:, :w]

    q = norm_rope(pq_ref[0].astype(F32), gq_ref[...], ATTN_HEADS * HEAD_DIM)
    qt = (q * (HEAD_DIM ** -0.5 * LOG2_E)).T
    zeros = jnp.zeros((HEAD_DIM, qt.shape[1]), F32)
    for h in range(ATTN_HEADS):
        rows = [zeros] * KV_HEADS
        rows[h // GQA_GROUP] = qt[h * HEAD_DIM:(h + 1) * HEAD_DIM]
        qt_ref[0, h] = jnp.concatenate(rows, axis=0).astype(BF16)
    kv = pkv_ref[0]
    k_ref[0, 0] = norm_rope(kv[:, :KW].astype(F32), gk_ref[...], KW).astype(BF16)
    vt = kv[:, KW:].astype(F32).T
    tm = vt.shape[1]
    extra = (lax.broadcasted_iota(jnp.int32, (V_AUG_ROWS - HEAD_DIM, tm), 0) == 0).astype(F32)
    vt_ref[0, 0] = jnp.concatenate(
        [blk for g in range(KV_HEADS) for blk in (vt[g * HEAD_DIM:(g + 1) * HEAD_DIM], extra)],
        axis=0).astype(BF16)


def _attn_prep(pq, pkv, cos_t, sin_t, gq, gk, bind, psw):
    B, S, QW = pq.shape
    KW = KV_HEADS * HEAD_DIM
    tm = min(TK_ATTN, S)
    return pl.pallas_call(
        _prep_kernel,
        grid=(B, S // tm),
        in_specs=[pl.BlockSpec((1, tm, QW), lambda b, i: (b, i, 0)),
                  pl.BlockSpec((1, tm, 2 * KW), lambda b, i: (b, i, 0)),
                  pl.BlockSpec((tm, QW), lambda b, i: (i, 0)),
                  pl.BlockSpec((tm, QW), lambda b, i: (i, 0)),
                  pl.BlockSpec((1, QW), lambda b, i: (0, 0)),
                  pl.BlockSpec((1, KW), lambda b, i: (0, 0)),
                  pl.BlockSpec((QW, QW), lambda b, i: (0, 0)),
                  pl.BlockSpec((QW, QW), lambda b, i: (0, 0))],
        out_specs=[pl.BlockSpec((1, ATTN_HEADS, KW, tm), lambda b, i: (b, 0, 0, i)),
                   pl.BlockSpec((1, 1, tm, KW), lambda b, i: (b, i, 0, 0)),
                   pl.BlockSpec((1, 1, KV_HEADS * V_AUG_ROWS, tm), lambda b, i: (b, i, 0, 0))],
        out_shape=[jax.ShapeDtypeStruct((B, ATTN_HEADS, KW, S), BF16),
                   jax.ShapeDtypeStruct((B, S // tm, tm, KW), BF16),
                   jax.ShapeDtypeStruct((B, S // tm, KV_HEADS * V_AUG_ROWS, tm), BF16)],
        compiler_params=_cparams(("parallel", "parallel")),
        name="attn_prep",
    )(pq, pkv, cos_t, sin_t, gq, gk, bind, psw)


def _attn_kernel(qt_ref, k_ref, vt_ref, o_ref, st_scr, m_scr, acc_scr, *, nk):
    def scores_to(slot, c):
        kc = k_ref[0, c]
        for h in range(GQA_GROUP):
            st_scr[slot, h] = _dot(kc, qt_ref[0, h])

    def consume(slot, c):
        vc = vt_ref[0, c]
        for h in range(GQA_GROUP):
            st = st_scr[slot, h]
            m_old = m_scr[h]
            m_new = jnp.maximum(m_old, jnp.max(st, axis=0, keepdims=True))
            p = jnp.exp2(st - m_new).astype(BF16)
            acc_scr[h] = jnp.exp2(m_old - m_new) * acc_scr[h] + _dot(vc, p)
            m_scr[h] = m_new

    m_scr[...] = jnp.full(m_scr.shape, -jnp.inf, F32)
    acc_scr[...] = jnp.zeros(acc_scr.shape, F32)
    scores_to(0, 0)

    def step(nslot, cn, cslot, cc):
        kc = k_ref[0, cn]
        vc = vt_ref[0, cc]
        for h in range(GQA_GROUP):
            st_scr[nslot, h] = _dot(kc, qt_ref[0, h])
            st = st_scr[cslot, h]
            m_old = m_scr[h]
            m_new = jnp.maximum(m_old, jnp.max(st, axis=0, keepdims=True))
            p = jnp.exp2(st - m_new).astype(BF16)
            acc_scr[h] = jnp.exp2(m_old - m_new) * acc_scr[h] + _dot(vc, p)
            m_scr[h] = m_new

    def body(j, carry):
        c0 = 2 * j
        step(1, c0 + 1, 0, c0)
        step(0, c0 + 2, 1, c0 + 1)
        return carry
    lax.fori_loop(0, nk // 2 - 1, body, 0)
    step(1, nk - 1, 0, nk - 2)
    consume(1, nk - 1)

    outs = []
    for h in range(GQA_GROUP):
        acc = acc_scr[h]
        outs.append((acc[:HEAD_DIM] / acc[HEAD_DIM:HEAD_DIM + 1]).T)
    o_ref[0] = jnp.concatenate(outs, axis=-1).astype(o_ref.dtype)


def _attention(qt, k, vt):
    B, nk, tk, KW = k.shape
    S = nk * tk
    GW = GQA_GROUP * HEAD_DIM
    tq = min(TQ_ATTN, S)
    assert nk % 2 == 0
    scratch = [pltpu.VMEM((2, GQA_GROUP, tk, tq), F32), pltpu.VMEM((GQA_GROUP, 1, tq), F32),
               pltpu.VMEM((GQA_GROUP, V_AUG_ROWS, tq), F32)]
    return pl.pallas_call(
        functools.partial(_attn_kernel, nk=nk),
        scratch_shapes=scratch,
        grid=(B, KV_HEADS, S // tq),
        in_specs=[pl.BlockSpec((1, GQA_GROUP, KW, tq), lambda b, g, i: (b, g, 0, i)),
                  pl.BlockSpec((1, nk, tk, KW), lambda b, g, i: (b, 0, 0, 0)),
                  pl.BlockSpec((1, nk, V_AUG_ROWS, tk), lambda b, g, i: (b, 0, g, 0))],
        out_specs=pl.BlockSpec((1, tq, GW), lambda b, g, i: (b, i, g)),
        out_shape=jax.ShapeDtypeStruct((B, S, ATTN_HEADS * HEAD_DIM), BF16),
        compiler_params=_cparams(("parallel", "parallel", "parallel")),
        name="attention",
    )(qt, k, vt)


def _log1p(y):
    u = 1.0 + y
    return jnp.where(u == 1.0, y, jnp.log(u) * (y / (u - 1.0)))


def _softplus(z):
    return jnp.maximum(z, 0.0) + _log1p(jnp.exp(-jnp.abs(z)))


def _neg_expm1_2x(z, ez):
    return -jnp.tanh(z) * (ez * ez + 1.0)


def _gelu_tanh(x):
    return 0.5 * x * (1.0 + jnp.tanh(math.sqrt(2.0 / math.pi) * (x + 0.044715 * (x * x * x))))


def _lru_kernel(*refs, reverse, T):
    if reverse:
        xc_ref, wbd_ref, gb_ref, lam_ref, hf_ref, lg_ref, o_ref, carry_ref = refs
    else:
        (x_ref, xp_ref, xn_ref, cw_ref, cb_ref, wbd_ref, gb_ref, lam_ref,
         o_ref, xc_out_ref, carry_ref) = refs
    i = pl.program_id(1)

    @pl.when(i == 0)
    def _():
        carry_ref[...] = jnp.zeros_like(carry_ref)

    xc = xc_ref[0] if reverse else _lru_conv(x_ref, xp_ref, xn_ref, cw_ref, cb_ref, i, T)
    if not reverse:
        xc_out_ref[0] = xc
    _lru_scan(xc, wbd_ref, gb_ref, lam_ref, carry_ref, o_ref,
              (hf_ref, lg_ref) if reverse else None, reverse, T)


def _lru_conv(x_ref, xp_ref, xn_ref, cw_ref, cb_ref, tile, T):
    nt = pl.num_programs(1)
    W = x_ref.shape[-1]
    x = x_ref[0].astype(F32)
    prev = jnp.where(tile == 0, 0.0, xp_ref[0].astype(F32))
    nxt = jnp.where(tile == nt - 1, 0.0, xn_ref[0].astype(F32))
    row8 = lax.broadcasted_iota(jnp.int32, (V7X_SUBLANES, W), 0)

    def shifted(k):
        if k == 0:
            return x
        r = pltpu.roll(x, (-k) % T, axis=0)
        if k < 0:
            fill = pltpu.roll(prev, (-k) % V7X_SUBLANES, axis=0)
            head = jnp.where(row8 < -k, fill, r[:V7X_SUBLANES])
            return jnp.concatenate([head, r[V7X_SUBLANES:]], axis=0)
        fill = pltpu.roll(nxt, (-k) % V7X_SUBLANES, axis=0)
        tail = jnp.where(row8 >= V7X_SUBLANES - k, fill, r[T - V7X_SUBLANES:])
        return jnp.concatenate([r[:T - V7X_SUBLANES], tail], axis=0)

    cw = cw_ref[...]
    return cb_ref[...] + sum(shifted(j - CONV_LEFT) * cw[j:j + 1] for j in range(CONV_W))


def _lru_scan(xc, wbd_ref, gb_ref, lam_ref, carry_ref, o_ref, gate_refs, reverse, T):
    W = xc.shape[-1]
    gates = _dot(xc.astype(BF16), wbd_ref[0]) + gb_ref[0]
    r = _sigmoid(gates[:, :W])
    ig = _sigmoid(gates[:, W:])
    log_a = (-LRU_C) * r * _softplus(-lam_ref[0])
    a = jnp.exp(log_a)
    u = jnp.sqrt(_neg_expm1_2x(log_a, a)) * (ig * xc)

    row = lax.broadcasted_iota(jnp.int32, (T, W), 0)

    def neighbour(z, d, fill):
        if d % V7X_SUBLANES == 0:
            pad = jnp.full((d, W), fill, F32)
            return jnp.concatenate([z[d:], pad] if reverse else [pad, z[:T - d]], axis=0)
        if reverse:
            return jnp.where(row < T - d, pltpu.roll(z, T - d, axis=0), fill)
        return jnp.where(row >= d, pltpu.roll(z, d, axis=0), fill)

    A, U = a, u
    d = 1
    while d < T:
        U = A * neighbour(U, d, 0.0) + U
        A = A * neighbour(A, d, 1.0)
        d *= 2
    h = A * carry_ref[...] + U
    if reverse:
        hf_ref, lg_ref = gate_refs
        carry_ref[...] = h[0:1]
        o_ref[0] = ((hf_ref[0] + h) * lg_ref[0].astype(F32)).astype(o_ref.dtype)
    else:
        carry_ref[...] = h[T - 1:T]
        o_ref[0] = h


def _lru_pass(plru, conv_w, conv_b, wbd, gb, lam, fwd, reverse):
    B, S, W2 = plru.shape
    W = W2 // 2
    T = min(T_LRU, S)
    nt = S // T
    r8 = T // V7X_SUBLANES
    nb8 = S // V7X_SUBLANES
    d = 1 if reverse else 0

    def tmap(i):
        return nt - 1 - i if reverse else i

    tile_spec = pl.BlockSpec((1, T, W), lambda b, i: (b, tmap(i), 0))
    gate_specs = [pl.BlockSpec((1, W, 2 * W), lambda b, i: (d, 0, 0)),
                  pl.BlockSpec((1, 1, 2 * W), lambda b, i: (d, 0, 0)),
                  pl.BlockSpec((1, 1, W), lambda b, i: (d, 0, 0))]
    if reverse:
        hf, xc = fwd
        in_specs = [tile_spec] + gate_specs + [tile_spec, pl.BlockSpec((1, T, W), lambda b, i: (b, tmap(i), 1))]
        args = [xc, wbd, gb, lam, hf, plru]
        out_specs = tile_spec
        out_shape = jax.ShapeDtypeStruct((B, S, W), BF16)
    else:
        in_specs = [
            tile_spec,
            pl.BlockSpec((1, V7X_SUBLANES, W), lambda b, i: (b, jnp.maximum(i * r8 - 1, 0), 0)),
            pl.BlockSpec((1, V7X_SUBLANES, W), lambda b, i: (b, jnp.minimum((i + 1) * r8, nb8 - 1), 0)),
            pl.BlockSpec((CONV_W, W), lambda b, i: (0, 0)),
            pl.BlockSpec((1, W), lambda b, i: (0, 0)),
        ] + gate_specs
        args = [plru, plru, plru, conv_w, conv_b, wbd, gb, lam]
        out_specs = [tile_spec, tile_spec]
        out_shape = [jax.ShapeDtypeStruct((B, S, W), F32), jax.ShapeDtypeStruct((B, S, W), F32)]
    return pl.pallas_call(
        functools.partial(_lru_kernel, reverse=reverse, T=T),
        grid=(B, nt),
        in_specs=in_specs,
        out_specs=out_specs,
        out_shape=out_shape,
        scratch_shapes=[pltpu.VMEM((1, W), F32)],
        compiler_params=_cparams(("parallel", "arbitrary")),
        name="lru_bwd" if reverse else "lru_fwd",
    )(*args)


def _hgrn_consts(C, reverse):
    nl = int(round(math.log2(C)))
    t = np.arange(C)
    halves = [C >> (lvl + 1) for lvl in range(nl)]
    a_rows, masks, upper = [], [], []
    for m in halves:
        blk = t // (2 * m)
        up = (t % (2 * m)) >= m
        mid = blk * 2 * m + m
        r = t[None, :]
        masks.append((blk[:, None] == blk[None, :]) & up[:, None] & (~up)[None, :])
        if m < V7X_SUBLANES:
            a_up = (r >= mid[:, None]) & (r <= t[:, None])
            a_lo = (r > t[:, None]) & (r <= mid[:, None] - 1)
            a_rows.append(np.where(up[:, None], a_up, a_lo))
            upper.append(up)
    a_rows.append(t[None, :] <= t[:, None])
    a_rows.append(np.ones((V7X_SUBLANES, C), bool))
    masks.append(np.eye(C, dtype=bool))
    if reverse:
        a_rows = [a[::-1, ::-1] if a.shape[0] == C else a for a in a_rows]
        masks = [mm[::-1, ::-1] for mm in masks]
        upper = [u[::-1] for u in upper]
    amat = np.concatenate(a_rows, 0).astype(np.float32)
    msk = np.stack(masks, 0).astype(np.float32)
    upv = np.stack(upper, 0).astype(np.float32)[:, :, None]
    return halves, amat, msk, upv


def _hgrn_kernel(*refs, layer, reverse, C, CB, halves):
    if reverse:
        (hq_ref, hf_ref, hi_ref, lbl_ref, amat_ref, msk_ref, up_ref, of_ref, hg_ref, gain_ref,
         o_ref, st_ref) = refs
    else:
        (hq_ref, hf_ref, hi_ref, lbl_ref, amat_ref, msk_ref, up_ref, o_ref, st_ref) = refs
    H = HGRN_HEADS
    W = hq_ref.shape[-1]
    dk = W // H
    NL = len(halves)
    n_small = sum(1 for m in halves if m < V7X_SUBLANES)

    @pl.when(pl.program_id(1) == 0)
    def _():
        st_ref[...] = jnp.zeros_like(st_ref)

    lgt = lbl_ref[...]
    e = jnp.exp(lgt - jnp.max(lgt, axis=0, keepdims=True))
    p = e / jnp.sum(e, axis=0, keepdims=True)
    lb = jnp.zeros((1, W), F32)
    for j in range(1, layer + 1):
        lb = lb + p[j:j + 1]

    def gates_and_factors(j):
        rows = slice(j * C, (j + 1) * C)
        q = hq_ref[0, rows].astype(F32)
        f = lb + (1.0 - lb) * _sigmoid(hf_ref[0, rows].astype(F32))
        k = 1.0 - f
        g = jnp.log2(f)
        v = hi_ref[0, rows].astype(F32)

        sums = _dot_sel_x(amat_ref[...], g)
        b = sums[n_small * C:(n_small + 1) * C]
        tot = sums[(n_small + 1) * C:(n_small + 1) * C + 1]
        xs = []
        small = 0
        for m in halves:
            if m >= V7X_SUBLANES:
                shp = (C // (2 * m), 2 * m, W)
                b3 = b.reshape(shp)
                r0 = m if reverse else m - 1
                ref = b3[:, r0:r0 + 1, :]
                rowi = lax.broadcasted_iota(jnp.int32, shp, 1)
                qside = (rowi < m) if reverse else (rowi >= m)
                x3 = jnp.where(qside, q.reshape(shp), k.reshape(shp)) * jnp.exp2(
                    jnp.where(qside, b3 - ref, ref - b3))
                xs.append(x3.reshape(C, W).astype(BF16))
            else:
                xs.append((jnp.where(up_ref[small] > 0.5, q, k)
                           * jnp.exp2(sums[small * C:(small + 1) * C])).astype(BF16))
                small += 1
        return dict(rows=rows, v=v.astype(BF16), vt=v.T.astype(BF16), xs=xs, qk=q * k,
                    qb=(q * jnp.exp2(b)).astype(BF16), kr=(k * jnp.exp2(tot - b)).astype(BF16),
                    ex_tot=jnp.exp2(tot))

    def scores_and_state(a):
        scs, inters = [], []
        for h in range(H):
            sl = slice(h * dk, (h + 1) * dk)
            parts = [_dot_nt(a['xs'][l][:, sl], a['xs'][l][:, sl]) for l in range(NL)]
            st = st_ref[h]
            inters.append(_dot_nt(a['qb'][:, sl], st.astype(BF16)))
            st_ref[h] = st * a['ex_tot'][:, sl] + _dot(a['vt'][sl, :], a['kr'][:, sl])
            sc = msk_ref[NL] * jnp.sum(a['qk'][:, sl], axis=-1, keepdims=True)
            for l in range(NL):
                sc = sc + msk_ref[l] * parts[l]
            scs.append(sc.astype(BF16))
        return scs, inters

    def outputs(a, scs, inters):
        rows = a['rows']
        if reverse:
            of = of_ref[0, rows]
            hg = hg_ref[0, rows].astype(F32)
            gain = gain_ref[...]
        for h in range(H):
            sl = slice(h * dk, (h + 1) * dk)
            o_h = inters[h] + _dot(scs[h], a['v'][:, sl])
            if reverse:
                tot_o = of[:, sl] + o_h
                ms = jnp.mean(tot_o * tot_o, axis=-1, keepdims=True)
                o_ref[0, rows, sl] = (tot_o * lax.rsqrt(ms + RMS_EPS) * gain
                                      * hg[:, sl]).astype(o_ref.dtype)
            else:
                o_ref[0, rows, sl] = o_h

    order = list(range(CB - 1, -1, -1) if reverse else range(CB))
    cur = gates_and_factors(order[0])
    for n, j in enumerate(order):
        scs, inters = scores_and_state(cur)
        nxt = gates_and_factors(order[n + 1]) if n + 1 < CB else None
        outputs(cur, scs, inters)
        cur = nxt


def _hgrn_pass(phg, lb_logits, norm_gain, of, layer, reverse):
    B, S, W5 = phg.shape
    W = W5 // 5
    C = min(C_HGRN, S)
    CB = min(CB_HGRN, S // C)
    TB = CB * C
    nc = S // TB
    halves, amat, msk, upv = _hgrn_consts(C, reverse)
    L = lb_logits.shape[0]
    dk = W // HGRN_HEADS

    def cmap(i):
        return nc - 1 - i if reverse else i

    fcol = 2 if reverse else 1
    in_specs = [
        pl.BlockSpec((1, TB, W), lambda b, i: (b, cmap(i), 0)),
        pl.BlockSpec((1, TB, W), lambda b, i: (b, cmap(i), fcol)),
        pl.BlockSpec((1, TB, W), lambda b, i: (b, cmap(i), 3)),
        pl.BlockSpec((L, W), lambda b, i: (0, 0)),
        pl.BlockSpec(amat.shape, lambda b, i: (0, 0)),
        pl.BlockSpec(msk.shape, lambda b, i: (0, 0, 0)),
        pl.BlockSpec(upv.shape, lambda b, i: (0, 0, 0)),
    ]
    args = [phg, phg, phg, lb_logits, jnp.asarray(amat, BF16), jnp.asarray(msk, F32), jnp.asarray(upv, F32)]
    if reverse:
        in_specs += [pl.BlockSpec((1, TB, W), lambda b, i: (b, cmap(i), 0)),
                     pl.BlockSpec((1, TB, W), lambda b, i: (b, cmap(i), 4)),
                     pl.BlockSpec((1, dk), lambda b, i: (0, 0))]
        args += [of, phg, norm_gain.reshape(1, dk)]
    return pl.pallas_call(
        functools.partial(_hgrn_kernel, layer=layer, reverse=reverse, C=C, CB=CB, halves=tuple(halves)),
        grid=(B, nc),
        in_specs=in_specs,
        out_specs=pl.BlockSpec((1, TB, W), lambda b, i: (b, cmap(i), 0)),
        out_shape=jax.ShapeDtypeStruct((B, S, W), BF16 if reverse else F32),
        scratch_shapes=[pltpu.VMEM((HGRN_HEADS, dk, dk), F32)],
        compiler_params=_cparams(("parallel", "arbitrary")),
        name="hgrn_bwd" if reverse else "hgrn_fwd",
    )(*args)


def _merge_kernel(ya_ref, yl_ref, yh_ref, g0_ref, g1_ref, g2_ref, x_ref, mod_ref,
                  wa_ref, wl_ref, wh_ref, wo_ref, lng_ref, lnb_ref, wr_ref,
                  x1_ref, h2_ref, route_ref, *, alpha, n_groups, e_per):
    m = mod_ref[0]
    half = x_ref.shape[1] // 2
    halves = [slice(r0, r0 + half) for r0 in (0, half)]

    branch = [(_dot(ya_ref[0, rs], wa_ref[...]), _dot(yl_ref[0, rs], wl_ref[...]),
               _dot(yh_ref[0, rs], wh_ref[...])) for rs in halves]
    mixes = []
    for rs, (pa, pl_, ph) in zip(halves, branch):
        merged = (_sigmoid(g0_ref[0, rs].astype(F32)) * pa + _sigmoid(g1_ref[0, rs].astype(F32)) * pl_
                  + _sigmoid(g2_ref[0, rs].astype(F32)) * ph)
        mixes.append(_dot(merged.astype(BF16), wo_ref[...]))
    logit_halves = []
    for rs, mix in zip(halves, mixes):
        x1 = _layernorm(alpha * x_ref[0, rs] + (1.0 + m[2:3]) * mix, lng_ref[...], lnb_ref[...])
        x1_ref[0, rs] = x1
        h2 = x1 * (1.0 + m[4:5]) + m[3:4]
        h2_ref[0, rs] = h2
        xh, xl = _split2(h2)
        r = _dot(xh, wr_ref[...])
        logit_halves.append(r[:, :V7X_LANES] + r[:, V7X_LANES:] + _dot(xl, wr_ref[:, :V7X_LANES]))
    for rs, logits in zip(halves, logit_halves):
        route_ref[0, rs] = _route(logits, n_groups, e_per)


def _route(logits, n_groups, e_per):
    lane = lax.broadcasted_iota(jnp.int32, logits.shape, 1)
    big = jnp.int32(V7X_LANES)
    neg = jnp.float32(-jnp.inf)
    gl = jnp.where(lane < n_groups, logits, neg)
    gmax = jnp.max(gl, axis=-1, keepdims=True)
    gsel = jnp.min(jnp.where(gl == gmax, lane, big), axis=-1, keepdims=True)
    gp = 1.0 / jnp.sum(jnp.exp(gl - gmax), axis=-1, keepdims=True)
    lo = n_groups + gsel * e_per
    el = jnp.where((lane >= lo) & (lane < lo + e_per), logits, neg)
    m1 = jnp.max(el, axis=-1, keepdims=True)
    i1 = jnp.min(jnp.where(el == m1, lane, big), axis=-1, keepdims=True)
    el2 = jnp.where(lane == i1, neg, el)
    m2 = jnp.max(el2, axis=-1, keepdims=True)
    i2 = jnp.min(jnp.where(el2 == m2, lane, big), axis=-1, keepdims=True)
    z = jnp.sum(jnp.exp(el - m1), axis=-1, keepdims=True)
    p1 = 1.0 / z
    p2 = jnp.exp(m2 - m1) / z
    w1 = gp * (p1 / (p1 + p2))
    w2 = gp * (p2 / (p1 + p2))
    e1 = (i1 - n_groups).astype(F32)
    e2 = (i2 - n_groups).astype(F32)
    return jnp.where(lane == 0, w1, jnp.where(lane == 1, w2,
                     jnp.where(lane == 2, e1, jnp.where(lane == 3, e2, 0.0))))


def _merge(ya, yl, yh, pmg, x, mod_l, wa, wl, wh, wo, lng, lnb, wr, n_groups, e_per, alpha):
    B, S, D = x.shape
    tm = min(TM_MERGE, S)
    bw = ya.shape[-1]
    tok = lambda b, i: (b, i, 0)
    const = lambda b, i: (0, 0)
    return pl.pallas_call(
        functools.partial(_merge_kernel, alpha=alpha, n_groups=n_groups, e_per=e_per),
        grid=(B, S // tm),
        in_specs=[pl.BlockSpec((1, tm, bw), tok), pl.BlockSpec((1, tm, bw), tok), pl.BlockSpec((1, tm, bw), tok),
                  pl.BlockSpec((1, tm, D), lambda b, i: (b, i, 0)),
                  pl.BlockSpec((1, tm, D), lambda b, i: (b, i, 1)),
                  pl.BlockSpec((1, tm, D), lambda b, i: (b, i, 2)),
                  pl.BlockSpec((1, tm, D), tok),
                  pl.BlockSpec((1, 6, D), lambda b, i: (b, 0, 0)),
                  pl.BlockSpec((bw, D), const), pl.BlockSpec((bw, D), const), pl.BlockSpec((bw, D), const),
                  pl.BlockSpec((D, D), const),
                  pl.BlockSpec((1, D), const), pl.BlockSpec((1, D), const),
                  pl.BlockSpec((D, 2 * V7X_LANES), const)],
        out_specs=[pl.BlockSpec((1, tm, D), tok), pl.BlockSpec((1, tm, D), tok),
                   pl.BlockSpec((1, tm, V7X_LANES), tok)],
        out_shape=[jax.ShapeDtypeStruct((B, S, D), F32), jax.ShapeDtypeStruct((B, S, D), F32),
                   jax.ShapeDtypeStruct((B, S, V7X_LANES), F32)],
        compiler_params=_cparams(("parallel", "parallel")),
        name="merge",
    )(ya, yl, yh, pmg, pmg, pmg, x, mod_l, wa, wl, wh, wo, lng, lnb, wr)


def _dispatch_kernel(dest_ref, h_ref, xs_in, xs_hbm, sem, *, TM):
    del xs_in

    for r in range(TM):
        for a in range(TOP_K):
            pltpu.make_async_copy(h_ref.at[pl.ds(r, 1)],
                                  xs_hbm.at[pl.ds(dest_ref[a, 0, 0, r], 1)],
                                  sem).start(priority=a % 2)
    for a in range(TOP_K):
        pltpu.make_async_copy(h_ref, xs_hbm.at[pl.ds(0, TM)], sem).wait()


def _dispatch(h2, dest, xs_prev, S):
    N, D = h2.shape
    TM = min(TM_DISPATCH, S)
    nt = N // TM
    return pl.pallas_call(
        functools.partial(_dispatch_kernel, TM=TM),
        grid=(nt,),
        in_specs=[pl.BlockSpec((TOP_K, 1, 1, TM), lambda i: (0, i, 0, 0), memory_space=pltpu.SMEM),
                  pl.BlockSpec((TM, D), lambda i: (i, 0)),
                  pl.BlockSpec(memory_space=pl.ANY)],
        out_specs=pl.BlockSpec(memory_space=pl.ANY),
        out_shape=jax.ShapeDtypeStruct(xs_prev.shape, xs_prev.dtype),
        scratch_shapes=[pltpu.SemaphoreType.DMA(())],
        input_output_aliases={2: 0},
        compiler_params=_cparams(("arbitrary",)),
        name="moe_dispatch",
    )(dest.reshape(TOP_K, nt, 1, TM), h2, xs_prev)


def _expert_kernel(blk_e_ref, nused_ref, xs_ref, w13_ref, w2_ref, ys_ref, *, F):
    i = pl.program_id(0)

    @pl.when(i < nused_ref[0])
    def _():
        half = xs_ref.shape[0] // 2
        abs_ = [_dot(xs_ref[r0:r0 + half, :].astype(BF16), w13_ref[0, 0]) for r0 in (0, half)]
        for n, r0 in enumerate((0, half)):
            hmid = (_silu(abs_[n][:, :F]) * abs_[n][:, F:]).astype(BF16)
            ys_ref[r0:r0 + half, :] = _dot(hmid, w2_ref[0, 0])

    @pl.when(i >= nused_ref[0])
    def _():
        ys_ref[...] = jnp.zeros_like(ys_ref)


def _experts(xs, blk_e, nused, w13, w2, layer):
    P, D = xs.shape
    R = R_MOE
    nb = P // R
    F = w2.shape[-2]
    grid_spec = pltpu.PrefetchScalarGridSpec(
        num_scalar_prefetch=2,
        grid=(nb,),
        in_specs=[pl.BlockSpec((R, D), lambda i, be, nu: (i, 0)),
                  pl.BlockSpec((1, 1, D, 2 * F), lambda i, be, nu: (layer, be[i], 0, 0)),
                  pl.BlockSpec((1, 1, F, D), lambda i, be, nu: (layer, be[i], 0, 0))],
        out_specs=pl.BlockSpec((R, D), lambda i, be, nu: (i, 0)),
    )
    return pl.pallas_call(
        functools.partial(_expert_kernel, F=F),
        grid_spec=grid_spec,
        out_shape=jax.ShapeDtypeStruct((P, D), F32),
        compiler_params=_cparams(("arbitrary",)),
        name="moe_experts",
    )(blk_e, nused, xs, w13, w2)


def _combine_kernel(pos_ref, posn_ref, ys_hbm, route_ref, x_ref, mod_ref, lng_ref, lnb_ref, o_ref,
                    ybuf, sems, *, TC, alpha):
    i = pl.program_id(0)
    n = pl.num_programs(0)
    slot = i % 2

    def issue(p_ref, s):
        for r in range(TC):
            for a in range(TOP_K):
                pltpu.make_async_copy(ys_hbm.at[pl.ds(p_ref[a, 0, 0, r], 1)],
                                      ybuf.at[s, a, pl.ds(r, 1)], sems.at[s]).start(priority=a % 2)

    @pl.when(i == 0)
    def _():
        issue(pos_ref, 0)

    for s in range(2):
        @pl.when((i + 1 < n) & (slot == 1 - s))
        def _(s=s):
            issue(posn_ref, s)

    for a in range(TOP_K):
        pltpu.make_async_copy(ys_hbm.at[pl.ds(0, TC)], ybuf.at[slot, a], sems.at[slot]).wait()

    m = mod_ref[0]
    w = route_ref[...]
    ffn = w[:, 0:1] * ybuf[slot, 0]
    for a in range(1, TOP_K):
        ffn = ffn + w[:, a:a + 1] * ybuf[slot, a]
    o_ref[...] = _layernorm(alpha * x_ref[...] + (1.0 + m[5:6]) * ffn, lng_ref[...], lnb_ref[...])


def _combine(ys, dest, route, x1, mod_l, lng, lnb, S, alpha):
    N, D = x1.shape
    TC = min(TC_MOE, S)
    nt = N // TC
    per_b = S // TC
    pos = dest.reshape(TOP_K, nt, 1, TC)
    return pl.pallas_call(
        functools.partial(_combine_kernel, TC=TC, alpha=alpha),
        grid=(nt,),
        in_specs=[pl.BlockSpec((TOP_K, 1, 1, TC), lambda i: (0, i, 0, 0), memory_space=pltpu.SMEM),
                  pl.BlockSpec((TOP_K, 1, 1, TC), lambda i: (0, jnp.minimum(i + 1, nt - 1), 0, 0),
                               memory_space=pltpu.SMEM),
                  pl.BlockSpec(memory_space=pl.ANY),
                  pl.BlockSpec((TC, V7X_LANES), lambda i: (i, 0)),
                  pl.BlockSpec((TC, D), lambda i: (i, 0)),
                  pl.BlockSpec((1, 6, D), lambda i: (i // per_b, 0, 0)),
                  pl.BlockSpec((1, D), lambda i: (0, 0)),
                  pl.BlockSpec((1, D), lambda i: (0, 0))],
        out_specs=pl.BlockSpec((TC, D), lambda i: (i, 0)),
        out_shape=jax.ShapeDtypeStruct((N, D), F32),
        scratch_shapes=[pltpu.VMEM((2, TOP_K, TC, D), F32), pltpu.SemaphoreType.DMA((2,))],
        compiler_params=_cparams(("arbitrary",)),
        name="moe_combine",
    )(pos, pos, ys, route, x1, mod_l, lng, lnb)


def _dispatch_plan(route, n_experts, R):
    lanes = jnp.arange(n_experts, dtype=jnp.int32)
    hots = [route[:, TOP_K + a].astype(jnp.int32)[:, None] == lanes for a in range(TOP_K)]
    hot = sum(h.astype(jnp.int32) for h in hots)
    csum = jnp.cumsum(hot, axis=0)
    counts = csum[-1]
    pcounts = (counts + R - 1) // R * R
    pends = jnp.cumsum(pcounts)
    pstarts = pends - pcounts
    base = pstarts[None, :] + csum - hot
    dest = jnp.stack([jnp.sum(jnp.where(h, base, 0), axis=1) for h in hots], 0).astype(jnp.int32)
    nb = (route.shape[0] * TOP_K + n_experts * R) // R
    starts = jnp.arange(nb, dtype=jnp.int32) * R
    blk_e = jnp.minimum((pends[None, :] <= starts[:, None]).astype(jnp.int32).sum(1), n_experts - 1)
    nused = (pends[-1:] // R).astype(jnp.int32)
    return blk_e, nused, dest


def _prep_weights(p):
    L = p['w_in'].shape[0]
    W = p['lru_conv_w'].shape[-1]
    nblk, bs = p['lru_wa'].shape[2], p['lru_wa'].shape[3]

    def blockdiag(w):
        eye = jnp.eye(nblk, dtype=w.dtype)
        return jnp.einsum('ldnij,nm->ldnimj', w, eye).reshape(L, 2, W, W)

    G, E = p['router_expert_w'].shape[1], p['router_expert_w'].shape[3]
    D = p['w_in'].shape[1]
    wr = jnp.concatenate([p['router_group_w'],
                          p['router_expert_w'].transpose(0, 2, 1, 3).reshape(L, D, G * E)], -1)
    wr = jnp.pad(wr, ((0, 0), (0, 0), (0, V7X_LANES - wr.shape[-1])))
    wr_hi = wr.astype(BF16)
    wr = jnp.concatenate([wr_hi, (wr - wr_hi.astype(F32)).astype(BF16)], -1)
    return dict(
        w_in=p['w_in'].astype(BF16),
        gq=jnp.tile(p['attn_q_gain'], (1, ATTN_HEADS))[:, None, :],
        gk=jnp.tile(p['attn_k_gain'], (1, KV_HEADS))[:, None, :],
        w_attn_o=p['w_attn_o'].astype(BF16),
        conv_w=p['lru_conv_w'], conv_b=p['lru_conv_b'][:, None, :],
        wbd=jnp.concatenate([blockdiag(p['lru_wa']), blockdiag(p['lru_wi'])], -1).astype(BF16),
        gb=jnp.concatenate([p['lru_ba'], p['lru_bi']], -1)[:, :, None, :],
        lam=p['lru_lambda'][:, :, None, :],
        w_lru_o=p['w_lru_o'].astype(BF16),
        w_hgrn_o=p['w_hgrn_o'].astype(BF16),
        w_out=p['w_out'].astype(BF16),
        ln1_g=p['ln1_g'][:, None, :], ln1_b=p['ln1_b'][:, None, :],
        ln2_g=p['ln2_g'][:, None, :], ln2_b=p['ln2_b'][:, None, :],
        wr=wr, n_groups=G, e_per=E,
        w13=p['expert_w13'].astype(BF16), w2=p['expert_w2'].astype(BF16),
    )


def _trunk(x, c, p, wp):
    B, S, D = x.shape
    L = p['w_in'].shape[0]
    alpha = (2 * L) ** 0.25
    QW = ATTN_HEADS * HEAD_DIM
    KW = KV_HEADS * HEAD_DIM
    LW = p['lru_conv_w'].shape[-1]
    HW = p['hgrn_lb_logits'].shape[-1]
    widths = (QW, 2 * KW, 2 * LW, 5 * HW, N_BRANCH * D)
    assert sum(widths) == p['w_in'].shape[-1]
    acts = (((0, None),), ((0, None),), ((0, None), (LW, _gelu_tanh)),
            ((0, _silu), (HW, None), (4 * HW, _silu)), ((0, None),))
    n_experts = wp['n_groups'] * wp['e_per']

    mod = _ada_mod(c, p['ada_w'], p['ada_b'])
    cos_t, sin_t = _rope_tables(S)
    bind, psw = _head_consts()
    cos_t, sin_t = jnp.asarray(cos_t), jnp.asarray(sin_t)
    bind, psw = jnp.asarray(bind, BF16), jnp.asarray(psw, BF16)

    xs = None
    for l in range(L):
        pq, pkv, plru, phg, pmg = _inproj(x, mod[l], wp['w_in'], l, widths, acts)
        qt, k, vt = _attn_prep(pq, pkv, cos_t, sin_t, wp['gq'][l], wp['gk'][l], bind, psw)
        ya = _attention(qt, k, vt)
        lru_args = (plru, wp['conv_w'][l], wp['conv_b'][l], wp['wbd'][l], wp['gb'][l], wp['lam'][l])
        hf = _lru_pass(*lru_args, None, reverse=False)
        yl = _lru_pass(*lru_args, hf, reverse=True)
        of = _hgrn_pass(phg, p['hgrn_lb_logits'], p['hgrn_norm_gain'][l], None, l, reverse=False)
        yh = _hgrn_pass(phg, p['hgrn_lb_logits'], p['hgrn_norm_gain'][l], of, l, reverse=True)
        x1, h2, route = _merge(ya, yl, yh, pmg, x, mod[l], wp['w_attn_o'][l], wp['w_lru_o'][l],
                               wp['w_hgrn_o'][l], wp['w_out'][l], wp['ln1_g'][l], wp['ln1_b'][l],
                               wp['wr'][l], wp['n_groups'], wp['e_per'], alpha)
        N = B * S
        route = route.reshape(N, V7X_LANES)
        blk_e, nused, dest = _dispatch_plan(route, n_experts, R_MOE)
        if xs is None:
            xs = jnp.zeros((N * TOP_K + n_experts * R_MOE, D), F32)
        xs = _dispatch(h2.reshape(N, D), dest, xs, S)
        ys = _experts(xs, blk_e, nused, wp['w13'], wp['w2'], l)
        x = _combine(ys, dest, route, x1.reshape(N, D), mod[l], wp['ln2_g'][l], wp['ln2_b'][l], S,
                     alpha).reshape(B, S, D)
    return x


def kernel(x_prompt, x_sample, c_prompt, c_sample, ada_w, ada_b, w_in, attn_q_gain, attn_k_gain, w_attn_o, lru_conv_w, lru_conv_b, lru_wa, lru_ba, lru_wi, lru_bi, lru_lambda, w_lru_o, hgrn_lb_logits, hgrn_norm_gain, w_hgrn_o, w_out, ln1_g, ln1_b, router_group_w, router_expert_w, expert_w13, expert_w2, ln2_g, ln2_b):
    p = dict(ada_w=ada_w, ada_b=ada_b, w_in=w_in, attn_q_gain=attn_q_gain, attn_k_gain=attn_k_gain,
             w_attn_o=w_attn_o, lru_conv_w=lru_conv_w, lru_conv_b=lru_conv_b, lru_wa=lru_wa, lru_ba=lru_ba,
             lru_wi=lru_wi, lru_bi=lru_bi, lru_lambda=lru_lambda, w_lru_o=w_lru_o,
             hgrn_lb_logits=hgrn_lb_logits, hgrn_norm_gain=hgrn_norm_gain, w_hgrn_o=w_hgrn_o, w_out=w_out,
             ln1_g=ln1_g, ln1_b=ln1_b, router_group_w=router_group_w, router_expert_w=router_expert_w,
             expert_w13=expert_w13, expert_w2=expert_w2, ln2_g=ln2_g, ln2_b=ln2_b)
    wp = _prep_weights(p)
    return (_trunk(x_prompt, c_prompt, p, wp), _trunk(x_sample, c_sample, p, wp))
```

```python
import functools
import math

import numpy as np
import jax
import jax.numpy as jnp
from jax import lax
from jax.experimental import pallas as pl
from jax.experimental.pallas import tpu as pltpu

F32 = jnp.float32
BF16 = jnp.bfloat16

GRID_W = 64
CONV_W = 4
CONV_LEFT = 2
LRU_C = 8.0
ATTN_HEADS = 8
KV_HEADS = 2
HEAD_DIM = 64
GQA_GROUP = ATTN_HEADS // KV_HEADS
AXIAL_DIM = HEAD_DIM // 2
AXIAL_FREQS = AXIAL_DIM // 2
ROPE_THETA = 10000.0
HGRN_HEADS = 4
N_BRANCH = 3
TOP_K = 2
LN_EPS = 1e-5
RMS_EPS = 1e-6
LOG2_E = 1.4426950408889634

V7X_LANES = 128
V7X_SUBLANES = 8
V7X_VMEM_LIMIT_BYTES = 56 * 1024 * 1024

TM_INPROJ = 512
V_AUG_ROWS = HEAD_DIM + 16
TQ_ATTN = 512
T_LRU = 256
C_HGRN = 128
CB_HGRN = 8
TK_ATTN = 512
TM_MERGE = 512
R_MOE = 512
TM_DISPATCH = 256
TC_MOE = 256


def _cparams(sem):
    return pltpu.CompilerParams(dimension_semantics=sem, vmem_limit_bytes=V7X_VMEM_LIMIT_BYTES)


def _dot(a, b):
    return jnp.dot(a, b, preferred_element_type=F32)


def _dot_nt(a, b):
    return lax.dot_general(a, b, (((1,), (1,)), ((), ())), preferred_element_type=F32)


def _dot_hi(a, b):
    return jnp.dot(a, b, preferred_element_type=F32, precision=lax.Precision.HIGHEST)


def _split2(x):
    hi = x.astype(BF16)
    lo = (x - hi.astype(F32)).astype(BF16)
    return hi, lo


def _dot_x_sel(x, m):
    hi, lo = _split2(x)
    return _dot(hi, m) + _dot(lo, m)


def _dot_sel_x(m, x):
    hi, lo = _split2(x)
    return _dot(m, hi) + _dot(m, lo)


def _sigmoid(x):
    return 1.0 / (1.0 + jnp.exp(-x))


def _silu(x):
    return x * _sigmoid(x)


def _layernorm(y, g, b):
    mu = jnp.mean(y, axis=-1, keepdims=True)
    d = y - mu
    var = jnp.mean(d * d, axis=-1, keepdims=True)
    return d * lax.rsqrt(var + LN_EPS) * g + b


def _ada_kernel(c_ref, w_ref, b_ref, o_ref):
    c = c_ref[...]
    o_ref[0] = _dot_hi(_silu(c), w_ref[0]) + b_ref[0]


def _ada_mod(c, ada_w, ada_b):
    L, D, N6 = ada_w.shape
    B = c.shape[0]
    tn = N6 // 6
    out = pl.pallas_call(
        _ada_kernel,
        grid=(L, N6 // tn),
        in_specs=[pl.BlockSpec((B, D), lambda l, j: (0, 0)),
                  pl.BlockSpec((1, D, tn), lambda l, j: (l, 0, j)),
                  pl.BlockSpec((1, 1, tn), lambda l, j: (l, 0, j))],
        out_specs=pl.BlockSpec((1, B, tn), lambda l, j: (l, 0, j)),
        out_shape=jax.ShapeDtypeStruct((L, B, N6), F32),
        compiler_params=_cparams(("arbitrary", "arbitrary")),
        name="ada_mod",
    )(c, ada_w, ada_b.reshape(L, 1, N6))
    return out.reshape(L, B, 6, N6 // 6)


def _inproj_kernel(x_ref, mod_ref, w_ref, *o_refs, chunk, acts):
    m = mod_ref[0]
    h = (x_ref[0] * (1.0 + m[1:2]) + m[0:1]).astype(BF16)
    off = 0
    for o_ref, segs in zip(o_refs, acts):
        n = o_ref.shape[-1]
        for c0 in range(0, n, chunk):
            c1 = min(c0 + chunk, n)
            act = [f for start, f in segs if start <= c0][-1]
            y = _dot(h, w_ref[0, :, off + c0:off + c1])
            o_ref[0, :, c0:c1] = (y if act is None else act(y)).astype(o_ref.dtype)
        off += n


def _inproj(x, mod_l, w_in_bf, layer, widths, acts):
    B, S, D = x.shape
    N = w_in_bf.shape[-1]
    tm = min(TM_INPROJ, S)
    out_shape = [jax.ShapeDtypeStruct((B, S, n), BF16) for n in widths]
    out_specs = [pl.BlockSpec((1, tm, n), lambda b, i: (b, i, 0)) for n in widths]
    return pl.pallas_call(
        functools.partial(_inproj_kernel, chunk=512, acts=acts),
        grid=(B, S // tm),
        in_specs=[pl.BlockSpec((1, tm, D), lambda b, i: (b, i, 0)),
                  pl.BlockSpec((1, 6, D), lambda b, i: (b, 0, 0)),
                  pl.BlockSpec((1, D, N), lambda b, i: (layer, 0, 0), pipeline_mode=pl.Buffered(1))],
        out_specs=out_specs,
        out_shape=out_shape,
        compiler_params=_cparams(("parallel", "parallel")),
        name="in_proj",
    )(x, mod_l, w_in_bf)


def _rope_tables(S):
    t = np.arange(S)
    row = (t // GRID_W).astype(np.float64)
    col = (t % GRID_W).astype(np.float64)
    inv = ROPE_THETA ** (-np.arange(AXIAL_FREQS, dtype=np.float64) / AXIAL_FREQS)
    ang_r = (row[:, None].astype(np.float32) * inv.astype(np.float32)[None, :]).astype(np.float32)
    ang_c = (col[:, None].astype(np.float32) * inv.astype(np.float32)[None, :]).astype(np.float32)
    cos_h = np.concatenate([np.cos(ang_r), np.cos(ang_r), np.cos(ang_c), np.cos(ang_c)], -1)
    sin_h = np.concatenate([np.sin(ang_r), np.sin(ang_r), np.sin(ang_c), np.sin(ang_c)], -1)
    cos_t = np.tile(cos_h, (1, ATTN_HEADS)).astype(np.float32)
    sin_t = np.tile(sin_h, (1, ATTN_HEADS)).astype(np.float32)
    return cos_t, sin_t


def _head_consts():
    W = ATTN_HEADS * HEAD_DIM
    d = np.arange(W)
    bind = (d[:, None] // HEAD_DIM == d[None, :] // HEAD_DIM).astype(np.float32) / HEAD_DIM
    psw = np.zeros((W, W), np.float32)
    lowhalf = (d % AXIAL_DIM) < AXIAL_FREQS
    for j in d:
        if lowhalf[j]:
            psw[j + AXIAL_FREQS, j] = -1.0
        else:
            psw[j - AXIAL_FREQS, j] = 1.0
    return bind, psw


def _prep_kernel(pq_ref, pkv_ref, cos_ref, sin_ref, gq_ref, gk_ref, bind_ref, psw_ref,
                 qt_ref, k_ref, vt_ref):
    KW = KV_HEADS * HEAD_DIM
    cos = cos_ref[...]
    sin = sin_ref[...]

    def norm_rope(u, gain, w):
        ms = _dot_x_sel(u * u, bind_ref[:w, :w])
        un = u * lax.rsqrt(ms + RMS_EPS) * gain
        return un * cos[:, :w] + _dot_x_sel(un, psw_ref[:w, :w]) * sin[:, :w]

    q = norm_rope(pq_ref[0].astype(F32), gq_ref[...], ATTN_HEADS * HEAD_DIM)
    qt = (q * (HEAD_DIM ** -0.5 * LOG2_E)).T
    zeros = jnp.zeros((HEAD_DIM, qt.shape[1]), F32)
    for h in range(ATTN_HEADS):
        rows = [zeros] * KV_HEADS
        rows[h // GQA_GROUP] = qt[h * HEAD_DIM:(h + 1) * HEAD_DIM]
        qt_ref[0, h] = jnp.concatenate(rows, axis=0).astype(BF16)
    kv = pkv_ref[0]
    k_ref[0, 0] = norm_rope(kv[:, :KW].astype(F32), gk_ref[...], KW).astype(BF16)
    vt = kv[:, KW:].astype(F32).T
    tm = vt.shape[1]
    extra = (lax.broadcasted_iota(jnp.int32, (V_AUG_ROWS - HEAD_DIM, tm), 0) == 0).astype(F32)
    vt_ref[0, 0] = jnp.concatenate(
        [blk for g in range(KV_HEADS) for blk in (vt[g * HEAD_DIM:(g + 1) * HEAD_DIM], extra)],
        axis=0).astype(BF16)


def _attn_prep(pq, pkv, cos_t, sin_t, gq, gk, bind, psw):
    B, S, QW = pq.shape
    KW = KV_HEADS * HEAD_DIM
    tm = min(TK_ATTN, S)
    return pl.pallas_call(
        _prep_kernel,
        grid=(B, S // tm),
        in_specs=[pl.BlockSpec((1, tm, QW), lambda b, i: (b, i, 0)),
                  pl.BlockSpec((1, tm, 2 * KW), lambda b, i: (b, i, 0)),
                  pl.BlockSpec((tm, QW), lambda b, i: (i, 0)),
                  pl.BlockSpec((tm, QW), lambda b, i: (i, 0)),
                  pl.BlockSpec((1, QW), lambda b, i: (0, 0)),
                  pl.BlockSpec((1, KW), lambda b, i: (0, 0)),
                  pl.BlockSpec((QW, QW), lambda b, i: (0, 0)),
                  pl.BlockSpec((QW, QW), lambda b, i: (0, 0))],
        out_specs=[pl.BlockSpec((1, ATTN_HEADS, KW, tm), lambda b, i: (b, 0, 0, i)),
                   pl.BlockSpec((1, 1, tm, KW), lambda b, i: (b, i, 0, 0)),
                   pl.BlockSpec((1, 1, KV_HEADS * V_AUG_ROWS, tm), lambda b, i: (b, i, 0, 0))],
        out_shape=[jax.ShapeDtypeStruct((B, ATTN_HEADS, KW, S), BF16),
                   jax.ShapeDtypeStruct((B, S // tm, tm, KW), BF16),
                   jax.ShapeDtypeStruct((B, S // tm, KV_HEADS * V_AUG_ROWS, tm), BF16)],
        compiler_params=_cparams(("parallel", "parallel")),
        name="attn_prep",
    )(pq, pkv, cos_t, sin_t, gq, gk, bind, psw)


def _attn_kernel(qt_ref, k_ref, vt_ref, o_ref, st_scr, m_scr, acc_scr, *, nk):
    def scores_to(slot, c):
        kc = k_ref[0, c]
        for h in range(GQA_GROUP):
            st_scr[slot, h] = _dot(kc, qt_ref[0, h])

    def consume(slot, c):
        vc = vt_ref[0, c]
        for h in range(GQA_GROUP):
            st = st_scr[slot, h]
            m_old = m_scr[h]
            m_new = jnp.maximum(m_old, jnp.max(st, axis=0, keepdims=True))
            p = jnp.exp2(st - m_new).astype(BF16)
            acc_scr[h] = jnp.exp2(m_old - m_new) * acc_scr[h] + _dot(vc, p)
            m_scr[h] = m_new

    m_scr[...] = jnp.full(m_scr.shape, -jnp.inf, F32)
    acc_scr[...] = jnp.zeros(acc_scr.shape, F32)
    scores_to(0, 0)

    def step(nslot, cn, cslot, cc):
        kc = k_ref[0, cn]
        vc = vt_ref[0, cc]
        for h in range(GQA_GROUP):
            st_scr[nslot, h] = _dot(kc, qt_ref[0, h])
            st = st_scr[cslot, h]
            m_old = m_scr[h]
            m_new = jnp.maximum(m_old, jnp.max(st, axis=0, keepdims=True))
            p = jnp.exp2(st - m_new).astype(BF16)
            acc_scr[h] = jnp.exp2(m_old - m_new) * acc_scr[h] + _dot(vc, p)
            m_scr[h] = m_new

    def body(j, carry):
        c0 = 2 * j
        step(1, c0 + 1, 0, c0)
        step(0, c0 + 2, 1, c0 + 1)
        return carry
    lax.fori_loop(0, nk // 2 - 1, body, 0)
    step(1, nk - 1, 0, nk - 2)
    consume(1, nk - 1)

    outs = []
    for h in range(GQA_GROUP):
        acc = acc_scr[h]
        outs.append((acc[:HEAD_DIM] / acc[HEAD_DIM:HEAD_DIM + 1]).T)
    o_ref[0] = jnp.concatenate(outs, axis=-1).astype(o_ref.dtype)


def _attention(qt, k, vt):
    B, nk, tk, KW = k.shape
    S = nk * tk
    GW = GQA_GROUP * HEAD_DIM
    tq = min(TQ_ATTN, S)
    assert nk % 2 == 0
    scratch = [pltpu.VMEM((2, GQA_GROUP, tk, tq), F32), pltpu.VMEM((GQA_GROUP, 1, tq), F32),
               pltpu.VMEM((GQA_GROUP, V_AUG_ROWS, tq), F32)]
    return pl.pallas_call(
        functools.partial(_attn_kernel, nk=nk),
        scratch_shapes=scratch,
        grid=(B, KV_HEADS, S // tq),
        in_specs=[pl.BlockSpec((1, GQA_GROUP, KW, tq), lambda b, g, i: (b, g, 0, i)),
                  pl.BlockSpec((1, nk, tk, KW), lambda b, g, i: (b, 0, 0, 0)),
                  pl.BlockSpec((1, nk, V_AUG_ROWS, tk), lambda b, g, i: (b, 0, g, 0))],
        out_specs=pl.BlockSpec((1, tq, GW), lambda b, g, i: (b, i, g)),
        out_shape=jax.ShapeDtypeStruct((B, S, ATTN_HEADS * HEAD_DIM), BF16),
        compiler_params=_cparams(("parallel", "parallel", "parallel")),
        name="attention",
    )(qt, k, vt)


def _log1p(y):
    u = 1.0 + y
    return jnp.where(u == 1.0, y, jnp.log(u) * (y / (u - 1.0)))


def _softplus(z):
    return jnp.maximum(z, 0.0) + _log1p(jnp.exp(-jnp.abs(z)))


def _neg_expm1_2x(z, ez):
    return -jnp.tanh(z) * (ez * ez + 1.0)


def _gelu_tanh(x):
    return 0.5 * x * (1.0 + jnp.tanh(math.sqrt(2.0 / math.pi) * (x + 0.044715 * (x * x * x))))


def _lru_kernel(*refs, reverse, T):
    if reverse:
        xc_ref, wbd_ref, gb_ref, lam_ref, hf_ref, lg_ref, o_ref, carry_ref = refs
    else:
        (x_ref, xp_ref, xn_ref, cw_ref, cb_ref, wbd_ref, gb_ref, lam_ref,
         o_ref, xc_out_ref, carry_ref) = refs
    i = pl.program_id(1)

    @pl.when(i == 0)
    def _():
        carry_ref[...] = jnp.zeros_like(carry_ref)

    xc = xc_ref[0] if reverse else _lru_conv(x_ref, xp_ref, xn_ref, cw_ref, cb_ref, i, T)
    if not reverse:
        xc_out_ref[0] = xc
    _lru_scan(xc, wbd_ref, gb_ref, lam_ref, carry_ref, o_ref,
              (hf_ref, lg_ref) if reverse else None, reverse, T)


def _lru_conv(x_ref, xp_ref, xn_ref, cw_ref, cb_ref, tile, T):
    nt = pl.num_programs(1)
    W = x_ref.shape[-1]
    x = x_ref[0].astype(F32)
    prev = jnp.where(tile == 0, 0.0, xp_ref[0].astype(F32))
    nxt = jnp.where(tile == nt - 1, 0.0, xn_ref[0].astype(F32))
    row8 = lax.broadcasted_iota(jnp.int32, (V7X_SUBLANES, W), 0)

    def shifted(k):
        if k == 0:
            return x
        r = pltpu.roll(x, (-k) % T, axis=0)
        if k < 0:
            fill = pltpu.roll(prev, (-k) % V7X_SUBLANES, axis=0)
            head = jnp.where(row8 < -k, fill, r[:V7X_SUBLANES])
            return jnp.concatenate([head, r[V7X_SUBLANES:]], axis=0)
        fill = pltpu.roll(nxt, (-k) % V7X_SUBLANES, axis=0)
        tail = jnp.where(row8 >= V7X_SUBLANES - k, fill, r[T - V7X_SUBLANES:])
        return jnp.concatenate([r[:T - V7X_SUBLANES], tail], axis=0)

    cw = cw_ref[...]
    return cb_ref[...] + sum(shifted(j - CONV_LEFT) * cw[j:j + 1] for j in range(CONV_W))


def _lru_scan(xc, wbd_ref, gb_ref, lam_ref, carry_ref, o_ref, gate_refs, reverse, T):
    W = xc.shape[-1]
    gates = _dot(xc.astype(BF16), wbd_ref[0]) + gb_ref[0]
    r = _sigmoid(gates[:, :W])
    ig = _sigmoid(gates[:, W:])
    log_a = (-LRU_C) * r * _softplus(-lam_ref[0])
    a = jnp.exp(log_a)
    u = jnp.sqrt(_neg_expm1_2x(log_a, a)) * (ig * xc)

    row = lax.broadcasted_iota(jnp.int32, (T, W), 0)

    def neighbour(z, d, fill):
        if d % V7X_SUBLANES == 0:
            pad = jnp.full((d, W), fill, F32)
            return jnp.concatenate([z[d:], pad] if reverse else [pad, z[:T - d]], axis=0)
        if reverse:
            return jnp.where(row < T - d, pltpu.roll(z, T - d, axis=0), fill)
        return jnp.where(row >= d, pltpu.roll(z, d, axis=0), fill)

    A, U = a, u
    d = 1
    while d < T:
        U = A * neighbour(U, d, 0.0) + U
        A = A * neighbour(A, d, 1.0)
        d *= 2
    h = A * carry_ref[...] + U
    if reverse:
        hf_ref, lg_ref = gate_refs
        carry_ref[...] = h[0:1]
        o_ref[0] = ((hf_ref[0] + h) * lg_ref[0].astype(F32)).astype(o_ref.dtype)
    else:
        carry_ref[...] = h[T - 1:T]
        o_ref[0] = h


def _lru_pass(plru, conv_w, conv_b, wbd, gb, lam, fwd, reverse):
    B, S, W2 = plru.shape
    W = W2 // 2
    T = min(T_LRU, S)
    nt = S // T
    r8 = T // V7X_SUBLANES
    nb8 = S // V7X_SUBLANES
    d = 1 if reverse else 0

    def tmap(i):
        return nt - 1 - i if reverse else i

    tile_spec = pl.BlockSpec((1, T, W), lambda b, i: (b, tmap(i), 0))
    gate_specs = [pl.BlockSpec((1, W, 2 * W), lambda b, i: (d, 0, 0)),
                  pl.BlockSpec((1, 1, 2 * W), lambda b, i: (d, 0, 0)),
                  pl.BlockSpec((1, 1, W), lambda b, i: (d, 0, 0))]
    if reverse:
        hf, xc = fwd
        in_specs = [tile_spec] + gate_specs + [tile_spec, pl.BlockSpec((1, T, W), lambda b, i: (b, tmap(i), 1))]
        args = [xc, wbd, gb, lam, hf, plru]
        out_specs = tile_spec
        out_shape = jax.ShapeDtypeStruct((B, S, W), BF16)
    else:
        in_specs = [
            tile_spec,
            pl.BlockSpec((1, V7X_SUBLANES, W), lambda b, i: (b, jnp.maximum(i * r8 - 1, 0), 0)),
            pl.BlockSpec((1, V7X_SUBLANES, W), lambda b, i: (b, jnp.minimum((i + 1) * r8, nb8 - 1), 0)),
            pl.BlockSpec((CONV_W, W), lambda b, i: (0, 0)),
            pl.BlockSpec((1, W), lambda b, i: (0, 0)),
        ] + gate_specs
        args = [plru, plru, plru, conv_w, conv_b, wbd, gb, lam]
        out_specs = [tile_spec, tile_spec]
        out_shape = [jax.ShapeDtypeStruct((B, S, W), F32), jax.ShapeDtypeStruct((B, S, W), F32)]
    return pl.pallas_call(
        functools.partial(_lru_kernel, reverse=reverse, T=T),
        grid=(B, nt),
        in_specs=in_specs,
        out_specs=out_specs,
        out_shape=out_shape,
        scratch_shapes=[pltpu.VMEM((1, W), F32)],
        compiler_params=_cparams(("parallel", "arbitrary")),
        name="lru_bwd" if reverse else "lru_fwd",
    )(*args)


def _hgrn_consts(C, reverse):
    nl = int(round(math.log2(C)))
    t = np.arange(C)
    halves = [C >> (lvl + 1) for lvl in range(nl)]
    a_rows, masks, upper = [], [], []
    for m in halves:
        blk = t // (2 * m)
        up = (t % (2 * m)) >= m
        mid = blk * 2 * m + m
        r = t[None, :]
        masks.append((blk[:, None] == blk[None, :]) & up[:, None] & (~up)[None, :])
        if m < V7X_SUBLANES:
            a_up = (r >= mid[:, None]) & (r <= t[:, None])
            a_lo = (r > t[:, None]) & (r <= mid[:, None] - 1)
            a_rows.append(np.where(up[:, None], a_up, a_lo))
            upper.append(up)
    a_rows.append(t[None, :] <= t[:, None])
    a_rows.append(np.ones((V7X_SUBLANES, C), bool))
    masks.append(np.eye(C, dtype=bool))
    if reverse:
        a_rows = [a[::-1, ::-1] if a.shape[0] == C else a for a in a_rows]
        masks = [mm[::-1, ::-1] for mm in masks]
        upper = [u[::-1] for u in upper]
    amat = np.concatenate(a_rows, 0).astype(np.float32)
    msk = np.stack(masks, 0).astype(np.float32)
    upv = np.stack(upper, 0).astype(np.float32)[:, :, None]
    return halves, amat, msk, upv


def _hgrn_kernel(*refs, layer, reverse, C, CB, halves):
    if reverse:
        (hq_ref, hf_ref, hi_ref, lbl_ref, amat_ref, msk_ref, up_ref, of_ref, hg_ref, gain_ref,
         o_ref, st_ref) = refs
    else:
        (hq_ref, hf_ref, hi_ref, lbl_ref, amat_ref, msk_ref, up_ref, o_ref, st_ref) = refs
    H = HGRN_HEADS
    W = hq_ref.shape[-1]
    dk = W // H
    NL = len(halves)
    n_small = sum(1 for m in halves if m < V7X_SUBLANES)

    @pl.when(pl.program_id(1) == 0)
    def _():
        st_ref[...] = jnp.zeros_like(st_ref)

    lgt = lbl_ref[...]
    e = jnp.exp(lgt - jnp.max(lgt, axis=0, keepdims=True))
    p = e / jnp.sum(e, axis=0, keepdims=True)
    lb = jnp.zeros((1, W), F32)
    for j in range(1, layer + 1):
        lb = lb + p[j:j + 1]

    def gates_and_factors(j):
        rows = slice(j * C, (j + 1) * C)
        q = hq_ref[0, rows].astype(F32)
        f = lb + (1.0 - lb) * _sigmoid(hf_ref[0, rows].astype(F32))
        k = 1.0 - f
        g = jnp.log2(f)
        v = hi_ref[0, rows].astype(F32)

        sums = _dot_sel_x(amat_ref[...], g)
        b = sums[n_small * C:(n_small + 1) * C]
        tot = sums[(n_small + 1) * C:(n_small + 1) * C + 1]
        xs = []
        small = 0
        for m in halves:
            if m >= V7X_SUBLANES:
                shp = (C // (2 * m), 2 * m, W)
                b3 = b.reshape(shp)
                r0 = m if reverse else m - 1
                ref = b3[:, r0:r0 + 1, :]
                rowi = lax.broadcasted_iota(jnp.int32, shp, 1)
                qside = (rowi < m) if reverse else (rowi >= m)
                x3 = jnp.where(qside, q.reshape(shp), k.reshape(shp)) * jnp.exp2(
                    jnp.where(qside, b3 - ref, ref - b3))
                xs.append(x3.reshape(C, W).astype(BF16))
            else:
                xs.append((jnp.where(up_ref[small] > 0.5, q, k)
                           * jnp.exp2(sums[small * C:(small + 1) * C])).astype(BF16))
                small += 1
        return dict(rows=rows, v=v.astype(BF16), vt=v.T.astype(BF16), xs=xs, qk=q * k,
                    qb=(q * jnp.exp2(b)).astype(BF16), kr=(k * jnp.exp2(tot - b)).astype(BF16),
                    ex_tot=jnp.exp2(tot))

    def scores_and_state(a):
        scs, inters = [], []
        for h in range(H):
            sl = slice(h * dk, (h + 1) * dk)
            parts = [_dot_nt(a['xs'][l][:, sl], a['xs'][l][:, sl]) for l in range(NL)]
            st = st_ref[h]
            inters.append(_dot_nt(a['qb'][:, sl], st.astype(BF16)))
            st_ref[h] = st * a['ex_tot'][:, sl] + _dot(a['vt'][sl, :], a['kr'][:, sl])
            sc = msk_ref[NL] * jnp.sum(a['qk'][:, sl], axis=-1, keepdims=True)
            for l in range(NL):
                sc = sc + msk_ref[l] * parts[l]
            scs.append(sc.astype(BF16))
        return scs, inters

    def outputs(a, scs, inters):
        rows = a['rows']
        if reverse:
            of = of_ref[0, rows]
            hg = hg_ref[0, rows].astype(F32)
            gain = gain_ref[...]
        for h in range(H):
            sl = slice(h * dk, (h + 1) * dk)
            o_h = inters[h] + _dot(scs[h], a['v'][:, sl])
            if reverse:
                tot_o = of[:, sl] + o_h
                ms = jnp.mean(tot_o * tot_o, axis=-1, keepdims=True)
                o_ref[0, rows, sl] = (tot_o * lax.rsqrt(ms + RMS_EPS) * gain
                                      * hg[:, sl]).astype(o_ref.dtype)
            else:
                o_ref[0, rows, sl] = o_h

    order = list(range(CB - 1, -1, -1) if reverse else range(CB))
    cur = gates_and_factors(order[0])
    for n, j in enumerate(order):
        scs, inters = scores_and_state(cur)
        nxt = gates_and_factors(order[n + 1]) if n + 1 < CB else None
        outputs(cur, scs, inters)
        cur = nxt


def _hgrn_pass(phg, lb_logits, norm_gain, of, layer, reverse):
    B, S, W5 = phg.shape
    W = W5 // 5
    C = min(C_HGRN, S)
    CB = min(CB_HGRN, S // C)
    TB = CB * C
    nc = S // TB
    halves, amat, msk, upv = _hgrn_consts(C, reverse)
    L = lb_logits.shape[0]
    dk = W // HGRN_HEADS

    def cmap(i):
        return nc - 1 - i if reverse else i

    fcol = 2 if reverse else 1
    in_specs = [
        pl.BlockSpec((1, TB, W), lambda b, i: (b, cmap(i), 0)),
        pl.BlockSpec((1, TB, W), lambda b, i: (b, cmap(i), fcol)),
        pl.BlockSpec((1, TB, W), lambda b, i: (b, cmap(i), 3)),
        pl.BlockSpec((L, W), lambda b, i: (0, 0)),
        pl.BlockSpec(amat.shape, lambda b, i: (0, 0)),
        pl.BlockSpec(msk.shape, lambda b, i: (0, 0, 0)),
        pl.BlockSpec(upv.shape, lambda b, i: (0, 0, 0)),
    ]
    args = [phg, phg, phg, lb_logits, jnp.asarray(amat, BF16), jnp.asarray(msk, F32), jnp.asarray(upv, F32)]
    if reverse:
        in_specs += [pl.BlockSpec((1, TB, W), lambda b, i: (b, cmap(i), 0)),
                     pl.BlockSpec((1, TB, W), lambda b, i: (b, cmap(i), 4)),
                     pl.BlockSpec((1, dk), lambda b, i: (0, 0))]
        args += [of, phg, norm_gain.reshape(1, dk)]
    return pl.pallas_call(
        functools.partial(_hgrn_kernel, layer=layer, reverse=reverse, C=C, CB=CB, halves=tuple(halves)),
        grid=(B, nc),
        in_specs=in_specs,
        out_specs=pl.BlockSpec((1, TB, W), lambda b, i: (b, cmap(i), 0)),
        out_shape=jax.ShapeDtypeStruct((B, S, W), BF16 if reverse else F32),
        scratch_shapes=[pltpu.VMEM((HGRN_HEADS, dk, dk), F32)],
        compiler_params=_cparams(("parallel", "arbitrary")),
        name="hgrn_bwd" if reverse else "hgrn_fwd",
    )(*args)


def _merge_kernel(ya_ref, yl_ref, yh_ref, g0_ref, g1_ref, g2_ref, x_ref, mod_ref,
                  wa_ref, wl_ref, wh_ref, wo_ref, lng_ref, lnb_ref, wr_ref,
                  x1_ref, h2_ref, route_ref, *, alpha, n_groups, e_per):
    m = mod_ref[0]
    half = x_ref.shape[1] // 2
    halves = [slice(r0, r0 + half) for r0 in (0, half)]

    branch = [(_dot(ya_ref[0, rs], wa_ref[...]), _dot(yl_ref[0, rs], wl_ref[...]),
               _dot(yh_ref[0, rs], wh_ref[...])) for rs in halves]
    mixes = []
    for rs, (pa, pl_, ph) in zip(halves, branch):
        merged = (_sigmoid(g0_ref[0, rs].astype(F32)) * pa + _sigmoid(g1_ref[0, rs].astype(F32)) * pl_
                  + _sigmoid(g2_ref[0, rs].astype(F32)) * ph)
        mixes.append(_dot(merged.astype(BF16), wo_ref[...]))
    logit_halves = []
    for rs, mix in zip(halves, mixes):
        x1 = _layernorm(alpha * x_ref[0, rs] + (1.0 + m[2:3]) * mix, lng_ref[...], lnb_ref[...])
        x1_ref[0, rs] = x1
        h2 = x1 * (1.0 + m[4:5]) + m[3:4]
        h2_ref[0, rs] = h2
        xh, xl = _split2(h2)
        r = _dot(xh, wr_ref[...])
        logit_halves.append(r[:, :V7X_LANES] + r[:, V7X_LANES:] + _dot(xl, wr_ref[:, :V7X_LANES]))
    for rs, logits in zip(halves, logit_halves):
        route_ref[0, rs] = _route(logits, n_groups, e_per)


def _route(logits, n_groups, e_per):
    lane = lax.broadcasted_iota(jnp.int32, logits.shape, 1)
    big = jnp.int32(V7X_LANES)
    neg = jnp.float32(-jnp.inf)
    gl = jnp.where(lane < n_groups, logits, neg)
    gmax = jnp.max(gl, axis=-1, keepdims=True)
    gsel = jnp.min(jnp.where(gl == gmax, lane, big), axis=-1, keepdims=True)
    gp = 1.0 / jnp.sum(jnp.exp(gl - gmax), axis=-1, keepdims=True)
    lo = n_groups + gsel * e_per
    el = jnp.where((lane >= lo) & (lane < lo + e_per), logits, neg)
    m1 = jnp.max(el, axis=-1, keepdims=True)
    i1 = jnp.min(jnp.where(el == m1, lane, big), axis=-1, keepdims=True)
    el2 = jnp.where(lane == i1, neg, el)
    m2 = jnp.max(el2, axis=-1, keepdims=True)
    i2 = jnp.min(jnp.where(el2 == m2, lane, big), axis=-1, keepdims=True)
    z = jnp.sum(jnp.exp(el - m1), axis=-1, keepdims=True)
    p1 = 1.0 / z
    p2 = jnp.exp(m2 - m1) / z
    w1 = gp * (p1 / (p1 + p2))
    w2 = gp * (p2 / (p1 + p2))
    e1 = (i1 - n_groups).astype(F32)
    e2 = (i2 - n_groups).astype(F32)
    return jnp.where(lane == 0, w1, jnp.where(lane == 1, w2,
                     jnp.where(lane == 2, e1, jnp.where(lane == 3, e2, 0.0))))


def _merge(ya, yl, yh, pmg, x, mod_l, wa, wl, wh, wo, lng, lnb, wr, n_groups, e_per, alpha):
    B, S, D = x.shape
    tm = min(TM_MERGE, S)
    bw = ya.shape[-1]
    tok = lambda b, i: (b, i, 0)
    const = lambda b, i: (0, 0)
    return pl.pallas_call(
        functools.partial(_merge_kernel, alpha=alpha, n_groups=n_groups, e_per=e_per),
        grid=(B, S // tm),
        in_specs=[pl.BlockSpec((1, tm, bw), tok), pl.BlockSpec((1, tm, bw), tok), pl.BlockSpec((1, tm, bw), tok),
                  pl.BlockSpec((1, tm, D), lambda b, i: (b, i, 0)),
                  pl.BlockSpec((1, tm, D), lambda b, i: (b, i, 1)),
                  pl.BlockSpec((1, tm, D), lambda b, i: (b, i, 2)),
                  pl.BlockSpec((1, tm, D), tok),
                  pl.BlockSpec((1, 6, D), lambda b, i: (b, 0, 0)),
                  pl.BlockSpec((bw, D), const), pl.BlockSpec((bw, D), const), pl.BlockSpec((bw, D), const),
                  pl.BlockSpec((D, D), const),
                  pl.BlockSpec((1, D), const), pl.BlockSpec((1, D), const),
                  pl.BlockSpec((D, 2 * V7X_LANES), const)],
        out_specs=[pl.BlockSpec((1, tm, D), tok), pl.BlockSpec((1, tm, D), tok),
                   pl.BlockSpec((1, tm, V7X_LANES), tok)],
        out_shape=[jax.ShapeDtypeStruct((B, S, D), F32), jax.ShapeDtypeStruct((B, S, D), F32),
                   jax.ShapeDtypeStruct((B, S, V7X_LANES), F32)],
        compiler_params=_cparams(("parallel", "parallel")),
        name="merge",
    )(ya, yl, yh, pmg, pmg, pmg, x, mod_l, wa, wl, wh, wo, lng, lnb, wr)


def _dispatch_kernel(dest_ref, h_ref, xs_in, xs_hbm, sem, *, TM):
    del xs_in

    for r in range(TM):
        for a in range(TOP_K):
            pltpu.make_async_copy(h_ref.at[pl.ds(r, 1)],
                                  xs_hbm.at[pl.ds(dest_ref[a, 0, 0, r], 1)],
                                  sem).start(priority=a % 2)
    for a in range(TOP_K):
        pltpu.make_async_copy(h_ref, xs_hbm.at[pl.ds(0, TM)], sem).wait()


def _dispatch(h2, dest, xs_prev, S):
    N, D = h2.shape
    TM = min(TM_DISPATCH, S)
    nt = N // TM
    return pl.pallas_call(
        functools.partial(_dispatch_kernel, TM=TM),
        grid=(nt,),
        in_specs=[pl.BlockSpec((TOP_K, 1, 1, TM), lambda i: (0, i, 0, 0), memory_space=pltpu.SMEM),
                  pl.BlockSpec((TM, D), lambda i: (i, 0)),
                  pl.BlockSpec(memory_space=pl.ANY)],
        out_specs=pl.BlockSpec(memory_space=pl.ANY),
        out_shape=jax.ShapeDtypeStruct(xs_prev.shape, xs_prev.dtype),
        scratch_shapes=[pltpu.SemaphoreType.DMA(())],
        input_output_aliases={2: 0},
        compiler_params=_cparams(("arbitrary",)),
        name="moe_dispatch",
    )(dest.reshape(TOP_K, nt, 1, TM), h2, xs_prev)


def _expert_kernel(blk_e_ref, nused_ref, xs_ref, w13_ref, w2_ref, ys_ref, *, F):
    i = pl.program_id(0)

    @pl.when(i < nused_ref[0])
    def _():
        half = xs_ref.shape[0] // 2
        abs_ = [_dot(xs_ref[r0:r0 + half, :].astype(BF16), w13_ref[0, 0]) for r0 in (0, half)]
        for n, r0 in enumerate((0, half)):
            hmid = (_silu(abs_[n][:, :F]) * abs_[n][:, F:]).astype(BF16)
            ys_ref[r0:r0 + half, :] = _dot(hmid, w2_ref[0, 0])

    @pl.when(i >= nused_ref[0])
    def _():
        ys_ref[...] = jnp.zeros_like(ys_ref)


def _experts(xs, blk_e, nused, w13, w2, layer):
    P, D = xs.shape
    R = R_MOE
    nb = P // R
    F = w2.shape[-2]
    grid_spec = pltpu.PrefetchScalarGridSpec(
        num_scalar_prefetch=2,
        grid=(nb,),
        in_specs=[pl.BlockSpec((R, D), lambda i, be, nu: (i, 0)),
                  pl.BlockSpec((1, 1, D, 2 * F), lambda i, be, nu: (layer, be[i], 0, 0)),
                  pl.BlockSpec((1, 1, F, D), lambda i, be, nu: (layer, be[i], 0, 0))],
        out_specs=pl.BlockSpec((R, D), lambda i, be, nu: (i, 0)),
    )
    return pl.pallas_call(
        functools.partial(_expert_kernel, F=F),
        grid_spec=grid_spec,
        out_shape=jax.ShapeDtypeStruct((P, D), F32),
        compiler_params=_cparams(("arbitrary",)),
        name="moe_experts",
    )(blk_e, nused, xs, w13, w2)


def _combine_kernel(pos_ref, posn_ref, ys_hbm, route_ref, x_ref, mod_ref, lng_ref, lnb_ref, o_ref,
                    ybuf, sems, *, TC, alpha):
    i = pl.program_id(0)
    n = pl.num_programs(0)
    slot = i % 2

    def issue(p_ref, s):
        for r in range(TC):
            for a in range(TOP_K):
                pltpu.make_async_copy(ys_hbm.at[pl.ds(p_ref[a, 0, 0, r], 1)],
                                      ybuf.at[s, a, pl.ds(r, 1)], sems.at[s]).start(priority=a % 2)

    @pl.when(i == 0)
    def _():
        issue(pos_ref, 0)

    for s in range(2):
        @pl.when((i + 1 < n) & (slot == 1 - s))
        def _(s=s):
            issue(posn_ref, s)

    for a in range(TOP_K):
        pltpu.make_async_copy(ys_hbm.at[pl.ds(0, TC)], ybuf.at[slot, a], sems.at[slot]).wait()

    m = mod_ref[0]
    w = route_ref[...]
    ffn = w[:, 0:1] * ybuf[slot, 0]
    for a in range(1, TOP_K):
        ffn = ffn + w[:, a:a + 1] * ybuf[slot, a]
    o_ref[...] = _layernorm(alpha * x_ref[...] + (1.0 + m[5:6]) * ffn, lng_ref[...], lnb_ref[...])


def _combine(ys, dest, route, x1, mod_l, lng, lnb, S, alpha):
    N, D = x1.shape
    TC = min(TC_MOE, S)
    nt = N // TC
    per_b = S // TC
    pos = dest.reshape(TOP_K, nt, 1, TC)
    return pl.pallas_call(
        functools.partial(_combine_kernel, TC=TC, alpha=alpha),
        grid=(nt,),
        in_specs=[pl.BlockSpec((TOP_K, 1, 1, TC), lambda i: (0, i, 0, 0), memory_space=pltpu.SMEM),
                  pl.BlockSpec((TOP_K, 1, 1, TC), lambda i: (0, jnp.minimum(i + 1, nt - 1), 0, 0),
                               memory_space=pltpu.SMEM),
                  pl.BlockSpec(memory_space=pl.ANY),
                  pl.BlockSpec((TC, V7X_LANES), lambda i: (i, 0)),
                  pl.BlockSpec((TC, D), lambda i: (i, 0)),
                  pl.BlockSpec((1, 6, D), lambda i: (i // per_b, 0, 0)),
                  pl.BlockSpec((1, D), lambda i: (0, 0)),
                  pl.BlockSpec((1, D), lambda i: (0, 0))],
        out_specs=pl.BlockSpec((TC, D), lambda i: (i, 0)),
        out_shape=jax.ShapeDtypeStruct((N, D), F32),
        scratch_shapes=[pltpu.VMEM((2, TOP_K, TC, D), F32), pltpu.SemaphoreType.DMA((2,))],
        compiler_params=_cparams(("arbitrary",)),
        name="moe_combine",
    )(pos, pos, ys, route, x1, mod_l, lng, lnb)


def _dispatch_plan(route, n_experts, R):
    lanes = jnp.arange(n_experts, dtype=jnp.int32)
    hots = [route[:, TOP_K + a].astype(jnp.int32)[:, None] == lanes for a in range(TOP_K)]
    hot = sum(h.astype(jnp.int32) for h in hots)
    csum = jnp.cumsum(hot, axis=0)
    counts = csum[-1]
    pcounts = (counts + R - 1) // R * R
    pends = jnp.cumsum(pcounts)
    pstarts = pends - pcounts
    base = pstarts[None, :] + csum - hot
    dest = jnp.stack([jnp.sum(jnp.where(h, base, 0), axis=1) for h in hots], 0).astype(jnp.int32)
    nb = (route.shape[0] * TOP_K + n_experts * R) // R
    starts = jnp.arange(nb, dtype=jnp.int32) * R
    blk_e = jnp.minimum((pends[None, :] <= starts[:, None]).astype(jnp.int32).sum(1), n_experts - 1)
    nused = (pends[-1:] // R).astype(jnp.int32)
    return blk_e, nused, dest


def _prep_weights(p):
    L = p['w_in'].shape[0]
    W = p['lru_conv_w'].shape[-1]
    nblk, bs = p['lru_wa'].shape[2], p['lru_wa'].shape[3]

    def blockdiag(w):
        eye = jnp.eye(nblk, dtype=w.dtype)
        return jnp.einsum('ldnij,nm->ldnimj', w, eye).reshape(L, 2, W, W)

    G, E = p['router_expert_w'].shape[1], p['router_expert_w'].shape[3]
    D = p['w_in'].shape[1]
    wr = jnp.concatenate([p['router_group_w'],
                          p['router_expert_w'].transpose(0, 2, 1, 3).reshape(L, D, G * E)], -1)
    wr = jnp.pad(wr, ((0, 0), (0, 0), (0, V7X_LANES - wr.shape[-1])))
    wr_hi = wr.astype(BF16)
    wr = jnp.concatenate([wr_hi, (wr - wr_hi.astype(F32)).astype(BF16)], -1)
    return dict(
        w_in=p['w_in'].astype(BF16),
        gq=jnp.tile(p['attn_q_gain'], (1, ATTN_HEADS))[:, None, :],
        gk=jnp.tile(p['attn_k_gain'], (1, KV_HEADS))[:, None, :],
        w_attn_o=p['w_attn_o'].astype(BF16),
        conv_w=p['lru_conv_w'], conv_b=p['lru_conv_b'][:, None, :],
        wbd=jnp.concatenate([blockdiag(p['lru_wa']), blockdiag(p['lru_wi'])], -1).astype(BF16),
        gb=jnp.concatenate([p['lru_ba'], p['lru_bi']], -1)[:, :, None, :],
        lam=p['lru_lambda'][:, :, None, :],
        w_lru_o=p['w_lru_o'].astype(BF16),
        w_hgrn_o=p['w_hgrn_o'].astype(BF16),
        w_out=p['w_out'].astype(BF16),
        ln1_g=p['ln1_g'][:, None, :], ln1_b=p['ln1_b'][:, None, :],
        ln2_g=p['ln2_g'][:, None, :], ln2_b=p['ln2_b'][:, None, :],
        wr=wr, n_groups=G, e_per=E,
        w13=p['expert_w13'].astype(BF16), w2=p['expert_w2'].astype(BF16),
    )


def _trunk(x, c, p, wp):
    B, S, D = x.shape
    L = p['w_in'].shape[0]
    alpha = (2 * L) ** 0.25
    QW = ATTN_HEADS * HEAD_DIM
    KW = KV_HEADS * HEAD_DIM
    LW = p['lru_conv_w'].shape[-1]
    HW = p['hgrn_lb_logits'].shape[-1]
    widths = (QW, 2 * KW, 2 * LW, 5 * HW, N_BRANCH * D)
    assert sum(widths) == p['w_in'].shape[-1]
    acts = (((0, None),), ((0, None),), ((0, None), (LW, _gelu_tanh)),
            ((0, _silu), (HW, None), (4 * HW, _silu)), ((0, None),))
    n_experts = wp['n_groups'] * wp['e_per']

    mod = _ada_mod(c, p['ada_w'], p['ada_b'])
    cos_t, sin_t = _rope_tables(S)
    bind, psw = _head_consts()
    cos_t, sin_t = jnp.asarray(cos_t), jnp.asarray(sin_t)
    bind, psw = jnp.asarray(bind, BF16), jnp.asarray(psw, BF16)

    xs = None
    for l in range(L):
        pq, pkv, plru, phg, pmg = _inproj(x, mod[l], wp['w_in'], l, widths, acts)
        qt, k, vt = _attn_prep(pq, pkv, cos_t, sin_t, wp['gq'][l], wp['gk'][l], bind, psw)
        ya = _attention(qt, k, vt)
        lru_args = (plru, wp['conv_w'][l], wp['conv_b'][l], wp['wbd'][l], wp['gb'][l], wp['lam'][l])
        hf = _lru_pass(*lru_args, None, reverse=False)
        yl = _lru_pass(*lru_args, hf, reverse=True)
        of = _hgrn_pass(phg, p['hgrn_lb_logits'], p['hgrn_norm_gain'][l], None, l, reverse=False)
        yh = _hgrn_pass(phg, p['hgrn_lb_logits'], p['hgrn_norm_gain'][l], of, l, reverse=True)
        x1, h2, route = _merge(ya, yl, yh, pmg, x, mod[l], wp['w_attn_o'][l], wp['w_lru_o'][l],
                               wp['w_hgrn_o'][l], wp['w_out'][l], wp['ln1_g'][l], wp['ln1_b'][l],
                               wp['wr'][l], wp['n_groups'], wp['e_per'], alpha)
        N = B * S
        route = route.reshape(N, V7X_LANES)
        blk_e, nused, dest = _dispatch_plan(route, n_experts, R_MOE)
        if xs is None:
            xs = jnp.zeros((N * TOP_K + n_experts * R_MOE, D), F32)
        xs = _dispatch(h2.reshape(N, D), dest, xs, S)
        ys = _experts(xs, blk_e, nused, wp['w13'], wp['w2'], l)
        x = _combine(ys, dest, route, x1.reshape(N, D), mod[l], wp['ln2_g'][l], wp['ln2_b'][l], S,
                     alpha).reshape(B, S, D)
    return x


def kernel(x_prompt, x_sample, c_prompt, c_sample, ada_w, ada_b, w_in, attn_q_gain, attn_k_gain, w_attn_o, lru_conv_w, lru_conv_b, lru_wa, lru_ba, lru_wi, lru_bi, lru_lambda, w_lru_o, hgrn_lb_logits, hgrn_norm_gain, w_hgrn_o, w_out, ln1_g, ln1_b, router_group_w, router_expert_w, expert_w13, expert_w2, ln2_g, ln2_b):
    p = dict(ada_w=ada_w, ada_b=ada_b, w_in=w_in, attn_q_gain=attn_q_gain, attn_k_gain=attn_k_gain,
             w_attn_o=w_attn_o, lru_conv_w=lru_conv_w, lru_conv_b=lru_conv_b, lru_wa=lru_wa, lru_ba=lru_ba,
             lru_wi=lru_wi, lru_bi=lru_bi, lru_lambda=lru_lambda, w_lru_o=w_lru_o,
             hgrn_lb_logits=hgrn_lb_logits, hgrn_norm_gain=hgrn_norm_gain, w_hgrn_o=w_hgrn_o, w_out=w_out,
             ln1_g=ln1_g, ln1_b=ln1_b, router_group_w=router_group_w, router_expert_w=router_expert_w,
             expert_w13=expert_w13, expert_w2=expert_w2, ln2_g=ln2_g, ln2_b=ln2_b)
    wp = _prep_weights(p)
    return (_trunk(x_prompt, c_prompt, p, wp), _trunk(x_sample, c_sample, p, wp))
```

```python
import functools
import math

import numpy as np
import jax
import jax.numpy as jnp
from jax import lax
from jax.experimental import pallas as pl
from jax.experimental.pallas import tpu as pltpu

F32 = jnp.float32
BF16 = jnp.bfloat16

GRID_W = 64
CONV_W = 4
CONV_LEFT = 2
LRU_C = 8.0
ATTN_HEADS = 8
KV_HEADS = 2
HEAD_DIM = 64
GQA_GROUP = ATTN_HEADS // KV_HEADS
AXIAL_DIM = HEAD_DIM // 2
AXIAL_FREQS = AXIAL_DIM // 2
ROPE_THETA = 10000.0
HGRN_HEADS = 4
N_BRANCH = 3
TOP_K = 2
LN_EPS = 1e-5
RMS_EPS = 1e-6
LOG2_E = 1.4426950408889634

V7X_LANES = 128
V7X_SUBLANES = 8
V7X_BF16_SUBLANES = 16
V7X_VMEM_LIMIT_BYTES = 56 * 1024 * 1024

TM_INPROJ = 512
V_AUG_ROWS = HEAD_DIM + V7X_BF16_SUBLANES
TQ_ATTN = 512
T_LRU = 256
C_HGRN = 128
CB_HGRN = 4
TK_ATTN = 512
TM_MERGE = 512
R_MOE = 512
TM_DISPATCH = 256
TC_MOE = 256


def _cparams(sem):
    return pltpu.CompilerParams(dimension_semantics=sem, vmem_limit_bytes=V7X_VMEM_LIMIT_BYTES)


def _dot(a, b):
    return jnp.dot(a, b, preferred_element_type=F32)


def _dot_nt(a, b):
    return lax.dot_general(a, b, (((1,), (1,)), ((), ())), preferred_element_type=F32)


def _dot_hi(a, b):
    return jnp.dot(a, b, preferred_element_type=F32, precision=lax.Precision.HIGHEST)


def _split2(x):
    hi = x.astype(BF16)
    lo = (x - hi.astype(F32)).astype(BF16)
    return hi, lo


def _dot_x_sel(x, m):
    hi, lo = _split2(x)
    return _dot(hi, m) + _dot(lo, m)


def _dot_sel_x(m, x):
    hi, lo = _split2(x)
    return _dot(m, hi) + _dot(m, lo)


def _sigmoid(x):
    return 1.0 / (1.0 + jnp.exp(-x))


def _silu(x):
    return x * _sigmoid(x)


def _layernorm(y, g, b):
    mu = jnp.mean(y, axis=-1, keepdims=True)
    d = y - mu
    var = jnp.mean(d * d, axis=-1, keepdims=True)
    return d * lax.rsqrt(var + LN_EPS) * g + b


def _ada_kernel(c_ref, w_ref, b_ref, o_ref):
    c = c_ref[...]
    o_ref[0] = _dot_hi(_silu(c), w_ref[0]) + b_ref[0]


def _ada_mod(c, ada_w, ada_b):
    L, D, N6 = ada_w.shape
    B = c.shape[0]
    tn = N6 // 6
    out = pl.pallas_call(
        _ada_kernel,
        grid=(L, N6 // tn),
        in_specs=[pl.BlockSpec((B, D), lambda l, j: (0, 0)),
                  pl.BlockSpec((1, D, tn), lambda l, j: (l, 0, j)),
                  pl.BlockSpec((1, 1, tn), lambda l, j: (l, 0, j))],
        out_specs=pl.BlockSpec((1, B, tn), lambda l, j: (l, 0, j)),
        out_shape=jax.ShapeDtypeStruct((L, B, N6), F32),
        compiler_params=_cparams(("arbitrary", "arbitrary")),
        name="ada_mod",
    )(c, ada_w, ada_b.reshape(L, 1, N6))
    return out.reshape(L, B, 6, N6 // 6)


def _inproj_kernel(x_ref, mod_ref, w_ref, *o_refs, chunk, acts):
    m = mod_ref[0]
    h = (x_ref[0] * (1.0 + m[1:2]) + m[0:1]).astype(BF16)
    off = 0
    for o_ref, segs in zip(o_refs, acts):
        n = o_ref.shape[-1]
        for c0 in range(0, n, chunk):
            c1 = min(c0 + chunk, n)
            act = [f for start, f in segs if start <= c0][-1]
            y = _dot(h, w_ref[0, :, off + c0:off + c1])
            o_ref[0, :, c0:c1] = (y if act is None else act(y)).astype(o_ref.dtype)
        off += n


def _inproj(x, mod_l, w_in_bf, layer, widths, acts):
    B, S, D = x.shape
    N = w_in_bf.shape[-1]
    tm = min(TM_INPROJ, S)
    out_shape = [jax.ShapeDtypeStruct((B, S, n), BF16) for n in widths]
    out_specs = [pl.BlockSpec((1, tm, n), lambda b, i: (b, i, 0)) for n in widths]
    return pl.pallas_call(
        functools.partial(_inproj_kernel, chunk=512, acts=acts),
        grid=(B, S // tm),
        in_specs=[pl.BlockSpec((1, tm, D), lambda b, i: (b, i, 0)),
                  pl.BlockSpec((1, 6, D), lambda b, i: (b, 0, 0)),
                  pl.BlockSpec((1, D, N), lambda b, i: (layer, 0, 0), pipeline_mode=pl.Buffered(1))],
        out_specs=out_specs,
        out_shape=out_shape,
        compiler_params=_cparams(("parallel", "parallel")),
        name="in_proj",
    )(x, mod_l, w_in_bf)


def _rope_tables(S):
    t = np.arange(S)
    row = (t // GRID_W).astype(np.float64)
    col = (t % GRID_W).astype(np.float64)
    inv = ROPE_THETA ** (-np.arange(AXIAL_FREQS, dtype=np.float64) / AXIAL_FREQS)
    ang_r = (row[:, None].astype(np.float32) * inv.astype(np.float32)[None, :]).astype(np.float32)
    ang_c = (col[:, None].astype(np.float32) * inv.astype(np.float32)[None, :]).astype(np.float32)
    cos_h = np.concatenate([np.cos(ang_r), np.cos(ang_r), np.cos(ang_c), np.cos(ang_c)], -1)
    sin_h = np.concatenate([np.sin(ang_r), np.sin(ang_r), np.sin(ang_c), np.sin(ang_c)], -1)
    cos_t = np.tile(cos_h, (1, ATTN_HEADS)).astype(np.float32)
    sin_t = np.tile(sin_h, (1, ATTN_HEADS)).astype(np.float32)
    return cos_t, sin_t


def _head_consts():
    W = ATTN_HEADS * HEAD_DIM
    d = np.arange(W)
    bind = (d[:, None] // HEAD_DIM == d[None, :] // HEAD_DIM).astype(np.float32) / HEAD_DIM
    psw = np.zeros((W, W), np.float32)
    lowhalf = (d % AXIAL_DIM) < AXIAL_FREQS
    for j in d:
        if lowhalf[j]:
            psw[j + AXIAL_FREQS, j] = -1.0
        else:
            psw[j - AXIAL_FREQS, j] = 1.0
    return bind, psw


def _prep_kernel(pq_ref, pkv_ref, cos_ref, sin_ref, gq_ref, gk_ref, bind_ref, psw_ref,
                 qt_ref, k_ref, vt_ref):
    KW = KV_HEADS * HEAD_DIM
    cos = cos_ref[...]
    sin = sin_ref[...]

    def norm_rope(u, gain, w):
        ms = _dot_x_sel(u * u, bind_ref[:w, :w])
        un = u * lax.rsqrt(ms + RMS_EPS) * gain
        return un * cos[:, :w] + _dot_x_sel(un, psw_ref[:w, :w]) * sin[:, :w]

    q = norm_rope(pq_ref[0].astype(F32), gq_ref[...], ATTN_HEADS * HEAD_DIM)
    qt = (q * (HEAD_DIM ** -0.5 * LOG2_E)).T
    zeros = jnp.zeros((HEAD_DIM, qt.shape[1]), F32)
    for h in range(ATTN_HEADS):
        rows = [zeros] * KV_HEADS
        rows[h // GQA_GROUP] = qt[h * HEAD_DIM:(h + 1) * HEAD_DIM]
        qt_ref[0, h] = jnp.concatenate(rows, axis=0).astype(BF16)
    kv = pkv_ref[0]
    k_ref[0, 0] = norm_rope(kv[:, :KW].astype(F32), gk_ref[...], KW).astype(BF16)
    vt = kv[:, KW:].astype(F32).T
    tm = vt.shape[1]
    extra = (lax.broadcasted_iota(jnp.int32, (V_AUG_ROWS - HEAD_DIM, tm), 0) == 0).astype(F32)
    vt_ref[0, 0] = jnp.concatenate(
        [blk for g in range(KV_HEADS) for blk in (vt[g * HEAD_DIM:(g + 1) * HEAD_DIM], extra)],
        axis=0).astype(BF16)


def _attn_prep(pq, pkv, cos_t, sin_t, gq, gk, bind, psw):
    B, S, QW = pq.shape
    KW = KV_HEADS * HEAD_DIM
    tm = min(TK_ATTN, S)
    return pl.pallas_call(
        _prep_kernel,
        grid=(B, S // tm),
        in_specs=[pl.BlockSpec((1, tm, QW), lambda b, i: (b, i, 0)),
                  pl.BlockSpec((1, tm, 2 * KW), lambda b, i: (b, i, 0)),
                  pl.BlockSpec((tm, QW), lambda b, i: (i, 0)),
                  pl.BlockSpec((tm, QW), lambda b, i: (i, 0)),
                  pl.BlockSpec((1, QW), lambda b, i: (0, 0)),
                  pl.BlockSpec((1, KW), lambda b, i: (0, 0)),
                  pl.BlockSpec((QW, QW), lambda b, i: (0, 0)),
                  pl.BlockSpec((QW, QW), lambda b, i: (0, 0))],
        out_specs=[pl.BlockSpec((1, ATTN_HEADS, KW, tm), lambda b, i: (b, 0, 0, i)),
                   pl.BlockSpec((1, 1, tm, KW), lambda b, i: (b, i, 0, 0)),
                   pl.BlockSpec((1, 1, KV_HEADS * V_AUG_ROWS, tm), lambda b, i: (b, i, 0, 0))],
        out_shape=[jax.ShapeDtypeStruct((B, ATTN_HEADS, KW, S), BF16),
                   jax.ShapeDtypeStruct((B, S // tm, tm, KW), BF16),
                   jax.ShapeDtypeStruct((B, S // tm, KV_HEADS * V_AUG_ROWS, tm), BF16)],
        compiler_params=_cparams(("parallel", "parallel")),
        name="attn_prep",
    )(pq, pkv, cos_t, sin_t, gq, gk, bind, psw)


def _attn_kernel(qt_ref, k_ref, vt_ref, o_ref, st_scr, m_scr, acc_scr, *, nk):
    def scores_to(slot, c):
        kc = k_ref[0, c]
        for h in range(GQA_GROUP):
            st_scr[slot, h] = _dot(kc, qt_ref[0, h])

    def consume(slot, c):
        vc = vt_ref[0, c]
        for h in range(GQA_GROUP):
            st = st_scr[slot, h]
            m_old = m_scr[h]
            m_new = jnp.maximum(m_old, jnp.max(st, axis=0, keepdims=True))
            p = jnp.exp2(st - m_new).astype(BF16)
            acc_scr[h] = jnp.exp2(m_old - m_new) * acc_scr[h] + _dot(vc, p)
            m_scr[h] = m_new

    m_scr[...] = jnp.full(m_scr.shape, -jnp.inf, F32)
    acc_scr[...] = jnp.zeros(acc_scr.shape, F32)
    scores_to(0, 0)

    def step(nslot, cn, cslot, cc):
        kc = k_ref[0, cn]
        vc = vt_ref[0, cc]
        for h in range(GQA_GROUP):
            st_scr[nslot, h] = _dot(kc, qt_ref[0, h])
            st = st_scr[cslot, h]
            m_old = m_scr[h]
            m_new = jnp.maximum(m_old, jnp.max(st, axis=0, keepdims=True))
            p = jnp.exp2(st - m_new).astype(BF16)
            acc_scr[h] = jnp.exp2(m_old - m_new) * acc_scr[h] + _dot(vc, p)
            m_scr[h] = m_new

    def body(j, carry):
        c0 = 2 * j
        step(1, c0 + 1, 0, c0)
        step(0, c0 + 2, 1, c0 + 1)
        return carry
    lax.fori_loop(0, nk // 2 - 1, body, 0)
    step(1, nk - 1, 0, nk - 2)
    consume(1, nk - 1)

    outs = []
    for h in range(GQA_GROUP):
        acc = acc_scr[h]
        outs.append((acc[:HEAD_DIM] / acc[HEAD_DIM:HEAD_DIM + 1]).T)
    o_ref[0] = jnp.concatenate(outs, axis=-1).astype(o_ref.dtype)


def _attention(qt, k, vt):
    B, nk, tk, KW = k.shape
    S = nk * tk
    GW = GQA_GROUP * HEAD_DIM
    tq = min(TQ_ATTN, S)
    assert nk % 2 == 0
    scratch = [pltpu.VMEM((2, GQA_GROUP, tk, tq), F32), pltpu.VMEM((GQA_GROUP, 1, tq), F32),
               pltpu.VMEM((GQA_GROUP, V_AUG_ROWS, tq), F32)]
    return pl.pallas_call(
        functools.partial(_attn_kernel, nk=nk),
        scratch_shapes=scratch,
        grid=(B, KV_HEADS, S // tq),
        in_specs=[pl.BlockSpec((1, GQA_GROUP, KW, tq), lambda b, g, i: (b, g, 0, i)),
                  pl.BlockSpec((1, nk, tk, KW), lambda b, g, i: (b, 0, 0, 0)),
                  pl.BlockSpec((1, nk, V_AUG_ROWS, tk), lambda b, g, i: (b, 0, g, 0))],
        out_specs=pl.BlockSpec((1, tq, GW), lambda b, g, i: (b, i, g)),
        out_shape=jax.ShapeDtypeStruct((B, S, ATTN_HEADS * HEAD_DIM), BF16),
        compiler_params=_cparams(("parallel", "parallel", "parallel")),
        name="attention",
    )(qt, k, vt)


def _log1p(y):
    u = 1.0 + y
    return jnp.where(u == 1.0, y, jnp.log(u) * (y / (u - 1.0)))


def _softplus(z):
    return jnp.maximum(z, 0.0) + _log1p(jnp.exp(-jnp.abs(z)))


def _neg_expm1_2x(z, ez):
    return -jnp.tanh(z) * (ez * ez + 1.0)


def _gelu_tanh(x):
    return 0.5 * x * (1.0 + jnp.tanh(math.sqrt(2.0 / math.pi) * (x + 0.044715 * (x * x * x))))


def _lru_kernel(*refs, reverse, T):
    if reverse:
        xc_ref, wbd_ref, gb_ref, lam_ref, hf_ref, lg_ref, o_ref, carry_ref = refs
    else:
        (x_ref, xp_ref, xn_ref, cw_ref, cb_ref, wbd_ref, gb_ref, lam_ref,
         o_ref, xc_out_ref, carry_ref) = refs
    i = pl.program_id(1)

    @pl.when(i == 0)
    def _():
        carry_ref[...] = jnp.zeros_like(carry_ref)

    xc = xc_ref[0] if reverse else _lru_conv(x_ref, xp_ref, xn_ref, cw_ref, cb_ref, i, T)
    if not reverse:
        xc_out_ref[0] = xc
    _lru_scan(xc, wbd_ref, gb_ref, lam_ref, carry_ref, o_ref,
              (hf_ref, lg_ref) if reverse else None, reverse, T)


def _lru_conv(x_ref, xp_ref, xn_ref, cw_ref, cb_ref, tile, T):
    nt = pl.num_programs(1)
    W = x_ref.shape[-1]
    x = x_ref[0].astype(F32)
    prev = jnp.where(tile == 0, 0.0, xp_ref[0].astype(F32))
    nxt = jnp.where(tile == nt - 1, 0.0, xn_ref[0].astype(F32))
    row8 = lax.broadcasted_iota(jnp.int32, (V7X_SUBLANES, W), 0)

    def shifted(k):
        if k == 0:
            return x
        r = pltpu.roll(x, (-k) % T, axis=0)
        if k < 0:
            fill = pltpu.roll(prev, (-k) % V7X_SUBLANES, axis=0)
            head = jnp.where(row8 < -k, fill, r[:V7X_SUBLANES])
            return jnp.concatenate([head, r[V7X_SUBLANES:]], axis=0)
        fill = pltpu.roll(nxt, (-k) % V7X_SUBLANES, axis=0)
        tail = jnp.where(row8 >= V7X_SUBLANES - k, fill, r[T - V7X_SUBLANES:])
        return jnp.concatenate([r[:T - V7X_SUBLANES], tail], axis=0)

    cw = cw_ref[...]
    return cb_ref[...] + sum(shifted(j - CONV_LEFT) * cw[j:j + 1] for j in range(CONV_W))


def _lru_scan(xc, wbd_ref, gb_ref, lam_ref, carry_ref, o_ref, gate_refs, reverse, T):
    W = xc.shape[-1]
    gates = _dot(xc.astype(BF16), wbd_ref[0]) + gb_ref[0]
    r = _sigmoid(gates[:, :W])
    ig = _sigmoid(gates[:, W:])
    log_a = (-LRU_C) * r * _softplus(-lam_ref[0])
    a = jnp.exp(log_a)
    u = jnp.sqrt(_neg_expm1_2x(log_a, a)) * (ig * xc)

    row = lax.broadcasted_iota(jnp.int32, (T, W), 0)

    def neighbour(z, d, fill):
        if d % V7X_SUBLANES == 0:
            pad = jnp.full((d, W), fill, F32)
            return jnp.concatenate([z[d:], pad] if reverse else [pad, z[:T - d]], axis=0)
        if reverse:
            return jnp.where(row < T - d, pltpu.roll(z, T - d, axis=0), fill)
        return jnp.where(row >= d, pltpu.roll(z, d, axis=0), fill)

    A, U = a, u
    d = 1
    while d < T:
        U = A * neighbour(U, d, 0.0) + U
        A = A * neighbour(A, d, 1.0)
        d *= 2
    h = A * carry_ref[...] + U
    if reverse:
        hf_ref, lg_ref = gate_refs
        carry_ref[...] = h[0:1]
        o_ref[0] = ((hf_ref[0] + h) * lg_ref[0].astype(F32)).astype(o_ref.dtype)
    else:
        carry_ref[...] = h[T - 1:T]
        o_ref[0] = h


def _lru_pass(plru, conv_w, conv_b, wbd, gb, lam, fwd, reverse):
    B, S, W2 = plru.shape
    W = W2 // 2
    T = min(T_LRU, S)
    nt = S // T
    r8 = T // V7X_SUBLANES
    nb8 = S // V7X_SUBLANES
    d = 1 if reverse else 0

    def tmap(i):
        return nt - 1 - i if reverse else i

    tile_spec = pl.BlockSpec((1, T, W), lambda b, i: (b, tmap(i), 0))
    gate_specs = [pl.BlockSpec((1, W, 2 * W), lambda b, i: (d, 0, 0)),
                  pl.BlockSpec((1, 1, 2 * W), lambda b, i: (d, 0, 0)),
                  pl.BlockSpec((1, 1, W), lambda b, i: (d, 0, 0))]
    if reverse:
        hf, xc = fwd
        in_specs = [tile_spec] + gate_specs + [tile_spec, pl.BlockSpec((1, T, W), lambda b, i: (b, tmap(i), 1))]
        args = [xc, wbd, gb, lam, hf, plru]
        out_specs = tile_spec
        out_shape = jax.ShapeDtypeStruct((B, S, W), BF16)
    else:
        in_specs = [
            tile_spec,
            pl.BlockSpec((1, V7X_SUBLANES, W), lambda b, i: (b, jnp.maximum(i * r8 - 1, 0), 0)),
            pl.BlockSpec((1, V7X_SUBLANES, W), lambda b, i: (b, jnp.minimum((i + 1) * r8, nb8 - 1), 0)),
            pl.BlockSpec((CONV_W, W), lambda b, i: (0, 0)),
            pl.BlockSpec((1, W), lambda b, i: (0, 0)),
        ] + gate_specs
        args = [plru, plru, plru, conv_w, conv_b, wbd, gb, lam]
        out_specs = [tile_spec, tile_spec]
        out_shape = [jax.ShapeDtypeStruct((B, S, W), F32), jax.ShapeDtypeStruct((B, S, W), F32)]
    return pl.pallas_call(
        functools.partial(_lru_kernel, reverse=reverse, T=T),
        grid=(B, nt),
        in_specs=in_specs,
        out_specs=out_specs,
        out_shape=out_shape,
        scratch_shapes=[pltpu.VMEM((1, W), F32)],
        compiler_params=_cparams(("parallel", "arbitrary")),
        name="lru_bwd" if reverse else "lru_fwd",
    )(*args)


def _hgrn_consts(C, reverse):
    nl = int(round(math.log2(C)))
    t = np.arange(C)
    halves = [C >> (lvl + 1) for lvl in range(nl)]
    a_rows, masks, upper = [], [], []
    for m in halves:
        blk = t // (2 * m)
        up = (t % (2 * m)) >= m
        mid = blk * 2 * m + m
        r = t[None, :]
        masks.append((blk[:, None] == blk[None, :]) & up[:, None] & (~up)[None, :])
        if m < V7X_SUBLANES:
            a_up = (r >= mid[:, None]) & (r <= t[:, None])
            a_lo = (r > t[:, None]) & (r <= mid[:, None] - 1)
            a_rows.append(np.where(up[:, None], a_up, a_lo))
            upper.append(up)
    a_rows.append(t[None, :] <= t[:, None])
    a_rows.append(np.ones((V7X_SUBLANES, C), bool))
    masks.append(np.eye(C, dtype=bool))
    if reverse:
        a_rows = [a[::-1, ::-1] if a.shape[0] == C else a for a in a_rows]
        masks = [mm[::-1, ::-1] for mm in masks]
        upper = [u[::-1] for u in upper]
    amat = np.concatenate(a_rows, 0).astype(np.float32)
    msk = np.stack(masks, 0).astype(np.float32)
    upv = np.stack(upper, 0).astype(np.float32)[:, :, None]
    return halves, amat, msk, upv


def _hgrn_kernel(*refs, layer, reverse, C, CB, halves):
    if reverse:
        (hq_ref, hf_ref, hi_ref, lbl_ref, amat_ref, msk_ref, up_ref, of_ref, hg_ref, gain_ref,
         o_ref, st_ref) = refs
    else:
        (hq_ref, hf_ref, hi_ref, lbl_ref, amat_ref, msk_ref, up_ref, o_ref, st_ref) = refs
    H = HGRN_HEADS
    W = hq_ref.shape[-1]
    dk = W // H
    NL = len(halves)
    n_small = sum(1 for m in halves if m < V7X_SUBLANES)

    @pl.when(pl.program_id(1) == 0)
    def _():
        st_ref[...] = jnp.zeros_like(st_ref)

    lgt = lbl_ref[...]
    e = jnp.exp(lgt - jnp.max(lgt, axis=0, keepdims=True))
    p = e / jnp.sum(e, axis=0, keepdims=True)
    lb = jnp.zeros((1, W), F32)
    for j in range(1, layer + 1):
        lb = lb + p[j:j + 1]

    def gates_and_factors(j):
        rows = slice(j * C, (j + 1) * C)
        q = hq_ref[0, rows].astype(F32)
        f = lb + (1.0 - lb) * _sigmoid(hf_ref[0, rows].astype(F32))
        k = 1.0 - f
        g = jnp.log2(f)
        v = hi_ref[0, rows].astype(F32)

        sums = _dot_sel_x(amat_ref[...], g)
        b = sums[n_small * C:(n_small + 1) * C]
        tot = sums[(n_small + 1) * C:(n_small + 1) * C + 1]
        xs = []
        small = 0
        for m in halves:
            if m >= V7X_SUBLANES:
                shp = (C // (2 * m), 2 * m, W)
                b3 = b.reshape(shp)
                r0 = m if reverse else m - 1
                ref = b3[:, r0:r0 + 1, :]
                rowi = lax.broadcasted_iota(jnp.int32, shp, 1)
                qside = (rowi < m) if reverse else (rowi >= m)
                x3 = jnp.where(qside, q.reshape(shp), k.reshape(shp)) * jnp.exp2(
                    jnp.where(qside, b3 - ref, ref - b3))
                xs.append(x3.reshape(C, W).astype(BF16))
            else:
                xs.append((jnp.where(up_ref[small] > 0.5, q, k)
                           * jnp.exp2(sums[small * C:(small + 1) * C])).astype(BF16))
                small += 1
        return dict(rows=rows, v=v.astype(BF16), vt=v.T.astype(BF16), xs=xs, qk=q * k,
                    qb=(q * jnp.exp2(b)).astype(BF16), kr=(k * jnp.exp2(tot - b)).astype(BF16),
                    ex_tot=jnp.exp2(tot))

    def scores_and_state(a):
        scs, inters = [], []
        for h in range(H):
            sl = slice(h * dk, (h + 1) * dk)
            parts = [_dot_nt(a['xs'][l][:, sl], a['xs'][l][:, sl]) for l in range(NL)]
            st = st_ref[h]
            inters.append(_dot_nt(a['qb'][:, sl], st.astype(BF16)))
            st_ref[h] = st * a['ex_tot'][:, sl] + _dot(a['vt'][sl, :], a['kr'][:, sl])
            sc = msk_ref[NL] * jnp.sum(a['qk'][:, sl], axis=-1, keepdims=True)
            for l in range(NL):
                sc = sc + msk_ref[l] * parts[l]
            scs.append(sc.astype(BF16))
        return scs, inters

    def outputs(a, scs, inters):
        rows = a['rows']
        if reverse:
            of = of_ref[0, rows]
            hg = hg_ref[0, rows].astype(F32)
            gain = gain_ref[...]
        for h in range(H):
            sl = slice(h * dk, (h + 1) * dk)
            o_h = inters[h] + _dot(scs[h], a['v'][:, sl])
            if reverse:
                tot_o = of[:, sl] + o_h
                ms = jnp.mean(tot_o * tot_o, axis=-1, keepdims=True)
                o_ref[0, rows, sl] = (tot_o * lax.rsqrt(ms + RMS_EPS) * gain
                                      * hg[:, sl]).astype(o_ref.dtype)
            else:
                o_ref[0, rows, sl] = o_h

    order = list(range(CB - 1, -1, -1) if reverse else range(CB))
    cur = gates_and_factors(order[0])
    for n, j in enumerate(order):
        scs, inters = scores_and_state(cur)
        nxt = gates_and_factors(order[n + 1]) if n + 1 < CB else None
        outputs(cur, scs, inters)
        cur = nxt


def _hgrn_pass(phg, lb_logits, norm_gain, of, layer, reverse):
    B, S, W5 = phg.shape
    W = W5 // 5
    C = min(C_HGRN, S)
    CB = min(CB_HGRN, S // C)
    TB = CB * C
    nc = S // TB
    halves, amat, msk, upv = _hgrn_consts(C, reverse)
    L = lb_logits.shape[0]
    dk = W // HGRN_HEADS

    def cmap(i):
        return nc - 1 - i if reverse else i

    fcol = 2 if reverse else 1
    in_specs = [
        pl.BlockSpec((1, TB, W), lambda b, i: (b, cmap(i), 0)),
        pl.BlockSpec((1, TB, W), lambda b, i: (b, cmap(i), fcol)),
        pl.BlockSpec((1, TB, W), lambda b, i: (b, cmap(i), 3)),
        pl.BlockSpec((L, W), lambda b, i: (0, 0)),
        pl.BlockSpec(amat.shape, lambda b, i: (0, 0)),
        pl.BlockSpec(msk.shape, lambda b, i: (0, 0, 0)),
        pl.BlockSpec(upv.shape, lambda b, i: (0, 0, 0)),
    ]
    args = [phg, phg, phg, lb_logits, jnp.asarray(amat, BF16), jnp.asarray(msk, F32), jnp.asarray(upv, F32)]
    if reverse:
        in_specs += [pl.BlockSpec((1, TB, W), lambda b, i: (b, cmap(i), 0)),
                     pl.BlockSpec((1, TB, W), lambda b, i: (b, cmap(i), 4)),
                     pl.BlockSpec((1, dk), lambda b, i: (0, 0))]
        args += [of, phg, norm_gain.reshape(1, dk)]
    return pl.pallas_call(
        functools.partial(_hgrn_kernel, layer=layer, reverse=reverse, C=C, CB=CB, halves=tuple(halves)),
        grid=(B, nc),
        in_specs=in_specs,
        out_specs=pl.BlockSpec((1, TB, W), lambda b, i: (b, cmap(i), 0)),
        out_shape=jax.ShapeDtypeStruct((B, S, W), BF16 if reverse else F32),
        scratch_shapes=[pltpu.VMEM((HGRN_HEADS, dk, dk), F32)],
        compiler_params=_cparams(("parallel", "arbitrary")),
        name="hgrn_bwd" if reverse else "hgrn_fwd",
    )(*args)


def _merge_kernel(ya_ref, yl_ref, yh_ref, g0_ref, g1_ref, g2_ref, x_ref, mod_ref,
                  wa_ref, wl_ref, wh_ref, wo_ref, lng_ref, lnb_ref, wr_ref,
                  x1_ref, h2_ref, route_ref, *, alpha, n_groups, e_per):
    m = mod_ref[0]
    half = x_ref.shape[1] // 2
    halves = [slice(r0, r0 + half) for r0 in (0, half)]

    branch = [(_dot(ya_ref[0, rs], wa_ref[...]), _dot(yl_ref[0, rs], wl_ref[...]),
               _dot(yh_ref[0, rs], wh_ref[...])) for rs in halves]
    mixes = []
    for rs, (pa, pl_, ph) in zip(halves, branch):
        merged = (_sigmoid(g0_ref[0, rs].astype(F32)) * pa + _sigmoid(g1_ref[0, rs].astype(F32)) * pl_
                  + _sigmoid(g2_ref[0, rs].astype(F32)) * ph)
        mixes.append(_dot(merged.astype(BF16), wo_ref[...]))
    logit_halves = []
    for rs, mix in zip(halves, mixes):
        x1 = _layernorm(alpha * x_ref[0, rs] + (1.0 + m[2:3]) * mix, lng_ref[...], lnb_ref[...])
        x1_ref[0, rs] = x1
        h2 = x1 * (1.0 + m[4:5]) + m[3:4]
        h2_ref[0, rs] = h2
        xh, xl = _split2(h2)
        r = _dot(xh, wr_ref[...])
        logit_halves.append(r[:, :V7X_LANES] + r[:, V7X_LANES:] + _dot(xl, wr_ref[:, :V7X_LANES]))
    for rs, logits in zip(halves, logit_halves):
        route_ref[0, rs] = _route(logits, n_groups, e_per)


def _route(logits, n_groups, e_per):
    lane = lax.broadcasted_iota(jnp.int32, logits.shape, 1)
    big = jnp.int32(V7X_LANES)
    neg = jnp.float32(-jnp.inf)
    gl = jnp.where(lane < n_groups, logits, neg)
    gmax = jnp.max(gl, axis=-1, keepdims=True)
    gsel = jnp.min(jnp.where(gl == gmax, lane, big), axis=-1, keepdims=True)
    gp = 1.0 / jnp.sum(jnp.exp(gl - gmax), axis=-1, keepdims=True)
    lo = n_groups + gsel * e_per
    el = jnp.where((lane >= lo) & (lane < lo + e_per), logits, neg)
    m1 = jnp.max(el, axis=-1, keepdims=True)
    i1 = jnp.min(jnp.where(el == m1, lane, big), axis=-1, keepdims=True)
    el2 = jnp.where(lane == i1, neg, el)
    m2 = jnp.max(el2, axis=-1, keepdims=True)
    i2 = jnp.min(jnp.where(el2 == m2, lane, big), axis=-1, keepdims=True)
    z = jnp.sum(jnp.exp(el - m1), axis=-1, keepdims=True)
    p1 = 1.0 / z
    p2 = jnp.exp(m2 - m1) / z
    w1 = gp * (p1 / (p1 + p2))
    w2 = gp * (p2 / (p1 + p2))
    e1 = (i1 - n_groups).astype(F32)
    e2 = (i2 - n_groups).astype(F32)
    return jnp.where(lane == 0, w1, jnp.where(lane == 1, w2,
                     jnp.where(lane == 2, e1, jnp.where(lane == 3, e2, 0.0))))


def _merge(ya, yl, yh, pmg, x, mod_l, wa, wl, wh, wo, lng, lnb, wr, n_groups, e_per, alpha):
    B, S, D = x.shape
    tm = min(TM_MERGE, S)
    bw = ya.shape[-1]
    tok = lambda b, i: (b, i, 0)
    const = lambda b, i: (0, 0)
    return pl.pallas_call(
        functools.partial(_merge_kernel, alpha=alpha, n_groups=n_groups, e_per=e_per),
        grid=(B, S // tm),
        in_specs=[pl.BlockSpec((1, tm, bw), tok), pl.BlockSpec((1, tm, bw), tok), pl.BlockSpec((1, tm, bw), tok),
                  pl.BlockSpec((1, tm, D), lambda b, i: (b, i, 0)),
                  pl.BlockSpec((1, tm, D), lambda b, i: (b, i, 1)),
                  pl.BlockSpec((1, tm, D), lambda b, i: (b, i, 2)),
                  pl.BlockSpec((1, tm, D), tok),
                  pl.BlockSpec((1, 6, D), lambda b, i: (b, 0, 0)),
                  pl.BlockSpec((bw, D), const), pl.BlockSpec((bw, D), const), pl.BlockSpec((bw, D), const),
                  pl.BlockSpec((D, D), const),
                  pl.BlockSpec((1, D), const), pl.BlockSpec((1, D), const),
                  pl.BlockSpec((D, 2 * V7X_LANES), const)],
        out_specs=[pl.BlockSpec((1, tm, D), tok), pl.BlockSpec((1, tm, D), tok),
                   pl.BlockSpec((1, tm, V7X_LANES), tok)],
        out_shape=[jax.ShapeDtypeStruct((B, S, D), F32), jax.ShapeDtypeStruct((B, S, D), F32),
                   jax.ShapeDtypeStruct((B, S, V7X_LANES), F32)],
        compiler_params=_cparams(("parallel", "parallel")),
        name="merge",
    )(ya, yl, yh, pmg, pmg, pmg, x, mod_l, wa, wl, wh, wo, lng, lnb, wr)


def _dispatch_kernel(dest_ref, h_ref, xs_in, xs_hbm, sem, *, TM):
    del xs_in

    for r in range(TM):
        for a in range(TOP_K):
            pltpu.make_async_copy(h_ref.at[pl.ds(r, 1)],
                                  xs_hbm.at[pl.ds(dest_ref[a, 0, 0, r], 1)],
                                  sem).start(priority=a % 2)
    for a in range(TOP_K):
        pltpu.make_async_copy(h_ref, xs_hbm.at[pl.ds(0, TM)], sem).wait()


def _dispatch(h2, dest, xs_prev, S):
    N, D = h2.shape
    TM = min(TM_DISPATCH, S)
    nt = N // TM
    return pl.pallas_call(
        functools.partial(_dispatch_kernel, TM=TM),
        grid=(nt,),
        in_specs=[pl.BlockSpec((TOP_K, 1, 1, TM), lambda i: (0, i, 0, 0), memory_space=pltpu.SMEM),
                  pl.BlockSpec((TM, D), lambda i: (i, 0)),
                  pl.BlockSpec(memory_space=pl.ANY)],
        out_specs=pl.BlockSpec(memory_space=pl.ANY),
        out_shape=jax.ShapeDtypeStruct(xs_prev.shape, xs_prev.dtype),
        scratch_shapes=[pltpu.SemaphoreType.DMA(())],
        input_output_aliases={2: 0},
        compiler_params=_cparams(("arbitrary",)),
        name="moe_dispatch",
    )(dest.reshape(TOP_K, nt, 1, TM), h2, xs_prev)


def _expert_kernel(blk_e_ref, nused_ref, xs_ref, w13_ref, w2_ref, ys_ref, *, F):
    i = pl.program_id(0)

    @pl.when(i < nused_ref[0])
    def _():
        half = xs_ref.shape[0] // 2
        abs_ = [_dot(xs_ref[r0:r0 + half, :].astype(BF16), w13_ref[0, 0]) for r0 in (0, half)]
        for n, r0 in enumerate((0, half)):
            hmid = (_silu(abs_[n][:, :F]) * abs_[n][:, F:]).astype(BF16)
            ys_ref[r0:r0 + half, :] = _dot(hmid, w2_ref[0, 0])

    @pl.when(i >= nused_ref[0])
    def _():
        ys_ref[...] = jnp.zeros_like(ys_ref)


def _experts(xs, blk_e, nused, w13, w2, layer):
    P, D = xs.shape
    R = R_MOE
    nb = P // R
    F = w2.shape[-2]
    grid_spec = pltpu.PrefetchScalarGridSpec(
        num_scalar_prefetch=2,
        grid=(nb,),
        in_specs=[pl.BlockSpec((R, D), lambda i, be, nu: (i, 0)),
                  pl.BlockSpec((1, 1, D, 2 * F), lambda i, be, nu: (layer, be[i], 0, 0)),
                  pl.BlockSpec((1, 1, F, D), lambda i, be, nu: (layer, be[i], 0, 0))],
        out_specs=pl.BlockSpec((R, D), lambda i, be, nu: (i, 0)),
    )
    return pl.pallas_call(
        functools.partial(_expert_kernel, F=F),
        grid_spec=grid_spec,
        out_shape=jax.ShapeDtypeStruct((P, D), F32),
        compiler_params=_cparams(("arbitrary",)),
        name="moe_experts",
    )(blk_e, nused, xs, w13, w2)


def _combine_kernel(pos_ref, posn_ref, ys_hbm, route_ref, x_ref, mod_ref, lng_ref, lnb_ref, o_ref,
                    ybuf, sems, *, TC, alpha):
    i = pl.program_id(0)
    n = pl.num_programs(0)
    slot = i % 2

    def issue(p_ref, s):
        for r in range(TC):
            for a in range(TOP_K):
                pltpu.make_async_copy(ys_hbm.at[pl.ds(p_ref[a, 0, 0, r], 1)],
                                      ybuf.at[s, a, pl.ds(r, 1)], sems.at[s]).start(priority=a % 2)

    @pl.when(i == 0)
    def _():
        issue(pos_ref, 0)

    for s in range(2):
        @pl.when((i + 1 < n) & (slot == 1 - s))
        def _(s=s):
            issue(posn_ref, s)

    for a in range(TOP_K):
        pltpu.make_async_copy(ys_hbm.at[pl.ds(0, TC)], ybuf.at[slot, a], sems.at[slot]).wait()

    m = mod_ref[0]
    w = route_ref[...]
    ffn = w[:, 0:1] * ybuf[slot, 0]
    for a in range(1, TOP_K):
        ffn = ffn + w[:, a:a + 1] * ybuf[slot, a]
    o_ref[...] = _layernorm(alpha * x_ref[...] + (1.0 + m[5:6]) * ffn, lng_ref[...], lnb_ref[...])


def _combine(ys, dest, route, x1, mod_l, lng, lnb, S, alpha):
    N, D = x1.shape
    TC = min(TC_MOE, S)
    nt = N // TC
    per_b = S // TC
    pos = dest.reshape(TOP_K, nt, 1, TC)
    return pl.pallas_call(
        functools.partial(_combine_kernel, TC=TC, alpha=alpha),
        grid=(nt,),
        in_specs=[pl.BlockSpec((TOP_K, 1, 1, TC), lambda i: (0, i, 0, 0), memory_space=pltpu.SMEM),
                  pl.BlockSpec((TOP_K, 1, 1, TC), lambda i: (0, jnp.minimum(i + 1, nt - 1), 0, 0),
                               memory_space=pltpu.SMEM),
                  pl.BlockSpec(memory_space=pl.ANY),
                  pl.BlockSpec((TC, V7X_LANES), lambda i: (i, 0)),
                  pl.BlockSpec((TC, D), lambda i: (i, 0)),
                  pl.BlockSpec((1, 6, D), lambda i: (i // per_b, 0, 0)),
                  pl.BlockSpec((1, D), lambda i: (0, 0)),
                  pl.BlockSpec((1, D), lambda i: (0, 0))],
        out_specs=pl.BlockSpec((TC, D), lambda i: (i, 0)),
        out_shape=jax.ShapeDtypeStruct((N, D), F32),
        scratch_shapes=[pltpu.VMEM((2, TOP_K, TC, D), F32), pltpu.SemaphoreType.DMA((2,))],
        compiler_params=_cparams(("arbitrary",)),
        name="moe_combine",
    )(pos, pos, ys, route, x1, mod_l, lng, lnb)


def _dispatch_plan(route, n_experts, R):
    lanes = jnp.arange(n_experts, dtype=jnp.int32)
    hots = [route[:, TOP_K + a].astype(jnp.int32)[:, None] == lanes for a in range(TOP_K)]
    hot = sum(h.astype(jnp.int32) for h in hots)
    csum = jnp.cumsum(hot, axis=0)
    counts = csum[-1]
    pcounts = (counts + R - 1) // R * R
    pends = jnp.cumsum(pcounts)
    pstarts = pends - pcounts
    base = pstarts[None, :] + csum - hot
    dest = jnp.stack([jnp.sum(jnp.where(h, base, 0), axis=1) for h in hots], 0).astype(jnp.int32)
    nb = (route.shape[0] * TOP_K + n_experts * R) // R
    starts = jnp.arange(nb, dtype=jnp.int32) * R
    blk_e = jnp.minimum((pends[None, :] <= starts[:, None]).astype(jnp.int32).sum(1), n_experts - 1)
    nused = (pends[-1:] // R).astype(jnp.int32)
    return blk_e, nused, dest


def _prep_weights(p):
    L = p['w_in'].shape[0]
    W = p['lru_conv_w'].shape[-1]
    nblk, bs = p['lru_wa'].shape[2], p['lru_wa'].shape[3]

    def blockdiag(w):
        eye = jnp.eye(nblk, dtype=w.dtype)
        return jnp.einsum('ldnij,nm->ldnimj', w, eye).reshape(L, 2, W, W)

    G, E = p['router_expert_w'].shape[1], p['router_expert_w'].shape[3]
    D = p['w_in'].shape[1]
    wr = jnp.concatenate([p['router_group_w'],
                          p['router_expert_w'].transpose(0, 2, 1, 3).reshape(L, D, G * E)], -1)
    wr = jnp.pad(wr, ((0, 0), (0, 0), (0, V7X_LANES - wr.shape[-1])))
    wr_hi = wr.astype(BF16)
    wr = jnp.concatenate([wr_hi, (wr - wr_hi.astype(F32)).astype(BF16)], -1)
    return dict(
        w_in=p['w_in'].astype(BF16),
        gq=jnp.tile(p['attn_q_gain'], (1, ATTN_HEADS))[:, None, :],
        gk=jnp.tile(p['attn_k_gain'], (1, KV_HEADS))[:, None, :],
        w_attn_o=p['w_attn_o'].astype(BF16),
        conv_w=p['lru_conv_w'], conv_b=p['lru_conv_b'][:, None, :],
        wbd=jnp.concatenate([blockdiag(p['lru_wa']), blockdiag(p['lru_wi'])], -1).astype(BF16),
        gb=jnp.concatenate([p['lru_ba'], p['lru_bi']], -1)[:, :, None, :],
        lam=p['lru_lambda'][:, :, None, :],
        w_lru_o=p['w_lru_o'].astype(BF16),
        w_hgrn_o=p['w_hgrn_o'].astype(BF16),
        w_out=p['w_out'].astype(BF16),
        ln1_g=p['ln1_g'][:, None, :], ln1_b=p['ln1_b'][:, None, :],
        ln2_g=p['ln2_g'][:, None, :], ln2_b=p['ln2_b'][:, None, :],
        wr=wr, n_groups=G, e_per=E,
        w13=p['expert_w13'].astype(BF16), w2=p['expert_w2'].astype(BF16),
    )


def _trunk(x, c, p, wp):
    B, S, D = x.shape
    L = p['w_in'].shape[0]
    alpha = (2 * L) ** 0.25
    QW = ATTN_HEADS * HEAD_DIM
    KW = KV_HEADS * HEAD_DIM
    LW = p['lru_conv_w'].shape[-1]
    HW = p['hgrn_lb_logits'].shape[-1]
    widths = (QW, 2 * KW, 2 * LW, 5 * HW, N_BRANCH * D)
    assert sum(widths) == p['w_in'].shape[-1]
    acts = (((0, None),), ((0, None),), ((0, None), (LW, _gelu_tanh)),
            ((0, _silu), (HW, None), (4 * HW, _silu)), ((0, None),))
    n_experts = wp['n_groups'] * wp['e_per']

    mod = _ada_mod(c, p['ada_w'], p['ada_b'])
    cos_t, sin_t = _rope_tables(S)
    bind, psw = _head_consts()
    cos_t, sin_t = jnp.asarray(cos_t), jnp.asarray(sin_t)
    bind, psw = jnp.asarray(bind, BF16), jnp.asarray(psw, BF16)

    xs = None
    for l in range(L):
        pq, pkv, plru, phg, pmg = _inproj(x, mod[l], wp['w_in'], l, widths, acts)
        qt, k, vt = _attn_prep(pq, pkv, cos_t, sin_t, wp['gq'][l], wp['gk'][l], bind, psw)
        ya = _attention(qt, k, vt)
        lru_args = (plru, wp['conv_w'][l], wp['conv_b'][l], wp['wbd'][l], wp['gb'][l], wp['lam'][l])
        hf = _lru_pass(*lru_args, None, reverse=False)
        yl = _lru_pass(*lru_args, hf, reverse=True)
        of = _hgrn_pass(phg, p['hgrn_lb_logits'], p['hgrn_norm_gain'][l], None, l, reverse=False)
        yh = _hgrn_pass(phg, p['hgrn_lb_logits'], p['hgrn_norm_gain'][l], of, l, reverse=True)
        x1, h2, route = _merge(ya, yl, yh, pmg, x, mod[l], wp['w_attn_o'][l], wp['w_lru_o'][l],
                               wp['w_hgrn_o'][l], wp['w_out'][l], wp['ln1_g'][l], wp['ln1_b'][l],
                               wp['wr'][l], wp['n_groups'], wp['e_per'], alpha)
        N = B * S
        route = route.reshape(N, V7X_LANES)
        blk_e, nused, dest = _dispatch_plan(route, n_experts, R_MOE)
        if xs is None:
            xs = jnp.zeros((N * TOP_K + n_experts * R_MOE, D), F32)
        xs = _dispatch(h2.reshape(N, D), dest, xs, S)
        ys = _experts(xs, blk_e, nused, wp['w13'], wp['w2'], l)
        x = _combine(ys, dest, route, x1.reshape(N, D), mod[l], wp['ln2_g'][l], wp['ln2_b'][l], S,
                     alpha).reshape(B, S, D)
    return x


def kernel(x_prompt, x_sample, c_prompt, c_sample, ada_w, ada_b, w_in, attn_q_gain, attn_k_gain, w_attn_o, lru_conv_w, lru_conv_b, lru_wa, lru_ba, lru_wi, lru_bi, lru_lambda, w_lru_o, hgrn_lb_logits, hgrn_norm_gain, w_hgrn_o, w_out, ln1_g, ln1_b, router_group_w, router_expert_w, expert_w13, expert_w2, ln2_g, ln2_b):
    p = dict(ada_w=ada_w, ada_b=ada_b, w_in=w_in, attn_q_gain=attn_q_gain, attn_k_gain=attn_k_gain,
             w_attn_o=w_attn_o, lru_conv_w=lru_conv_w, lru_conv_b=lru_conv_b, lru_wa=lru_wa, lru_ba=lru_ba,
             lru_wi=lru_wi, lru_bi=lru_bi, lru_lambda=lru_lambda, w_lru_o=w_lru_o,
             hgrn_lb_logits=hgrn_lb_logits, hgrn_norm_gain=hgrn_norm_gain, w_hgrn_o=w_hgrn_o, w_out=w_out,
             ln1_g=ln1_g, ln1_b=ln1_b, router_group_w=router_group_w, router_expert_w=router_expert_w,
             expert_w13=expert_w13, expert_w2=expert_w2, ln2_g=ln2_g, ln2_b=ln2_b)
    wp = _prep_weights(p)
    return (_trunk(x_prompt, c_prompt, p, wp), _trunk(x_sample, c_sample, p, wp))
```
